```python
import math
import jax, jax.numpy as jnp
from jax import lax
import numpy as np

D_MODEL = 1024
BATCH = 16
SEQ = 256
DEPTH = 4
DEC_BATCH = 2
DEC_SEQ = 2048
PAST_LEN = 256

GRID_W = 64
W_MIX = 512
N_BRANCH = 4
CONV_A = 3
HG_HEADS = 4
HG_DIM = 128
HG_CHUNK = 16
LRU_BLOCKS = 8
LRU_BLOCK = W_MIX // LRU_BLOCKS
CONV_C = 4
LRU_C = 8.0
SG_CHUNK = 128
SG_GROUPS = 4
SG_GROUP = W_MIX // SG_GROUPS
N_GROUPS = 4
EXP_PER_GROUP = 4
N_EXPERTS = N_GROUPS * EXP_PER_GROUP
TOP_K = 2
D_EXPERT = 512
EPS = 1e-6

A_IN = 0
A_B = W_MIX
A_C = 2 * W_MIX
B_Q = 3 * W_MIX
B_FF = 4 * W_MIX
B_FB = 5 * W_MIX
B_I = 6 * W_MIX
B_G = 7 * W_MIX
C_X = 8 * W_MIX
C_Y = 9 * W_MIX
D_UV = 10 * W_MIX
GATE = 12 * W_MIX
D_IN = 12 * W_MIX + N_BRANCH * D_MODEL

kernel_name = "hybrid_diffusion_parallel_branch_step"

F32 = jnp.float32


def rmsnorm(x, g):
    xf = x.astype(F32)
    return xf * lax.rsqrt(jnp.mean(xf * xf, axis=-1, keepdims=True) + EPS) * g.astype(F32)


def grid_pos_embed(rows, d):
    nf = d // 4
    freqs = jnp.exp(-math.log(10000.0) * jnp.arange(nf, dtype=F32) / nf)
    r = jnp.arange(rows, dtype=F32)[:, None] * freqs
    cl = jnp.arange(GRID_W, dtype=F32)[:, None] * freqs
    r_emb = jnp.concatenate([jnp.sin(r), jnp.cos(r)], axis=-1)
    c_emb = jnp.concatenate([jnp.sin(cl), jnp.cos(cl)], axis=-1)
    emb = jnp.concatenate([jnp.broadcast_to(r_emb[:, None], (rows, GRID_W, d // 2)),
                           jnp.broadcast_to(c_emb[None], (rows, GRID_W, d // 2))], axis=-1)
    return emb.reshape(rows * GRID_W, d)


def dwconv(x, w, pad):
    return lax.conv_general_dilated(x, w[:, None, :].astype(x.dtype), window_strides=(1,), padding=[pad],
                                    dimension_numbers=("NWC", "WIO", "NWC"), feature_group_count=x.shape[-1])


def shortconv_branch(z_in, z_b, z_c, conv_w):
    return z_b * dwconv(z_c * z_in, conv_w, (1, 1))


def hgrn_lower_bounds(lb_logits):
    cs = jnp.cumsum(jax.nn.softmax(lb_logits.astype(F32), axis=1), axis=1)
    return cs - cs[:, :1]


def hgrn2_chunkwise(q, k, v, log_f, s0):
    B, L, H, K = q.shape
    V = v.shape[-1]
    n = L // HG_CHUNK
    q, k, v, log_f = (t.reshape(B, n, HG_CHUNK, H, t.shape[-1]) for t in (q, k, v, log_f))
    b = jnp.cumsum(log_f, axis=2)
    b_last = b[:, :, -1]
    causal = jnp.tril(jnp.ones((HG_CHUNK, HG_CHUNK), dtype=bool))[None, None, :, :, None, None]
    decay = jnp.exp(jnp.where(causal, b[:, :, :, None] - b[:, :, None], -jnp.inf))
    scores = jnp.einsum("bnihk,bnjhk,bnijhk->bnhij", q, k, decay)
    o_intra = jnp.einsum("bnhij,bnjhv->bnihv", scores, v)
    d_state = jnp.einsum("bnchk,bnchv->bnhkv", k * jnp.exp(b_last[:, :, None] - b), v)

    def step(S, inp):
        dec, ds = inp
        return dec[..., None] * S + ds, S

    s_fin, s_in = lax.scan(step, s0, (jnp.moveaxis(jnp.exp(b_last), 1, 0), jnp.moveaxis(d_state, 1, 0)))
    s_in = jnp.moveaxis(s_in, 0, 1)
    o_inter = jnp.einsum("bnchk,bnhkv->bnchv", q * jnp.exp(b), s_in)
    return (o_intra + o_inter).reshape(B, L, H, V), s_fin


def hgrn2_branch(z_q, z_f_fwd, z_f_bwd, z_i, z_g, lb, g_norm, s0):
    B, L, _ = z_q.shape
    heads = lambda t: t.reshape(B, L, HG_HEADS, HG_DIM)
    q = heads(jax.nn.silu(z_q))
    v = heads(z_i)
    outs, finals = [], []
    for d, z_f in enumerate((z_f_fwd, z_f_bwd)):
        lbd = lb[d].reshape(HG_HEADS, HG_DIM)
        zf = heads(z_f)
        log_f = jnp.logaddexp(jnp.log(lbd), jnp.log1p(-lbd) + jax.nn.log_sigmoid(zf))
        k = (1.0 - lbd) * jax.nn.sigmoid(-zf)
        seqs = (q, k, v, log_f)
        if d == 1:
            seqs = tuple(jnp.flip(t, axis=1) for t in seqs)
        o_d, s_d = hgrn2_chunkwise(*seqs, s0[:, d].astype(F32))
        if d == 1:
            o_d = jnp.flip(o_d, axis=1)
        outs.append(o_d)
        finals.append(s_d)
    o = outs[0] + outs[1]
    o = o * lax.rsqrt(jnp.mean(o * o, axis=-1, keepdims=True) + EPS)
    y = o.reshape(B, L, W_MIX) * g_norm.astype(F32) * jax.nn.silu(z_g)
    return y, jnp.stack(finals, axis=1)


def _lin_combine(e1, e2):
    a1, b1 = e1
    a2, b2 = e2
    return a1 * a2, a2 * b1 + b2


def rglru_branch(z_x, z_y, conv_w, conv_b, wa, ba, wx, bx, lam, h0):
    B, L, _ = z_x.shape
    xc = (dwconv(z_x, conv_w, (2, 1)) + conv_b).astype(F32)
    xb = xc.reshape(B, L, LRU_BLOCKS, LRU_BLOCK)
    h0 = h0.astype(F32)
    hs, finals = [], []
    for d in range(2):
        r = jax.nn.sigmoid(jnp.einsum("blnc,ncd->blnd", xb, wa[d]).reshape(B, L, W_MIX) + ba[d])
        i = jax.nn.sigmoid(jnp.einsum("blnc,ncd->blnd", xb, wx[d]).reshape(B, L, W_MIX) + bx[d])
        log_a = -LRU_C * r * jax.nn.softplus(-lam[d])
        a = jnp.exp(log_a)
        u = jnp.sqrt(-jnp.expm1(2.0 * log_a)) * (i * xc)
        rev = d == 1
        edge = -1 if rev else 0
        u = u.at[:, edge].add(a[:, edge] * h0[:, d])
        _, h = lax.associative_scan(_lin_combine, (a, u), reverse=rev, axis=1)
        hs.append(h)
        finals.append(h[:, 0] if rev else h[:, -1])
    y = (hs[0] + hs[1]) * jax.nn.gelu(z_y)
    return y, jnp.stack(finals, axis=1)


def sgu_branch(z_uv, ln_g, ln_b, w_s, b_s):
    B, L, _ = z_uv.shape
    u, v = jnp.split(jax.nn.gelu(z_uv), 2, axis=-1)
    mu = jnp.mean(v, axis=-1, keepdims=True)
    var = jnp.mean(jnp.square(v - mu), axis=-1, keepdims=True)
    v = (v - mu) * lax.rsqrt(var + EPS) * ln_g + ln_b
    v = v.reshape(B, L // SG_CHUNK, SG_CHUNK, SG_GROUPS, SG_GROUP)
    v = jnp.einsum("gts,bnsgc->bntgc", w_s, v) + b_s.T[:, :, None]
    return u * v.reshape(B, L, W_MIX)


def hier_moe(h, w_rg, b_rg, w_re, b_re, w_gate, w_up, w_down):
    g_logits = h @ w_rg + b_rg
    g_prob = jax.nn.softmax(g_logits.astype(F32), axis=-1)
    _, g_idx = lax.top_k(g_logits, 1)
    g_sel = g_idx[..., 0]
    g_hot = jax.nn.one_hot(g_sel, N_GROUPS, dtype=F32)
    e_logits = (h @ w_re + b_re).reshape(*h.shape[:-1], N_GROUPS, EXP_PER_GROUP)
    e_in = jnp.einsum("blge,blg->ble", e_logits.astype(F32), g_hot)
    top_p, top_i = lax.top_k(jax.nn.softmax(e_in, axis=-1), TOP_K)
    top_p = top_p / jnp.sum(top_p, axis=-1, keepdims=True) * jnp.sum(g_prob * g_hot, axis=-1, keepdims=True)
    expert_id = g_sel[..., None] * EXP_PER_GROUP + top_i
    combine = jnp.sum(jax.nn.one_hot(expert_id, N_EXPERTS, dtype=F32) * top_p[..., None], axis=-2)
    a = jnp.einsum("bld,edf->blef", h, w_gate)
    u = jnp.einsum("bld,edf->blef", h, w_up)
    return jnp.einsum("blef,efd->bld", jax.nn.silu(a) * u * combine[..., None], w_down)


def trunk_layer(x, c_act, p, hg0, lru0):
    mod = (c_act @ p["w_ada"] + p["b_ada"])[:, None, :]
    sh1, sc1, g1, sh2, sc2, g2 = jnp.split(mod, 6, axis=-1)
    h = rmsnorm(x, p["norm1"]) * (1.0 + sc1) + sh1
    z = h @ p["w_in"]
    W = W_MIX
    ya = shortconv_branch(z[..., A_IN:A_IN + W], z[..., A_B:A_B + W], z[..., A_C:A_C + W], p["conv_a"])
    yb, hg_fin = hgrn2_branch(z[..., B_Q:B_Q + W], z[..., B_FF:B_FF + W], z[..., B_FB:B_FB + W],
                              z[..., B_I:B_I + W], z[..., B_G:B_G + W], p["lb"], p["hg_norm"], hg0)
    yc, lru_fin = rglru_branch(z[..., C_X:C_X + W], z[..., C_Y:C_Y + W], p["lru_conv_w"], p["lru_conv_b"],
                               p["lru_wa"], p["lru_ba"], p["lru_wx"], p["lru_bx"], p["lru_lam"], lru0)
    yd = sgu_branch(z[..., D_UV:D_UV + 2 * W], p["sg_ln_g"], p["sg_ln_b"], p["sg_w"], p["sg_b"])
    branches = jnp.stack([ya.astype(F32), yb, yc, yd], axis=2)
    proj = jnp.einsum("blkw,kwd->blkd", branches, p["w_br"])
    gates = jax.nn.sigmoid(z[..., GATE:].reshape(*z.shape[:-1], N_BRANCH, D_MODEL))
    x = x + g1 * (jnp.sum(gates * proj, axis=2) @ p["w_out"])
    h2 = rmsnorm(x, p["norm2"]) * (1.0 + sc2) + sh2
    x = x + g2 * hier_moe(h2, p["w_rg"], p["b_rg"], p["w_re"], p["b_re"], p["w_gate"], p["w_up"], p["w_down"])
    return x, hg_fin, lru_fin


def setup_inputs(seed: int = 0) -> dict:
    key = jax.random.key(seed)
    ks = iter(jax.random.split(key, 40))
    nrm = lambda shape, s: jax.random.normal(next(ks), shape, F32) * s
    W = W_MIX
    x_prompt = nrm((BATCH, SEQ, D_MODEL), 1.0)
    x_sample = nrm((DEC_BATCH, DEC_SEQ, D_MODEL), 1.0)
    state_hgrn = nrm((DEC_BATCH, DEPTH, 2, HG_HEADS, HG_DIM, HG_DIM), 0.3)
    state_lru = nrm((DEC_BATCH, DEPTH, 2, W), 0.5)
    c = nrm((DEC_BATCH, D_MODEL), 1.0)
    c_ctx = nrm((D_MODEL,), 1.0)
    w_ada = nrm((DEPTH, D_MODEL, 6 * D_MODEL), D_MODEL ** -0.5)
    b_ada = nrm((DEPTH, 6 * D_MODEL), 0.01)
    norm1 = 1.0 + nrm((DEPTH, D_MODEL), 0.01)
    norm2 = 1.0 + nrm((DEPTH, D_MODEL), 0.01)
    w_in = nrm((DEPTH, D_MODEL, D_IN), D_MODEL ** -0.5)
    conv_a = nrm((DEPTH, CONV_A, W), CONV_A ** -0.5)
    hg_lb = nrm((2, DEPTH, W), 0.5)
    hg_norm = 1.0 + nrm((DEPTH, W), 0.01)
    lru_conv_w = nrm((DEPTH, CONV_C, W), CONV_C ** -0.5)
    lru_conv_b = nrm((DEPTH, W), 0.01)
    lru_wa = nrm((DEPTH, 2, LRU_BLOCKS, LRU_BLOCK, LRU_BLOCK), LRU_BLOCK ** -0.5)
    lru_ba = nrm((DEPTH, 2, W), 0.01)
    lru_wx = nrm((DEPTH, 2, LRU_BLOCKS, LRU_BLOCK, LRU_BLOCK), LRU_BLOCK ** -0.5)
    lru_bx = nrm((DEPTH, 2, W), 0.01)
    a0 = jax.random.uniform(next(ks), (DEPTH, 2, W), F32, 0.9, 0.999)
    s = a0 ** (1.0 / LRU_C)
    lru_lam = jnp.log(s) - jnp.log1p(-s)
    sg_ln_g = 1.0 + nrm((DEPTH, W), 0.01)
    sg_ln_b = nrm((DEPTH, W), 0.01)
    sg_w = nrm((DEPTH, SG_GROUPS, SG_CHUNK, SG_CHUNK), SG_CHUNK ** -0.5)
    sg_b = 1.0 + nrm((DEPTH, SG_GROUPS, SG_CHUNK), 0.01)
    w_br = nrm((DEPTH, N_BRANCH, W, D_MODEL), W ** -0.5)
    w_out = nrm((DEPTH, D_MODEL, D_MODEL), D_MODEL ** -0.5)
    w_rg = nrm((DEPTH, D_MODEL, N_GROUPS), D_MODEL ** -0.5)
    b_rg = nrm((DEPTH, N_GROUPS), 0.01)
    w_re = nrm((DEPTH, D_MODEL, N_EXPERTS), D_MODEL ** -0.5)
    b_re = nrm((DEPTH, N_EXPERTS), 0.01)
    w_gate = nrm((DEPTH, N_EXPERTS, D_MODEL, D_EXPERT), D_MODEL ** -0.5)
    w_up = nrm((DEPTH, N_EXPERTS, D_MODEL, D_EXPERT), D_MODEL ** -0.5)
    w_down = nrm((DEPTH, N_EXPERTS, D_EXPERT, D_MODEL), D_EXPERT ** -0.5)
    norm_f = 1.0 + nrm((D_MODEL,), 0.01)
    return {"x_prompt": x_prompt, "x_sample": x_sample, "state_hgrn": state_hgrn, "state_lru": state_lru,
            "c": c, "c_ctx": c_ctx, "w_ada": w_ada, "b_ada": b_ada, "norm1": norm1, "norm2": norm2,
            "w_in": w_in, "conv_a": conv_a, "hg_lb": hg_lb, "hg_norm": hg_norm,
            "lru_conv_w": lru_conv_w, "lru_conv_b": lru_conv_b, "lru_wa": lru_wa, "lru_ba": lru_ba,
            "lru_wx": lru_wx, "lru_bx": lru_bx, "lru_lam": lru_lam, "sg_ln_g": sg_ln_g, "sg_ln_b": sg_ln_b,
            "sg_w": sg_w, "sg_b": sg_b, "w_br": w_br, "w_out": w_out, "w_rg": w_rg, "b_rg": b_rg,
            "w_re": w_re, "b_re": b_re, "w_gate": w_gate, "w_up": w_up, "w_down": w_down, "norm_f": norm_f}


def reference(x_prompt, x_sample, state_hgrn, state_lru, c, c_ctx, w_ada, b_ada, norm1, norm2, w_in, conv_a,
              hg_lb, hg_norm, lru_conv_w, lru_conv_b, lru_wa, lru_ba, lru_wx, lru_bx, lru_lam, sg_ln_g, sg_ln_b,
              sg_w, sg_b, w_br, w_out, w_rg, b_rg, w_re, b_re, w_gate, w_up, w_down, norm_f):
    lb_all = hgrn_lower_bounds(hg_lb)
    ctx_act = jax.nn.silu(c_ctx.astype(F32))[None]
    lat_act = jax.nn.silu(c.astype(F32))
    rows = x_sample.shape[1] // GRID_W
    xs = x_sample.astype(F32) + grid_pos_embed(rows, D_MODEL)
    xp = x_prompt.astype(F32)
    nb = xp.shape[0]
    zero_hg = jnp.zeros((nb, 2, HG_HEADS, HG_DIM, HG_DIM), F32)
    zero_lru = jnp.zeros((nb, 2, W_MIX), F32)
    hg_states, lru_states = [], []
    for l in range(DEPTH):
        p = {"w_ada": w_ada[l], "b_ada": b_ada[l], "norm1": norm1[l], "norm2": norm2[l], "w_in": w_in[l],
             "conv_a": conv_a[l], "lb": lb_all[:, l], "hg_norm": hg_norm[l],
             "lru_conv_w": lru_conv_w[l], "lru_conv_b": lru_conv_b[l], "lru_wa": lru_wa[l], "lru_ba": lru_ba[l],
             "lru_wx": lru_wx[l], "lru_bx": lru_bx[l], "lru_lam": lru_lam[l], "sg_ln_g": sg_ln_g[l],
             "sg_ln_b": sg_ln_b[l], "sg_w": sg_w[l], "sg_b": sg_b[l], "w_br": w_br[l], "w_out": w_out[l],
             "w_rg": w_rg[l], "b_rg": b_rg[l], "w_re": w_re[l], "b_re": b_re[l], "w_gate": w_gate[l],
             "w_up": w_up[l], "w_down": w_down[l]}
        xp, hg_fin, lru_fin = trunk_layer(xp, ctx_act, p, zero_hg, zero_lru)
        hg_states.append(hg_fin)
        lru_states.append(lru_fin)
        xs, _, _ = trunk_layer(xs, lat_act, p, state_hgrn[:, l], state_lru[:, l])
    y_prompt = rmsnorm(xp, norm_f)
    y_sample = rmsnorm(xs, norm_f)
    new_state_hgrn = jnp.stack(hg_states, axis=1)
    new_state_lru = jnp.stack(lru_states, axis=1)
    return (y_prompt, y_sample, new_state_hgrn, new_state_lru)
```

```python
import functools
import math

import numpy as np
import jax
import jax.numpy as jnp
from jax import lax
from jax.experimental import pallas as pl
from jax.experimental.pallas import tpu as pltpu

F32 = jnp.float32
BF16 = jnp.bfloat16

D = 1024
W = 512
DEPTH = 4
GRID_W = 64
N_HEADS = 4
HD = 128
LRU_BLOCKS = 8
LRU_BLOCK = W // LRU_BLOCKS
LRU_C = 8.0
SG_CHUNK = 128
SG_GROUPS = 4
N_GROUPS = 4
EXP_PER_GROUP = 4
N_EXPERTS = 16
D_EXPERT = 512
EPS = 1e-6
D_IN = 12 * W + 4 * D

S_A_IN, S_A_B, S_A_C, S_B_Q, S_B_FF, S_B_FB, S_B_I, S_B_G, S_C_X, S_C_Y, S_D_U, S_D_V = range(12)
GATE_BLK = (12 * W) // D

N_CTX, L_CTX = 16, 256
N_LAT, L_LAT = 2, 2048
T = N_CTX * L_CTX + N_LAT * L_LAT
TL = 256
NT = T // TL
CTX_TILES = (N_CTX * L_CTX) // TL
LAT_TILES = L_LAT // TL
LOG_TL = 8
N_LEVELS = LOG_TL
HALO = 16
ROUTE_W = 128
EXP_LANE0 = 4

VMEM_LIMIT = 56 * 1024 * 1024

TM_IN, TN_IN = 1024, 1024
TM_MOE = 1024
TN_MOD = 1536


def _scan_constants(reverse):
    i = np.arange(TL)[:, None]
    s = np.arange(TL)[None, :]
    mats = []
    B = TL
    while B >= 2:
        m = (i // B) * B + B // 2 - 1
        mats.append(np.where(i > m, (s > m) & (s <= i), (s > i) & (s <= m)))
        B //= 2
    mats.append(s <= i)
    mats.append(s > i)
    if reverse:
        mats = [g[::-1, ::-1] for g in mats]
    G = np.concatenate(mats, 0).astype(np.float32)
    x = i ^ s
    hb = np.floor(np.log2(np.maximum(x, 1))).astype(np.int32)
    lev = (LOG_TL - 1) - hb
    lev = np.where(i == s, N_LEVELS, lev)
    causal = (i < s) if reverse else (i > s)
    lev = np.where(causal | (i == s), lev, N_LEVELS + 1).astype(np.int32)
    return G, lev


def _split2(x):
    hi = x.astype(BF16)
    lo = (x - hi.astype(F32)).astype(BF16)
    return hi, lo


def _dot(a, b):
    return jnp.dot(a, b, preferred_element_type=F32)


def _dot_nt(a, b):
    return lax.dot_general(a, b, (((1,), (1,)), ((), ())), preferred_element_type=F32)


def _dot_split(a, b):
    ah, al = _split2(a)
    bh, bl = _split2(b)
    return _dot(ah, bh) + (_dot(ah, bl) + _dot(al, bh))


def _log_sigmoid(x):
    return jnp.minimum(x, 0.0) - jnp.log1p(jnp.exp(-jnp.abs(x)))


def _softplus(x):
    return jnp.maximum(x, 0.0) + jnp.log1p(jnp.exp(-jnp.abs(x)))


def _tile_flags(tt):
    is_lat = tt >= CTX_TILES
    pos = tt & (LAT_TILES - 1)
    first = jnp.logical_or(jnp.logical_not(is_lat), pos == 0)
    last = jnp.logical_or(jnp.logical_not(is_lat), pos == LAT_TILES - 1)
    return first, last


def _seq_idx(t):
    return jnp.where(t < CTX_TILES, 0, 1 + (t - CTX_TILES) // LAT_TILES)


def _shifted(x, prev, nxt, k):
    rows = lax.broadcasted_iota(jnp.int32, x.shape, 0)
    if k == 0:
        return x
    if k < 0:
        y = pltpu.roll(x, -k, 0)
        for r in range(-k):
            src = prev[HALO + k + r:HALO + k + r + 1, :]
            y = jnp.where(rows == r, src, y)
        return y
    y = pltpu.roll(x, TL - k, 0)
    for r in range(k):
        src = nxt[r:r + 1, :]
        y = jnp.where(rows == TL - k + r, src, y)
    return y


def _hgrn_tile(q, zf, v, lb, G_ref, lev, s_ref, reverse):
    ls = _log_sigmoid(zf)
    bb = jnp.log1p(-lb) + ls
    la = jnp.log(lb)
    lf = jnp.maximum(la, bb) + jnp.log1p(jnp.exp(-jnp.abs(la - bb)))
    kk = (1.0 - lb) * jax.nn.sigmoid(-zf)
    end_row = 0 if reverse else TL - 1
    outs = []
    for h in range(N_HEADS):
        sl = slice(h * HD, (h + 1) * HD)
        lf_hi, lf_lo = _split2(lf[:, sl])
        g = G_ref[...]
        e_all = jnp.exp(_dot(g, lf_hi) + _dot(g, lf_lo))
        q_h, k_h, v_h = q[:, sl], kk[:, sl], v[:, sl]
        v_b = v_h.astype(BF16)
        scores = jnp.zeros((TL, TL), F32)
        for l in range(N_LEVELS):
            e = e_all[l * TL:(l + 1) * TL]
            p = _dot_nt((q_h * e).astype(BF16), (k_h * e).astype(BF16))
            scores = jnp.where(lev == l, p, scores)
        e_b = e_all[N_LEVELS * TL:(N_LEVELS + 1) * TL]
        e_e = e_all[(N_LEVELS + 1) * TL:(N_LEVELS + 2) * TL]
        st = s_ref[h]
        o = _dot(scores.astype(BF16), v_b)
        o = o + jnp.sum(q_h * k_h, axis=-1, keepdims=True) * v_h
        o = o + _dot_nt((q_h * e_b).astype(BF16), st.astype(BF16))
        upd = _dot(v_h.T.astype(BF16), (k_h * e_e).astype(BF16))
        s_ref[h] = e_b[end_row:end_row + 1, :] * st + upd
        outs.append(o)
    return jnp.concatenate(outs, axis=-1)


def _lru_tile(zx, zx_p, zx_n, cw, cb, wax, bax, lam, hl_ref, reverse):
    xc = cb + sum(cw[k:k + 1, :] * _shifted(zx, zx_p, zx_n, k - 2) for k in range(4))
    rx = _dot(xc.astype(BF16), wax) + bax
    r = jax.nn.sigmoid(rx[:, :W])
    ig = jax.nn.sigmoid(rx[:, W:])
    log_a = -LRU_C * r * _softplus(-lam)
    a = jnp.exp(log_a)
    u = jnp.sqrt(-jnp.tanh(log_a) * (1.0 + a * a)) * (ig * xc)
    rows = lax.broadcasted_iota(jnp.int32, a.shape, 0)
    d = 1
    while d < TL:
        if reverse:
            a_s, u_s, ok = pltpu.roll(a, TL - d, 0), pltpu.roll(u, TL - d, 0), rows < TL - d
        else:
            a_s, u_s, ok = pltpu.roll(a, d, 0), pltpu.roll(u, d, 0), rows >= d
        u = jnp.where(ok, a * u_s + u, u)
        a = jnp.where(ok, a * a_s, a)
        d *= 2
    hcar = hl_ref[...]
    hseq = u + a * hcar
    end_row = 0 if reverse else TL - 1
    hl_ref[...] = hseq[end_row:end_row + 1, :]
    return hseq


def _mod_kernel(c_ref, w_ref, b_ref, o_ref):
    c = c_ref[...]
    act = c * jax.nn.sigmoid(c)
    o_ref[...] = _dot_split(act, w_ref[...]) + b_ref[...]


def _modulation(cond, w_ada, b_ada):
    n = 6 * D // TN_MOD
    return pl.pallas_call(
        _mod_kernel,
        grid=(DEPTH, n),
        in_specs=[pl.BlockSpec((8, D), lambda l, j: (0, 0)),
                  pl.BlockSpec((None, D, TN_MOD), lambda l, j: (l, 0, j)),
                  pl.BlockSpec((None, 1, TN_MOD), lambda l, j: (l, 0, j))],
        out_specs=pl.BlockSpec((None, 8, TN_MOD), lambda l, j: (l, 0, j)),
        out_shape=jax.ShapeDtypeStruct((DEPTH, 8, 6 * D), F32),
        compiler_params=pltpu.CompilerParams(dimension_semantics=("arbitrary", "arbitrary"),
                                             vmem_limit_bytes=VMEM_LIMIT),
        name="modulation",
    )(cond, w_ada, b_ada.reshape(DEPTH, 1, 6 * D))


def _in_kernel(x_ref, mod_ref, g_ref, w_ref, z_ref, h_scr):
    @pl.when(pl.program_id(1) == 0)
    def _():
        x = x_ref[...]
        h = x * lax.rsqrt(jnp.mean(x * x, axis=-1, keepdims=True) + EPS) * g_ref[...]
        h = h * (1.0 + mod_ref[1:2, :]) + mod_ref[0:1, :]
        h_scr[...] = h.astype(BF16)

    z_ref[...] = _dot(h_scr[...], w_ref[...].astype(BF16)).astype(BF16)


def _in_projection(l, x, mod, norm1, w_in):
    tiles_per_seq = L_LAT // TM_IN
    ctx = (N_CTX * L_CTX) // TM_IN

    def seq(i):
        return jnp.where(i < ctx, 0, 1 + (i - ctx) // tiles_per_seq)

    return pl.pallas_call(
        _in_kernel,
        grid=(T // TM_IN, D_IN // TN_IN),
        in_specs=[pl.BlockSpec((TM_IN, D), lambda i, j: (i, 0)),
                  pl.BlockSpec((None, None, 6, D), lambda i, j: (l, seq(i), 0, 0)),
                  pl.BlockSpec((None, 1, D), lambda i, j: (l, 0, 0)),
                  pl.BlockSpec((None, D, TN_IN), lambda i, j: (l, 0, j))],
        out_specs=pl.BlockSpec((TM_IN, TN_IN), lambda i, j: (i, j)),
        out_shape=jax.ShapeDtypeStruct((T, D_IN), BF16),
        scratch_shapes=[pltpu.VMEM((TM_IN, D), BF16)],
        compiler_params=pltpu.CompilerParams(dimension_semantics=("arbitrary", "arbitrary"),
                                             vmem_limit_bytes=VMEM_LIMIT),
        name="in_projection",
    )(x, mod, norm1, w_in)


def _bwd_kernel(zq, zf, zi, zx, zx_p, zx_n, G_ref, lev_ref, lb_ref, cw_ref, cb_ref, wax_ref, bax_ref,
                lam_ref, hg0_ref, lru0_ref,
                o_ref, h_ref, hgs_ref, lrus_ref, s_scr, hl_scr):
    tt = NT - 1 - pl.program_id(0)
    first, last = _tile_flags(tt)

    @pl.when(last)
    def _():
        for h in range(N_HEADS):
            s_scr[h] = hg0_ref[h].T
        hl_scr[...] = lru0_ref[...]

    has_prev = jnp.where(first, 0.0, 1.0)
    has_next = jnp.where(last, 0.0, 1.0)
    zqv = zq[...].astype(F32)
    q = zqv * jax.nn.sigmoid(zqv)
    o_ref[...] = _hgrn_tile(q, zf[...].astype(F32), zi[...].astype(F32), lb_ref[...], G_ref, lev_ref[...],
                            s_scr, True)
    h_ref[...] = _lru_tile(zx[...].astype(F32), zx_p[...].astype(F32) * has_prev,
                           zx_n[...].astype(F32) * has_next, cw_ref[...], cb_ref[...], wax_ref[...],
                           bax_ref[...], lam_ref[...], hl_scr, True)
    for h in range(N_HEADS):
        hgs_ref[h] = s_scr[h].T
    lrus_ref[...] = hl_scr[...]


def _zslab(k, tile):
    return pl.BlockSpec((TL, W), lambda s: (tile(s), k))


def _zprev(k, tile):
    return pl.BlockSpec((HALO, W), lambda s: (jnp.maximum(tile(s) * (TL // HALO) - 1, 0), k))


def _znext(k, tile):
    return pl.BlockSpec((HALO, W), lambda s: (jnp.minimum((tile(s) + 1) * (TL // HALO), T // HALO - 1), k))


def _const(shape):
    nd = len(shape)
    return pl.BlockSpec(shape, lambda s: (0,) * nd)


def _layer_row(l, n):
    return pl.BlockSpec((None, 1, n), lambda s: (l, 0, 0))


def _reverse_pass(l, z, G, lev, lb, cw, cb, wax, bax, lam, hg_init, lru_init):
    tile = lambda s: NT - 1 - s
    state_specs = [
        pl.BlockSpec((None, None, N_HEADS, HD, HD), lambda s: (_seq_idx(tile(s)), 1, 0, 0, 0)),
        pl.BlockSpec((None, None, 1, W), lambda s: (_seq_idx(tile(s)), 1, 0, 0)),
    ]
    return pl.pallas_call(
        _bwd_kernel,
        grid=(NT,),
        in_specs=[_zslab(S_B_Q, tile), _zslab(S_B_FB, tile), _zslab(S_B_I, tile), _zslab(S_C_X, tile),
                  _zprev(S_C_X, tile), _znext(S_C_X, tile),
                  _const(G.shape), _const(lev.shape),
                  pl.BlockSpec((None, None, 1, W), lambda s: (1, l, 0, 0)),
                  pl.BlockSpec((None, 4, W), lambda s: (l, 0, 0)),
                  _layer_row(l, W),
                  pl.BlockSpec((None, None, W, 2 * W), lambda s: (l, 1, 0, 0)),
                  pl.BlockSpec((None, None, 1, 2 * W), lambda s: (l, 1, 0, 0)),
                  pl.BlockSpec((None, None, 1, W), lambda s: (l, 1, 0, 0)),
                  ] + state_specs,
        out_specs=[pl.BlockSpec((TL, W), lambda s: (tile(s), 0)),
                   pl.BlockSpec((TL, W), lambda s: (tile(s), 0)),
                   pl.BlockSpec((None, N_HEADS, HD, HD), lambda s: (tile(s), 0, 0, 0)),
                   pl.BlockSpec((None, 1, W), lambda s: (tile(s), 0, 0))],
        out_shape=[jax.ShapeDtypeStruct((T, W), F32), jax.ShapeDtypeStruct((T, W), F32),
                   jax.ShapeDtypeStruct((NT, N_HEADS, HD, HD), F32),
                   jax.ShapeDtypeStruct((NT, 1, W), F32)],
        scratch_shapes=[pltpu.VMEM((N_HEADS, HD, HD), F32), pltpu.VMEM((1, W), F32)],
        compiler_params=pltpu.CompilerParams(dimension_semantics=("arbitrary",),
                                             vmem_limit_bytes=VMEM_LIMIT),
        name="reverse_pass",
    )(z, z, z, z, z, z, G, lev, lb, cw, cb, wax, bax, lam, hg_init, lru_init)


def _route(h2, wr, br):
    logits = _dot_split(h2, wr) + br
    lane = lax.broadcasted_iota(jnp.int32, logits.shape, 1)
    lane_f = lane.astype(F32)
    neg = jnp.float32(-jnp.inf)
    is_g = lane < N_GROUPS
    gl = jnp.where(is_g, logits, neg)
    gmax = jnp.max(gl, axis=-1, keepdims=True)
    g_sel = jnp.min(jnp.where(gl == gmax, lane_f, float(ROUTE_W)), axis=-1, keepdims=True)
    g_p = 1.0 / jnp.sum(jnp.where(is_g, jnp.exp(gl - gmax), 0.0), axis=-1, keepdims=True)
    e_grp = ((lane - EXP_LANE0) >> 2).astype(F32)
    in_grp = (lane >= EXP_LANE0) & (lane < EXP_LANE0 + N_EXPERTS) & (e_grp == g_sel)
    el = jnp.where(in_grp, logits, neg)
    m1 = jnp.max(el, axis=-1, keepdims=True)
    i1 = jnp.min(jnp.where(el == m1, lane_f, float(ROUTE_W)), axis=-1, keepdims=True)
    el2 = jnp.where(lane_f == i1, neg, el)
    m2 = jnp.max(el2, axis=-1, keepdims=True)
    i2 = jnp.min(jnp.where(el2 == m2, lane_f, float(ROUTE_W)), axis=-1, keepdims=True)
    t = jnp.exp(m2 - m1)
    w1 = g_p / (1.0 + t)
    w2 = w1 * t
    return jnp.where(lane_f == i1, w1, 0.0) + jnp.where(lane_f == i2, w2, 0.0)


def _fwd_kernel(x_ref, mod_ref,
                za_in, za_b, za_c, za_in_p, za_in_n, za_c_p, za_c_n,
                zq, zf, zi, zg, zx, zx_p, zx_n, zy, zu, zv, gt0, gt1, gt2, gt3,
                ob_ref, hb_ref, G_ref, lev_ref,
                lb_ref, hgn_ref, ca_ref, cw_ref, cb_ref, wax_ref, bax_ref, lam_ref,
                lng_ref, lnb_ref, sgw_ref, sgb_ref, wbr_ref, wout_ref, n2_ref, wr_ref, br_ref,
                hg0_ref, lru0_ref,
                xo_ref, h2_ref, comb_ref, hgs_ref, lrus_ref,
                s_scr, hl_scr):
    tt = pl.program_id(0)
    first, last = _tile_flags(tt)

    @pl.when(first)
    def _():
        for h in range(N_HEADS):
            s_scr[h] = hg0_ref[h].T
        hl_scr[...] = lru0_ref[...]

    has_prev = jnp.where(first, 0.0, 1.0)
    has_next = jnp.where(last, 0.0, 1.0)
    f32 = lambda r: r[...].astype(F32)

    p = f32(za_c) * f32(za_in)
    p_p = f32(za_c_p) * f32(za_in_p) * has_prev
    p_n = f32(za_c_n) * f32(za_in_n) * has_next
    ca = ca_ref[...]
    ya = f32(za_b) * sum(ca[k:k + 1, :] * _shifted(p, p_p, p_n, k - 1) for k in range(3))

    zqv = f32(zq)
    q = zqv * jax.nn.sigmoid(zqv)
    o = _hgrn_tile(q, f32(zf), f32(zi), lb_ref[...], G_ref, lev_ref[...], s_scr, False) + ob_ref[...]
    zgv = f32(zg)
    gate_b = hgn_ref[...] * (zgv * jax.nn.sigmoid(zgv))
    yb = jnp.concatenate(
        [o[:, h * HD:(h + 1) * HD]
         * lax.rsqrt(jnp.mean(jnp.square(o[:, h * HD:(h + 1) * HD]), axis=-1, keepdims=True) + EPS)
         for h in range(N_HEADS)], axis=-1) * gate_b

    hf = _lru_tile(f32(zx), f32(zx_p) * has_prev, f32(zx_n) * has_next, cw_ref[...], cb_ref[...],
                   wax_ref[...], bax_ref[...], lam_ref[...], hl_scr, False)
    yc = (hf + hb_ref[...]) * jax.nn.gelu(f32(zy))

    u = jax.nn.gelu(f32(zu))
    v = jax.nn.gelu(f32(zv))
    mu = jnp.mean(v, axis=-1, keepdims=True)
    vc = v - mu
    var = jnp.mean(vc * vc, axis=-1, keepdims=True)
    vn = (vc * lax.rsqrt(var + EPS) * lng_ref[...] + lnb_ref[...]).astype(BF16)
    sgb = sgb_ref[...]
    gw = W // SG_GROUPS
    chunks = []
    for n in range(TL // SG_CHUNK):
        cols = []
        for g in range(SG_GROUPS):
            blk = vn[n * SG_CHUNK:(n + 1) * SG_CHUNK, g * gw:(g + 1) * gw]
            cols.append(_dot(sgw_ref[g], blk) + sgb[:, g:g + 1])
        chunks.append(jnp.concatenate(cols, axis=-1))
    yd = u * jnp.concatenate(chunks, axis=0)

    mix = jnp.zeros((TL, D), F32)
    for k, (y, gt) in enumerate(((ya, gt0), (yb, gt1), (yc, gt2), (yd, gt3))):
        mix = mix + jax.nn.sigmoid(f32(gt)) * _dot(y.astype(BF16), wbr_ref[k])
    x = x_ref[...] + mod_ref[2:3, :] * _dot(mix.astype(BF16), wout_ref[...])
    xo_ref[...] = x

    h2 = x * lax.rsqrt(jnp.mean(x * x, axis=-1, keepdims=True) + EPS) * n2_ref[...]
    h2 = h2 * (1.0 + mod_ref[4:5, :]) + mod_ref[3:4, :]
    h2_ref[...] = h2.astype(BF16)
    comb_ref[...] = _route(h2, wr_ref[...], br_ref[...])

    for h in range(N_HEADS):
        hgs_ref[h] = s_scr[h].T
    lrus_ref[...] = hl_scr[...]


def _forward_pass(l, x, mod, z, o_b, h_b, G, lev, lb, hg_norm, conv_a, cw, cb, wax, bax, lam,
                  ln_g, ln_b, sg_w, sg_bt, w_br, w_out, norm2, w_r, b_r, hg_init, lru_init):
    tile = lambda s: s
    gate = lambda k: pl.BlockSpec((TL, D), lambda s: (s, GATE_BLK + k))
    in_specs = [
        pl.BlockSpec((TL, D), lambda s: (s, 0)),
        pl.BlockSpec((None, None, 6, D), lambda s: (l, _seq_idx(s), 0, 0)),
        _zslab(S_A_IN, tile), _zslab(S_A_B, tile), _zslab(S_A_C, tile),
        _zprev(S_A_IN, tile), _znext(S_A_IN, tile), _zprev(S_A_C, tile), _znext(S_A_C, tile),
        _zslab(S_B_Q, tile), _zslab(S_B_FF, tile), _zslab(S_B_I, tile), _zslab(S_B_G, tile),
        _zslab(S_C_X, tile), _zprev(S_C_X, tile), _znext(S_C_X, tile), _zslab(S_C_Y, tile),
        _zslab(S_D_U, tile), _zslab(S_D_V, tile), gate(0), gate(1), gate(2), gate(3),
        pl.BlockSpec((TL, W), lambda s: (s, 0)), pl.BlockSpec((TL, W), lambda s: (s, 0)),
        _const(G.shape), _const(lev.shape),
        pl.BlockSpec((None, None, 1, W), lambda s: (0, l, 0, 0)),
        _layer_row(l, W),
        pl.BlockSpec((None, 3, W), lambda s: (l, 0, 0)),
        pl.BlockSpec((None, 4, W), lambda s: (l, 0, 0)),
        _layer_row(l, W),
        pl.BlockSpec((None, None, W, 2 * W), lambda s: (l, 0, 0, 0)),
        pl.BlockSpec((None, None, 1, 2 * W), lambda s: (l, 0, 0, 0)),
        pl.BlockSpec((None, None, 1, W), lambda s: (l, 0, 0, 0)),
        _layer_row(l, W), _layer_row(l, W),
        pl.BlockSpec((None, SG_GROUPS, SG_CHUNK, SG_CHUNK), lambda s: (l, 0, 0, 0)),
        pl.BlockSpec((None, SG_CHUNK, SG_GROUPS), lambda s: (l, 0, 0)),
        pl.BlockSpec((None, 4, W, D), lambda s: (l, 0, 0, 0)),
        pl.BlockSpec((None, D, D), lambda s: (l, 0, 0)),
        _layer_row(l, D),
        pl.BlockSpec((None, D, ROUTE_W), lambda s: (l, 0, 0)),
        _layer_row(l, ROUTE_W),
        pl.BlockSpec((None, None, N_HEADS, HD, HD), lambda s: (_seq_idx(s), 0, 0, 0, 0)),
        pl.BlockSpec((None, None, 1, W), lambda s: (_seq_idx(s), 0, 0, 0)),
    ]
    return pl.pallas_call(
        _fwd_kernel,
        grid=(NT,),
        in_specs=in_specs,
        out_specs=[pl.BlockSpec((TL, D), lambda s: (s, 0)),
                   pl.BlockSpec((TL, D), lambda s: (s, 0)),
                   pl.BlockSpec((TL, ROUTE_W), lambda s: (s, 0)),
                   pl.BlockSpec((None, N_HEADS, HD, HD), lambda s: (s, 0, 0, 0)),
                   pl.BlockSpec((None, 1, W), lambda s: (s, 0, 0))],
        out_shape=[jax.ShapeDtypeStruct((T, D), F32), jax.ShapeDtypeStruct((T, D), BF16),
                   jax.ShapeDtypeStruct((T, ROUTE_W), F32),
                   jax.ShapeDtypeStruct((NT, N_HEADS, HD, HD), F32),
                   jax.ShapeDtypeStruct((NT, 1, W), F32)],
        scratch_shapes=[pltpu.VMEM((N_HEADS, HD, HD), F32), pltpu.VMEM((1, W), F32)],
        compiler_params=pltpu.CompilerParams(dimension_semantics=("arbitrary",),
                                             vmem_limit_bytes=VMEM_LIMIT),
        name="forward_pass",
    )(x, mod, *([z] * 21), o_b, h_b, G, lev, lb, hg_norm, conv_a, cw, cb, wax, bax, lam,
      ln_g, ln_b, sg_w, sg_bt, w_br, w_out, norm2, w_r, b_r, hg_init, lru_init)


def _moe_kernel(x_ref, mod_ref, h_ref, comb_ref, wg_ref, wu_ref, wd_ref, o_ref):
    e = pl.program_id(1)
    h = h_ref[...]
    a = _dot(h, wg_ref[...].astype(BF16))
    u = _dot(h, wu_ref[...].astype(BF16))
    lane = lax.broadcasted_iota(jnp.int32, comb_ref.shape, 1)
    c = jnp.sum(jnp.where(lane == e + EXP_LANE0, comb_ref[...], 0.0), axis=-1, keepdims=True)
    mid = (a * jax.nn.sigmoid(a) * u * c).astype(BF16)
    y = _dot(mid, wd_ref[...].astype(BF16))

    @pl.when(e == 0)
    def _():
        o_ref[...] = y

    @pl.when(e > 0)
    def _():
        o_ref[...] += y

    @pl.when(e == N_EXPERTS - 1)
    def _():
        o_ref[...] = x_ref[...] + mod_ref[5:6, :] * o_ref[...]


def _expert_pass(l, x, mod, h2, comb, w_gate, w_up, w_down):
    tiles_per_seq = L_LAT // TM_MOE
    ctx = (N_CTX * L_CTX) // TM_MOE

    def seq(i):
        return jnp.where(i < ctx, 0, 1 + (i - ctx) // tiles_per_seq)

    return pl.pallas_call(
        _moe_kernel,
        grid=(T // TM_MOE, N_EXPERTS),
        in_specs=[pl.BlockSpec((TM_MOE, D), lambda i, e: (i, 0)),
                  pl.BlockSpec((None, None, 6, D), lambda i, e: (l, seq(i), 0, 0)),
                  pl.BlockSpec((TM_MOE, D), lambda i, e: (i, 0)),
                  pl.BlockSpec((TM_MOE, ROUTE_W), lambda i, e: (i, 0)),
                  pl.BlockSpec((None, None, D, D_EXPERT), lambda i, e: (l, e, 0, 0)),
                  pl.BlockSpec((None, None, D, D_EXPERT), lambda i, e: (l, e, 0, 0)),
                  pl.BlockSpec((None, None, D_EXPERT, D), lambda i, e: (l, e, 0, 0))],
        out_specs=pl.BlockSpec((TM_MOE, D), lambda i, e: (i, 0)),
        out_shape=jax.ShapeDtypeStruct((T, D), F32),
        compiler_params=pltpu.CompilerParams(dimension_semantics=("arbitrary", "arbitrary"),
                                             vmem_limit_bytes=VMEM_LIMIT),
        name="expert_pass",
    )(x, mod, h2, comb, w_gate, w_up, w_down)


def _final_kernel(x_ref, g_ref, o_ref):
    x = x_ref[...]
    o_ref[...] = x * lax.rsqrt(jnp.mean(x * x, axis=-1, keepdims=True) + EPS) * g_ref[...]


def _final_norm(x, g):
    tm = 1024
    return pl.pallas_call(
        _final_kernel,
        grid=(T // tm,),
        in_specs=[pl.BlockSpec((tm, D), lambda i: (i, 0)), pl.BlockSpec((1, D), lambda i: (0, 0))],
        out_specs=pl.BlockSpec((tm, D), lambda i: (i, 0)),
        out_shape=jax.ShapeDtypeStruct((T, D), F32),
        compiler_params=pltpu.CompilerParams(dimension_semantics=("arbitrary",)),
        name="final_norm",
    )(x, g.reshape(1, D))


def _grid_pos_embed(rows, d):
    nf = d // 4
    freqs = jnp.exp(-math.log(10000.0) * jnp.arange(nf, dtype=F32) / nf)
    r = jnp.arange(rows, dtype=F32)[:, None] * freqs
    cl = jnp.arange(GRID_W, dtype=F32)[:, None] * freqs
    r_emb = jnp.concatenate([jnp.sin(r), jnp.cos(r)], axis=-1)
    c_emb = jnp.concatenate([jnp.sin(cl), jnp.cos(cl)], axis=-1)
    emb = jnp.concatenate([jnp.broadcast_to(r_emb[:, None], (rows, GRID_W, d // 2)),
                           jnp.broadcast_to(c_emb[None], (rows, GRID_W, d // 2))], axis=-1)
    return emb.reshape(rows * GRID_W, d)


def _block_diag(w):
    n, c = w.shape[-3], w.shape[-1]
    eye = jnp.eye(n, dtype=w.dtype)
    full = w[..., :, :, None, :] * eye[:, None, :, None]
    return full.reshape(*w.shape[:-3], n * c, n * c)


def kernel(x_prompt, x_sample, state_hgrn, state_lru, c, c_ctx, w_ada, b_ada, norm1, norm2, w_in, conv_a,
           hg_lb, hg_norm, lru_conv_w, lru_conv_b, lru_wa, lru_ba, lru_wx, lru_bx, lru_lam, sg_ln_g, sg_ln_b,
           sg_w, sg_b, w_br, w_out, w_rg, b_rg, w_re, b_re, w_gate, w_up, w_down, norm_f):
    assert x_prompt.shape == (N_CTX, L_CTX, D) and x_sample.shape == (N_LAT, L_LAT, D)

    xs = x_sample.astype(F32) + _grid_pos_embed(L_LAT // GRID_W, D)
    x = jnp.concatenate([x_prompt.astype(F32).reshape(-1, D), xs.reshape(-1, D)], axis=0)

    cond = jnp.zeros((8, D), F32).at[0].set(c_ctx.astype(F32)).at[1:1 + N_LAT].set(c.astype(F32))
    mod = _modulation(cond, w_ada, b_ada).reshape(DEPTH, 8, 6, D)

    cs = jnp.cumsum(jax.nn.softmax(hg_lb.astype(F32), axis=1), axis=1)
    lb = (cs - cs[:, :1]).reshape(2, DEPTH, 1, W)
    wax = jnp.concatenate([_block_diag(lru_wa), _block_diag(lru_wx)], axis=-1).astype(BF16)
    bax = jnp.concatenate([lru_ba, lru_bx], axis=-1).reshape(DEPTH, 2, 1, 2 * W)
    lam = lru_lam.reshape(DEPTH, 2, 1, W)
    row = lambda a: a.reshape(DEPTH, 1, a.shape[-1])
    sg_bt = jnp.swapaxes(sg_b, 1, 2)
    w_r = jnp.zeros((DEPTH, D, ROUTE_W), F32).at[:, :, :N_GROUPS].set(w_rg)
    w_r = w_r.at[:, :, EXP_LANE0:EXP_LANE0 + N_EXPERTS].set(w_re)
    b_r = jnp.zeros((DEPTH, ROUTE_W), F32).at[:, :N_GROUPS].set(b_rg)
    b_r = b_r.at[:, EXP_LANE0:EXP_LANE0 + N_EXPERTS].set(b_re)
    w_br_b = w_br.astype(BF16)
    w_out_b = w_out.astype(BF16)
    sg_w_b = sg_w.astype(BF16)

    consts = [_scan_constants(rev) for rev in (False, True)]
    G_f, G_b = (jnp.asarray(g, BF16) for g, _ in consts)
    lev_f, lev_b = (jnp.asarray(v) for _, v in consts)

    hg_states, lru_states = [], []
    for l in range(DEPTH):
        hg_init = jnp.concatenate([jnp.zeros((1, 2, N_HEADS, HD, HD), F32), state_hgrn[:, l].astype(F32)], axis=0)
        lru_init = jnp.concatenate([jnp.zeros((1, 2, W), F32), state_lru[:, l].astype(F32)], axis=0)
        lru_init = lru_init.reshape(1 + N_LAT, 2, 1, W)
        z = _in_projection(l, x, mod, row(norm1), w_in)
        o_b, h_b, hgs_b, lrus_b = _reverse_pass(l, z, G_b, lev_b, lb, lru_conv_w, row(lru_conv_b), wax, bax, lam,
                                                hg_init, lru_init)
        x, h2, comb, hgs_f, lrus_f = _forward_pass(
            l, x, mod, z, o_b, h_b, G_f, lev_f, lb, row(hg_norm), conv_a, lru_conv_w, row(lru_conv_b), wax, bax,
            lam, row(sg_ln_g), row(sg_ln_b), sg_w_b, sg_bt, w_br_b, w_out_b, row(norm2), w_r, row(b_r),
            hg_init, lru_init)
        x = _expert_pass(l, x, mod, h2, comb, w_gate, w_up, w_down)
        hg_states.append(jnp.stack([hgs_f[:CTX_TILES], hgs_b[:CTX_TILES]], axis=1))
        lru_states.append(jnp.stack([lrus_f[:CTX_TILES, 0], lrus_b[:CTX_TILES, 0]], axis=1))

    y = _final_norm(x, norm_f)
    n_ctx = N_CTX * L_CTX
    y_prompt = y[:n_ctx].reshape(N_CTX, L_CTX, D)
    y_sample = y[n_ctx:].reshape(N_LAT, L_LAT, D)
    return y_prompt, y_sample, jnp.stack(hg_states, axis=1), jnp.stack(lru_states, axis=1)
```

```python
import functools
import math

import numpy as np
import jax
import jax.numpy as jnp
from jax import lax
from jax.experimental import pallas as pl
from jax.experimental.pallas import tpu as pltpu

F32 = jnp.float32
BF16 = jnp.bfloat16

D = 1024
W = 512
DEPTH = 4
GRID_W = 64
N_HEADS = 4
HD = 128
LRU_BLOCKS = 8
LRU_BLOCK = W // LRU_BLOCKS
LRU_C = 8.0
SG_CHUNK = 128
SG_GROUPS = 4
N_GROUPS = 4
EXP_PER_GROUP = 4
N_EXPERTS = 16
D_EXPERT = 512
EPS = 1e-6
D_IN = 12 * W + 4 * D

S_A_IN, S_A_B, S_A_C, S_B_Q, S_B_FF, S_B_FB, S_B_I, S_B_G, S_C_X, S_C_Y, S_D_U, S_D_V = range(12)
GATE_BLK = (12 * W) // D

N_CTX, L_CTX = 16, 256
N_LAT, L_LAT = 2, 2048
T = N_CTX * L_CTX + N_LAT * L_LAT
TL = 256
NT = T // TL
CTX_TILES = (N_CTX * L_CTX) // TL
LAT_TILES = L_LAT // TL
LOG_TL = 8
N_LEVELS = LOG_TL
HALO = 16
ROUTE_W = 128
EXP_LANE0 = 4

VMEM_LIMIT = 56 * 1024 * 1024

TM_IN, TN_IN = 1024, 1024
TN_MOD = 1536
ROW_TILES = D // 128
TME = 256
N_ETILES = (2 * T) // TME + N_EXPERTS
N_ROWS = N_ETILES * TME


def _scan_constants(reverse):
    i = np.arange(TL)[:, None]
    s = np.arange(TL)[None, :]
    mats = []
    B = TL
    while B >= 2:
        m = (i // B) * B + B // 2 - 1
        mats.append(np.where(i > m, (s > m) & (s <= i), (s > i) & (s <= m)))
        B //= 2
    mats.append(s <= i)
    mats.append(s > i)
    if reverse:
        mats = [g[::-1, ::-1] for g in mats]
    G = np.concatenate(mats, 0).astype(np.float32)
    x = i ^ s
    hb = np.floor(np.log2(np.maximum(x, 1))).astype(np.int32)
    lev = (LOG_TL - 1) - hb
    lev = np.where(i == s, N_LEVELS, lev)
    causal = (i < s) if reverse else (i > s)
    lev = np.where(causal | (i == s), lev, N_LEVELS + 1).astype(np.int32)
    return G, lev


def _split2(x):
    hi = x.astype(BF16)
    lo = (x - hi.astype(F32)).astype(BF16)
    return hi, lo


def _dot(a, b):
    return jnp.dot(a, b, preferred_element_type=F32)


def _dot_nt(a, b):
    return lax.dot_general(a, b, (((1,), (1,)), ((), ())), preferred_element_type=F32)


def _dot_split(a, b):
    ah, al = _split2(a)
    bh, bl = _split2(b)
    return _dot(ah, bh) + (_dot(ah, bl) + _dot(al, bh))


def _log_sigmoid(x):
    return jnp.minimum(x, 0.0) - jnp.log1p(jnp.exp(-jnp.abs(x)))


def _softplus(x):
    return jnp.maximum(x, 0.0) + jnp.log1p(jnp.exp(-jnp.abs(x)))


def _tile_flags(tt):
    is_lat = tt >= CTX_TILES
    pos = tt & (LAT_TILES - 1)
    first = jnp.logical_or(jnp.logical_not(is_lat), pos == 0)
    last = jnp.logical_or(jnp.logical_not(is_lat), pos == LAT_TILES - 1)
    return first, last


def _seq_idx(t):
    return jnp.where(t < CTX_TILES, 0, 1 + (t - CTX_TILES) // LAT_TILES)


def _shifted(x, prev, nxt, k):
    rows = lax.broadcasted_iota(jnp.int32, x.shape, 0)
    if k == 0:
        return x
    if k < 0:
        y = pltpu.roll(x, -k, 0)
        for r in range(-k):
            src = prev[HALO + k + r:HALO + k + r + 1, :]
            y = jnp.where(rows == r, src, y)
        return y
    y = pltpu.roll(x, TL - k, 0)
    for r in range(k):
        src = nxt[r:r + 1, :]
        y = jnp.where(rows == TL - k + r, src, y)
    return y


def _hgrn_tile(q, zf, v, lb, G_ref, lev, s_ref, reverse):
    ls = _log_sigmoid(zf)
    bb = jnp.log1p(-lb) + ls
    la = jnp.log(lb)
    lf = jnp.maximum(la, bb) + jnp.log1p(jnp.exp(-jnp.abs(la - bb)))
    kk = (1.0 - lb) * jax.nn.sigmoid(-zf)
    end_row = 0 if reverse else TL - 1
    outs = []
    for h in range(N_HEADS):
        sl = slice(h * HD, (h + 1) * HD)
        lf_hi, lf_lo = _split2(lf[:, sl])
        g = G_ref[...]
        e_all = jnp.exp(_dot(g, lf_hi) + _dot(g, lf_lo))
        q_h, k_h, v_h = q[:, sl], kk[:, sl], v[:, sl]
        v_b = v_h.astype(BF16)
        scores = jnp.zeros((TL, TL), F32)
        for l in range(N_LEVELS):
            e = e_all[l * TL:(l + 1) * TL]
            p = _dot_nt((q_h * e).astype(BF16), (k_h * e).astype(BF16))
            scores = jnp.where(lev == l, p, scores)
        e_b = e_all[N_LEVELS * TL:(N_LEVELS + 1) * TL]
        e_e = e_all[(N_LEVELS + 1) * TL:(N_LEVELS + 2) * TL]
        st = s_ref[h]
        o = _dot(scores.astype(BF16), v_b)
        o = o + jnp.sum(q_h * k_h, axis=-1, keepdims=True) * v_h
        o = o + _dot_nt((q_h * e_b).astype(BF16), st.astype(BF16))
        upd = _dot(v_h.T.astype(BF16), (k_h * e_e).astype(BF16))
        s_ref[h] = e_b[end_row:end_row + 1, :] * st + upd
        outs.append(o)
    return jnp.concatenate(outs, axis=-1)


def _lru_tile(zx, zx_p, zx_n, cw, cb, wax, bax, lam, hl_ref, reverse):
    xc = cb + sum(cw[k:k + 1, :] * _shifted(zx, zx_p, zx_n, k - 2) for k in range(4))
    rx = _dot(xc.astype(BF16), wax) + bax
    r = jax.nn.sigmoid(rx[:, :W])
    ig = jax.nn.sigmoid(rx[:, W:])
    log_a = -LRU_C * r * _softplus(-lam)
    a = jnp.exp(log_a)
    u = jnp.sqrt(-jnp.tanh(log_a) * (1.0 + a * a)) * (ig * xc)
    rows = lax.broadcasted_iota(jnp.int32, a.shape, 0)
    d = 1
    while d < TL:
        if reverse:
            a_s, u_s, ok = pltpu.roll(a, TL - d, 0), pltpu.roll(u, TL - d, 0), rows < TL - d
        else:
            a_s, u_s, ok = pltpu.roll(a, d, 0), pltpu.roll(u, d, 0), rows >= d
        u = jnp.where(ok, a * u_s + u, u)
        a = jnp.where(ok, a * a_s, a)
        d *= 2
    hcar = hl_ref[...]
    hseq = u + a * hcar
    end_row = 0 if reverse else TL - 1
    hl_ref[...] = hseq[end_row:end_row + 1, :]
    return hseq


def _mod_kernel(c_ref, w_ref, b_ref, o_ref):
    c = c_ref[...]
    act = c * jax.nn.sigmoid(c)
    o_ref[...] = _dot_split(act, w_ref[...]) + b_ref[...]


def _modulation(cond, w_ada, b_ada):
    n = 6 * D // TN_MOD
    return pl.pallas_call(
        _mod_kernel,
        grid=(DEPTH, n),
        in_specs=[pl.BlockSpec((8, D), lambda l, j: (0, 0)),
                  pl.BlockSpec((None, D, TN_MOD), lambda l, j: (l, 0, j)),
                  pl.BlockSpec((None, 1, TN_MOD), lambda l, j: (l, 0, j))],
        out_specs=pl.BlockSpec((None, 8, TN_MOD), lambda l, j: (l, 0, j)),
        out_shape=jax.ShapeDtypeStruct((DEPTH, 8, 6 * D), F32),
        compiler_params=pltpu.CompilerParams(dimension_semantics=("arbitrary", "arbitrary"),
                                             vmem_limit_bytes=VMEM_LIMIT),
        name="modulation",
    )(cond, w_ada, b_ada.reshape(DEPTH, 1, 6 * D))


def _in_kernel(x_ref, mod_ref, g_ref, w_ref, z_ref, h_scr):
    @pl.when(pl.program_id(1) == 0)
    def _():
        x = x_ref[...]
        h = x * lax.rsqrt(jnp.mean(x * x, axis=-1, keepdims=True) + EPS) * g_ref[...]
        h = h * (1.0 + mod_ref[1:2, :]) + mod_ref[0:1, :]
        h_scr[...] = h.astype(BF16)

    z_ref[...] = _dot(h_scr[...], w_ref[...].astype(BF16)).astype(BF16)


def _in_projection(l, x, mod, norm1, w_in):
    tiles_per_seq = L_LAT // TM_IN
    ctx = (N_CTX * L_CTX) // TM_IN

    def seq(i):
        return jnp.where(i < ctx, 0, 1 + (i - ctx) // tiles_per_seq)

    return pl.pallas_call(
        _in_kernel,
        grid=(T // TM_IN, D_IN // TN_IN),
        in_specs=[pl.BlockSpec((TM_IN, D), lambda i, j: (i, 0)),
                  pl.BlockSpec((None, None, 6, D), lambda i, j: (l, seq(i), 0, 0)),
                  pl.BlockSpec((None, 1, D), lambda i, j: (l, 0, 0)),
                  pl.BlockSpec((None, D, TN_IN), lambda i, j: (l, 0, j))],
        out_specs=pl.BlockSpec((TM_IN, TN_IN), lambda i, j: (i, j)),
        out_shape=jax.ShapeDtypeStruct((T, D_IN), BF16),
        scratch_shapes=[pltpu.VMEM((TM_IN, D), BF16)],
        compiler_params=pltpu.CompilerParams(dimension_semantics=("arbitrary", "arbitrary"),
                                             vmem_limit_bytes=VMEM_LIMIT),
        name="in_projection",
    )(x, mod, norm1, w_in)


def _bwd_kernel(zq, zf, zi, zx, zx_p, zx_n, G_ref, lev_ref, lb_ref, cw_ref, cb_ref, wax_ref, bax_ref,
                lam_ref, hg0_ref, lru0_ref,
                o_ref, h_ref, hgs_ref, lrus_ref, s_scr, hl_scr):
    tt = NT - 1 - pl.program_id(0)
    first, last = _tile_flags(tt)

    @pl.when(last)
    def _():
        for h in range(N_HEADS):
            s_scr[h] = hg0_ref[h].T
        hl_scr[...] = lru0_ref[...]

    has_prev = jnp.where(first, 0.0, 1.0)
    has_next = jnp.where(last, 0.0, 1.0)
    zqv = zq[...].astype(F32)
    q = zqv * jax.nn.sigmoid(zqv)
    o_ref[...] = _hgrn_tile(q, zf[...].astype(F32), zi[...].astype(F32), lb_ref[...], G_ref, lev_ref[...],
                            s_scr, True)
    h_ref[...] = _lru_tile(zx[...].astype(F32), zx_p[...].astype(F32) * has_prev,
                           zx_n[...].astype(F32) * has_next, cw_ref[...], cb_ref[...], wax_ref[...],
                           bax_ref[...], lam_ref[...], hl_scr, True)
    for h in range(N_HEADS):
        hgs_ref[h] = s_scr[h].T
    lrus_ref[...] = hl_scr[...]


def _zslab(k, tile):
    return pl.BlockSpec((TL, W), lambda s: (tile(s), k))


def _zprev(k, tile):
    return pl.BlockSpec((HALO, W), lambda s: (jnp.maximum(tile(s) * (TL // HALO) - 1, 0), k))


def _znext(k, tile):
    return pl.BlockSpec((HALO, W), lambda s: (jnp.minimum((tile(s) + 1) * (TL // HALO), T // HALO - 1), k))


def _const(shape):
    nd = len(shape)
    return pl.BlockSpec(shape, lambda s: (0,) * nd)


def _layer_row(l, n):
    return pl.BlockSpec((None, 1, n), lambda s: (l, 0, 0))


def _reverse_pass(l, z, G, lev, lb, cw, cb, wax, bax, lam, hg_init, lru_init):
    tile = lambda s: NT - 1 - s
    state_specs = [
        pl.BlockSpec((None, None, N_HEADS, HD, HD), lambda s: (_seq_idx(tile(s)), 1, 0, 0, 0)),
        pl.BlockSpec((None, None, 1, W), lambda s: (_seq_idx(tile(s)), 1, 0, 0)),
    ]
    return pl.pallas_call(
        _bwd_kernel,
        grid=(NT,),
        in_specs=[_zslab(S_B_Q, tile), _zslab(S_B_FB, tile), _zslab(S_B_I, tile), _zslab(S_C_X, tile),
                  _zprev(S_C_X, tile), _znext(S_C_X, tile),
                  _const(G.shape), _const(lev.shape),
                  pl.BlockSpec((None, None, 1, W), lambda s: (1, l, 0, 0)),
                  pl.BlockSpec((None, 4, W), lambda s: (l, 0, 0)),
                  _layer_row(l, W),
                  pl.BlockSpec((None, None, W, 2 * W), lambda s: (l, 1, 0, 0)),
                  pl.BlockSpec((None, None, 1, 2 * W), lambda s: (l, 1, 0, 0)),
                  pl.BlockSpec((None, None, 1, W), lambda s: (l, 1, 0, 0)),
                  ] + state_specs,
        out_specs=[pl.BlockSpec((TL, W), lambda s: (tile(s), 0)),
                   pl.BlockSpec((TL, W), lambda s: (tile(s), 0)),
                   pl.BlockSpec((None, N_HEADS, HD, HD), lambda s: (tile(s), 0, 0, 0)),
                   pl.BlockSpec((None, 1, W), lambda s: (tile(s), 0, 0))],
        out_shape=[jax.ShapeDtypeStruct((T, W), F32), jax.ShapeDtypeStruct((T, W), F32),
                   jax.ShapeDtypeStruct((NT, N_HEADS, HD, HD), F32),
                   jax.ShapeDtypeStruct((NT, 1, W), F32)],
        scratch_shapes=[pltpu.VMEM((N_HEADS, HD, HD), F32), pltpu.VMEM((1, W), F32)],
        compiler_params=pltpu.CompilerParams(dimension_semantics=("arbitrary",),
                                             vmem_limit_bytes=VMEM_LIMIT),
        name="reverse_pass",
    )(z, z, z, z, z, z, G, lev, lb, cw, cb, wax, bax, lam, hg_init, lru_init)


def _route(h2, wr, br, cnt_ref):
    logits = _dot_split(h2, wr) + br
    lane = lax.broadcasted_iota(jnp.int32, logits.shape, 1)
    lane_f = lane.astype(F32)
    neg = jnp.float32(-jnp.inf)
    is_g = lane < N_GROUPS
    gl = jnp.where(is_g, logits, neg)
    gmax = jnp.max(gl, axis=-1, keepdims=True)
    g_sel = jnp.min(jnp.where(gl == gmax, lane_f, float(ROUTE_W)), axis=-1, keepdims=True)
    g_p = 1.0 / jnp.sum(jnp.where(is_g, jnp.exp(gl - gmax), 0.0), axis=-1, keepdims=True)
    e_grp = ((lane - EXP_LANE0) >> 2).astype(F32)
    in_grp = (lane >= EXP_LANE0) & (lane < EXP_LANE0 + N_EXPERTS) & (e_grp == g_sel)
    el = jnp.where(in_grp, logits, neg)
    m1 = jnp.max(el, axis=-1, keepdims=True)
    i1 = jnp.min(jnp.where(el == m1, lane_f, float(ROUTE_W)), axis=-1, keepdims=True)
    el2 = jnp.where(lane_f == i1, neg, el)
    m2 = jnp.max(el2, axis=-1, keepdims=True)
    i2 = jnp.min(jnp.where(el2 == m2, lane_f, float(ROUTE_W)), axis=-1, keepdims=True)
    t = jnp.exp(m2 - m1)
    w1 = g_p / (1.0 + t)
    w2 = w1 * t

    oh1 = (lane_f == i1).astype(F32)
    oh2 = (lane_f == i2).astype(F32)
    oh = oh1 + oh2
    r = lax.broadcasted_iota(jnp.int32, (TL, TL), 0)
    cc = lax.broadcasted_iota(jnp.int32, (TL, TL), 1)
    before = _dot(jnp.where(r > cc, 1.0, 0.0).astype(BF16), oh.astype(BF16)) + cnt_ref[...]
    rank1 = jnp.sum(oh1 * before, axis=-1, keepdims=True)
    rank2 = jnp.sum(oh2 * before, axis=-1, keepdims=True)
    cnt_ref[...] += jnp.sum(oh, axis=0, keepdims=True)
    cols = (i1 - EXP_LANE0, i2 - EXP_LANE0, w1, w2, rank1, rank2)
    out = jnp.zeros(logits.shape, F32)
    for k, col in enumerate(cols):
        out = jnp.where(lane == k, col, out)
    return out


def _fwd_kernel(x_ref, mod_ref,
                za_in, za_b, za_c, za_in_p, za_in_n, za_c_p, za_c_n,
                zq, zf, zi, zg, zx, zx_p, zx_n, zy, zu, zv, gt0, gt1, gt2, gt3,
                ob_ref, hb_ref, G_ref, lev_ref,
                lb_ref, hgn_ref, ca_ref, cw_ref, cb_ref, wax_ref, bax_ref, lam_ref,
                lng_ref, lnb_ref, sgw_ref, sgb_ref, wbr_ref, wout_ref, n2_ref, wr_ref, br_ref,
                hg0_ref, lru0_ref,
                xo_ref, h2t_ref, route_ref, cnt_ref, hgs_ref, lrus_ref,
                s_scr, hl_scr):
    tt = pl.program_id(0)
    first, last = _tile_flags(tt)

    @pl.when(tt == 0)
    def _():
        cnt_ref[...] = jnp.zeros(cnt_ref.shape, F32)

    @pl.when(first)
    def _():
        for h in range(N_HEADS):
            s_scr[h] = hg0_ref[h].T
        hl_scr[...] = lru0_ref[...]

    has_prev = jnp.where(first, 0.0, 1.0)
    has_next = jnp.where(last, 0.0, 1.0)
    f32 = lambda r: r[...].astype(F32)

    p = f32(za_c) * f32(za_in)
    p_p = f32(za_c_p) * f32(za_in_p) * has_prev
    p_n = f32(za_c_n) * f32(za_in_n) * has_next
    ca = ca_ref[...]
    ya = f32(za_b) * sum(ca[k:k + 1, :] * _shifted(p, p_p, p_n, k - 1) for k in range(3))

    zqv = f32(zq)
    q = zqv * jax.nn.sigmoid(zqv)
    o = _hgrn_tile(q, f32(zf), f32(zi), lb_ref[...], G_ref, lev_ref[...], s_scr, False) + ob_ref[...]
    zgv = f32(zg)
    gate_b = hgn_ref[...] * (zgv * jax.nn.sigmoid(zgv))
    yb = jnp.concatenate(
        [o[:, h * HD:(h + 1) * HD]
         * lax.rsqrt(jnp.mean(jnp.square(o[:, h * HD:(h + 1) * HD]), axis=-1, keepdims=True) + EPS)
         for h in range(N_HEADS)], axis=-1) * gate_b

    hf = _lru_tile(f32(zx), f32(zx_p) * has_prev, f32(zx_n) * has_next, cw_ref[...], cb_ref[...],
                   wax_ref[...], bax_ref[...], lam_ref[...], hl_scr, False)
    yc = (hf + hb_ref[...]) * jax.nn.gelu(f32(zy))

    u = jax.nn.gelu(f32(zu))
    v = jax.nn.gelu(f32(zv))
    mu = jnp.mean(v, axis=-1, keepdims=True)
    vc = v - mu
    var = jnp.mean(vc * vc, axis=-1, keepdims=True)
    vn = (vc * lax.rsqrt(var + EPS) * lng_ref[...] + lnb_ref[...]).astype(BF16)
    sgb = sgb_ref[...]
    gw = W // SG_GROUPS
    chunks = []
    for n in range(TL // SG_CHUNK):
        cols = []
        for g in range(SG_GROUPS):
            blk = vn[n * SG_CHUNK:(n + 1) * SG_CHUNK, g * gw:(g + 1) * gw]
            cols.append(_dot(sgw_ref[g], blk) + sgb[:, g:g + 1])
        chunks.append(jnp.concatenate(cols, axis=-1))
    yd = u * jnp.concatenate(chunks, axis=0)

    mix = jnp.zeros((TL, D), F32)
    for k, (y, gt) in enumerate(((ya, gt0), (yb, gt1), (yc, gt2), (yd, gt3))):
        mix = mix + jax.nn.sigmoid(f32(gt)) * _dot(y.astype(BF16), wbr_ref[k])
    x = x_ref[...] + mod_ref[2:3, :] * _dot(mix.astype(BF16), wout_ref[...])
    xo_ref[...] = x

    h2 = x * lax.rsqrt(jnp.mean(x * x, axis=-1, keepdims=True) + EPS) * n2_ref[...]
    h2 = h2 * (1.0 + mod_ref[4:5, :]) + mod_ref[3:4, :]
    for s in range(ROW_TILES):
        h2t_ref[pl.ds(s, TL, stride=ROW_TILES), :] = h2[:, s * 128:(s + 1) * 128]
    route_ref[...] = _route(h2, wr_ref[...], br_ref[...], cnt_ref)

    for h in range(N_HEADS):
        hgs_ref[h] = s_scr[h].T
    lrus_ref[...] = hl_scr[...]


def _forward_pass(l, x, mod, z, o_b, h_b, G, lev, lb, hg_norm, conv_a, cw, cb, wax, bax, lam,
                  ln_g, ln_b, sg_w, sg_bt, w_br, w_out, norm2, w_r, b_r, hg_init, lru_init):
    tile = lambda s: s
    gate = lambda k: pl.BlockSpec((TL, D), lambda s: (s, GATE_BLK + k))
    in_specs = [
        pl.BlockSpec((TL, D), lambda s: (s, 0)),
        pl.BlockSpec((None, None, 6, D), lambda s: (l, _seq_idx(s), 0, 0)),
        _zslab(S_A_IN, tile), _zslab(S_A_B, tile), _zslab(S_A_C, tile),
        _zprev(S_A_IN, tile), _znext(S_A_IN, tile), _zprev(S_A_C, tile), _znext(S_A_C, tile),
        _zslab(S_B_Q, tile), _zslab(S_B_FF, tile), _zslab(S_B_I, tile), _zslab(S_B_G, tile),
        _zslab(S_C_X, tile), _zprev(S_C_X, tile), _znext(S_C_X, tile), _zslab(S_C_Y, tile),
        _zslab(S_D_U, tile), _zslab(S_D_V, tile), gate(0), gate(1), gate(2), gate(3),
        pl.BlockSpec((TL, W), lambda s: (s, 0)), pl.BlockSpec((TL, W), lambda s: (s, 0)),
        _const(G.shape), _const(lev.shape),
        pl.BlockSpec((None, None, 1, W), lambda s: (0, l, 0, 0)),
        _layer_row(l, W),
        pl.BlockSpec((None, 3, W), lambda s: (l, 0, 0)),
        pl.BlockSpec((None, 4, W), lambda s: (l, 0, 0)),
        _layer_row(l, W),
        pl.BlockSpec((None, None, W, 2 * W), lambda s: (l, 0, 0, 0)),
        pl.BlockSpec((None, None, 1, 2 * W), lambda s: (l, 0, 0, 0)),
        pl.BlockSpec((None, None, 1, W), lambda s: (l, 0, 0, 0)),
        _layer_row(l, W), _layer_row(l, W),
        pl.BlockSpec((None, SG_GROUPS, SG_CHUNK, SG_CHUNK), lambda s: (l, 0, 0, 0)),
        pl.BlockSpec((None, SG_CHUNK, SG_GROUPS), lambda s: (l, 0, 0)),
        pl.BlockSpec((None, 4, W, D), lambda s: (l, 0, 0, 0)),
        pl.BlockSpec((None, D, D), lambda s: (l, 0, 0)),
        _layer_row(l, D),
        pl.BlockSpec((None, D, ROUTE_W), lambda s: (l, 0, 0)),
        _layer_row(l, ROUTE_W),
        pl.BlockSpec((None, None, N_HEADS, HD, HD), lambda s: (_seq_idx(s), 0, 0, 0, 0)),
        pl.BlockSpec((None, None, 1, W), lambda s: (_seq_idx(s), 0, 0, 0)),
    ]
    return pl.pallas_call(
        _fwd_kernel,
        grid=(NT,),
        in_specs=in_specs,
        out_specs=[pl.BlockSpec((TL, D), lambda s: (s, 0)),
                   pl.BlockSpec((TL * ROW_TILES, 128), lambda s: (s, 0)),
                   pl.BlockSpec((TL, ROUTE_W), lambda s: (s, 0)),
                   pl.BlockSpec((1, ROUTE_W), lambda s: (0, 0)),
                   pl.BlockSpec((None, N_HEADS, HD, HD), lambda s: (s, 0, 0, 0)),
                   pl.BlockSpec((None, 1, W), lambda s: (s, 0, 0))],
        out_shape=[jax.ShapeDtypeStruct((T, D), F32), jax.ShapeDtypeStruct((T * ROW_TILES, 128), F32),
                   jax.ShapeDtypeStruct((T, ROUTE_W), F32), jax.ShapeDtypeStruct((1, ROUTE_W), F32),
                   jax.ShapeDtypeStruct((NT, N_HEADS, HD, HD), F32),
                   jax.ShapeDtypeStruct((NT, 1, W), F32)],
        scratch_shapes=[pltpu.VMEM((N_HEADS, HD, HD), F32), pltpu.VMEM((1, W), F32)],
        compiler_params=pltpu.CompilerParams(dimension_semantics=("arbitrary",),
                                             vmem_limit_bytes=VMEM_LIMIT),
        name="forward_pass",
    )(x, mod, *([z] * 21), o_b, h_b, G, lev, lb, hg_norm, conv_a, cw, cb, wax, bax, lam,
      ln_g, ln_b, sg_w, sg_bt, w_br, w_out, norm2, w_r, b_r, hg_init, lru_init)


def _token_rows(ref, row, n=1):
    return ref.at[pl.ds(row * ROW_TILES, n * ROW_TILES), :]


def _read_rows(ref, first_row, n):
    return jnp.concatenate([ref[pl.ds(first_row * ROW_TILES + s, n, stride=ROW_TILES), :]
                            for s in range(ROW_TILES)], axis=1)


def _padded(n):
    return ((n + TME - 1) // TME) * TME


def _dispatch_kernel(pos_ref, cnt_ref, h2t_hbm, xs_hbm, zero_buf, sem):
    s = pl.program_id(0)

    def copies(i):
        tok = s * TL + i
        return [pltpu.make_async_copy(_token_rows(h2t_hbm, tok), _token_rows(xs_hbm, pos_ref[2 * tok + k]), sem)
                for k in range(2)]

    def start(i, c):
        for cp in copies(i):
            cp.start()
        return c

    def wait(i, c):
        for cp in copies(i):
            cp.wait()
        return c

    lax.fori_loop(0, TL, start, 0, unroll=8)
    lax.fori_loop(0, TL, wait, 0, unroll=8)

    @pl.when(s == NT - 1)
    def _():
        zero_buf[...] = jnp.zeros(zero_buf.shape, F32)

        def pad_copy(j):
            return pltpu.make_async_copy(_token_rows(zero_buf, 0), _token_rows(xs_hbm, j), sem)

        def pad_start(j, c):
            pad_copy(j).start()
            return c

        def pad_wait(j, c):
            pad_copy(j).wait()
            return c

        def tail_copy(t):
            return pltpu.make_async_copy(zero_buf, _token_rows(xs_hbm, t * TME, TME), sem)

        def tail_start(t, c):
            tail_copy(t).start()
            return c

        def tail_wait(t, c):
            tail_copy(t).wait()
            return c

        seg = 0
        for e in range(N_EXPERTS):
            n = cnt_ref[e]
            lax.fori_loop(seg + n, seg + _padded(n), pad_start, 0)
            lax.fori_loop(seg + n, seg + _padded(n), pad_wait, 0)
            seg = seg + _padded(n)
        lax.fori_loop(seg // TME, N_ETILES, tail_start, 0)
        lax.fori_loop(seg // TME, N_ETILES, tail_wait, 0)


def _dispatch(pos, cnt, h2t):
    return pl.pallas_call(
        _dispatch_kernel,
        grid_spec=pltpu.PrefetchScalarGridSpec(
            num_scalar_prefetch=2, grid=(NT,),
            in_specs=[pl.BlockSpec(memory_space=pl.ANY)],
            out_specs=pl.BlockSpec(memory_space=pl.ANY),
            scratch_shapes=[pltpu.VMEM((TME * ROW_TILES, 128), F32), pltpu.SemaphoreType.DMA]),
        out_shape=jax.ShapeDtypeStruct((N_ROWS * ROW_TILES, 128), F32),
        compiler_params=pltpu.CompilerParams(dimension_semantics=("arbitrary",)),
        name="dispatch",
    )(pos, cnt, h2t)


def _expert_kernel(te_ref, tv_ref, xs_ref, wg_ref, wu_ref, wd_ref, ys_ref, wg_s, wu_s, wd_s):
    i = pl.program_id(0)

    @pl.when(tv_ref[i] == 0)
    def _():
        ys_ref[...] = jnp.zeros(ys_ref.shape, F32)

    @pl.when(tv_ref[i] == 1)
    def _():
        @pl.when(jnp.logical_or(i == 0, te_ref[i] != te_ref[jnp.maximum(i - 1, 0)]))
        def _():
            wg_s[...] = wg_ref[...].astype(BF16)
            wu_s[...] = wu_ref[...].astype(BF16)
            wd_s[...] = wd_ref[...].astype(BF16)

        x = _read_rows(xs_ref, 0, TME).astype(BF16)
        a = _dot(x, wg_s[...])
        u = _dot(x, wu_s[...])
        y = _dot((a * jax.nn.sigmoid(a) * u).astype(BF16), wd_s[...])
        for s in range(ROW_TILES):
            ys_ref[pl.ds(s, TME, stride=ROW_TILES), :] = y[:, s * 128:(s + 1) * 128]


def _grouped_experts(l, te, tv, xs, w_gate, w_up, w_down):
    rows = pl.BlockSpec((TME * ROW_TILES, 128), lambda i, te, tv: (i, 0))
    return pl.pallas_call(
        _expert_kernel,
        grid_spec=pltpu.PrefetchScalarGridSpec(
            num_scalar_prefetch=2, grid=(N_ETILES,),
            in_specs=[rows,
                      pl.BlockSpec((None, None, D, D_EXPERT), lambda i, te, tv: (l, te[i], 0, 0)),
                      pl.BlockSpec((None, None, D, D_EXPERT), lambda i, te, tv: (l, te[i], 0, 0)),
                      pl.BlockSpec((None, None, D_EXPERT, D), lambda i, te, tv: (l, te[i], 0, 0))],
            out_specs=rows,
            scratch_shapes=[pltpu.VMEM((D, D_EXPERT), BF16), pltpu.VMEM((D, D_EXPERT), BF16),
                            pltpu.VMEM((D_EXPERT, D), BF16)]),
        out_shape=jax.ShapeDtypeStruct((N_ROWS * ROW_TILES, 128), F32),
        compiler_params=pltpu.CompilerParams(dimension_semantics=("arbitrary",),
                                             vmem_limit_bytes=VMEM_LIMIT),
        name="grouped_experts",
    )(te, tv, xs, w_gate, w_up, w_down)


def _combine_kernel(pos_ref, x_ref, mod_ref, route_ref, ys_hbm, o_ref, buf, sem):
    s = pl.program_id(0)

    def copies(i):
        tok = s * TL + i
        return [pltpu.make_async_copy(_token_rows(ys_hbm, pos_ref[2 * tok + k]), _token_rows(buf, k * TL + i), sem)
                for k in range(2)]

    def start(i, c):
        for cp in copies(i):
            cp.start()
        return c

    def wait(i, c):
        for cp in copies(i):
            cp.wait()
        return c

    lax.fori_loop(0, TL, start, 0, unroll=8)
    lax.fori_loop(0, TL, wait, 0, unroll=8)
    route = route_ref[...]
    y = route[:, 2:3] * _read_rows(buf, 0, TL) + route[:, 3:4] * _read_rows(buf, TL, TL)
    o_ref[...] = x_ref[...] + mod_ref[5:6, :] * y


def _combine(l, pos, x, mod, route, ys):
    return pl.pallas_call(
        _combine_kernel,
        grid_spec=pltpu.PrefetchScalarGridSpec(
            num_scalar_prefetch=1, grid=(NT,),
            in_specs=[pl.BlockSpec((TL, D), lambda s, pos: (s, 0)),
                      pl.BlockSpec((None, None, 6, D), lambda s, pos: (l, _seq_idx(s), 0, 0)),
                      pl.BlockSpec((TL, ROUTE_W), lambda s, pos: (s, 0)),
                      pl.BlockSpec(memory_space=pl.ANY)],
            out_specs=pl.BlockSpec((TL, D), lambda s, pos: (s, 0)),
            scratch_shapes=[pltpu.VMEM((2 * TL * ROW_TILES, 128), F32), pltpu.SemaphoreType.DMA]),
        out_shape=jax.ShapeDtypeStruct((T, D), F32),
        compiler_params=pltpu.CompilerParams(dimension_semantics=("arbitrary",)),
        name="combine",
    )(pos, x, mod, route, ys)


def _expert_pass(l, x, mod, h2t, route, cnt, w_gate, w_up, w_down):
    ri = route[:, :6].astype(jnp.int32)
    cnt = cnt[0, EXP_LANE0:EXP_LANE0 + N_EXPERTS].astype(jnp.int32)
    tiles = (cnt + TME - 1) // TME
    ends = jnp.cumsum(tiles)
    seg = (ends - tiles) * TME
    pos = jnp.stack([seg[ri[:, 0]] + ri[:, 4], seg[ri[:, 1]] + ri[:, 5]], axis=1).reshape(-1)
    i = jnp.arange(N_ETILES, dtype=jnp.int32)
    tv = (i < ends[-1]).astype(jnp.int32)
    te = jnp.searchsorted(ends, jnp.minimum(i, ends[-1] - 1), side="right").astype(jnp.int32)
    xs = _dispatch(pos, cnt, h2t)
    ys = _grouped_experts(l, te, tv, xs, w_gate, w_up, w_down)
    return _combine(l, pos, x, mod, route, ys)


def _final_kernel(x_ref, g_ref, o_ref):
    x = x_ref[...]
    o_ref[...] = x * lax.rsqrt(jnp.mean(x * x, axis=-1, keepdims=True) + EPS) * g_ref[...]


def _final_norm(x, g):
    tm = 1024
    return pl.pallas_call(
        _final_kernel,
        grid=(T // tm,),
        in_specs=[pl.BlockSpec((tm, D), lambda i: (i, 0)), pl.BlockSpec((1, D), lambda i: (0, 0))],
        out_specs=pl.BlockSpec((tm, D), lambda i: (i, 0)),
        out_shape=jax.ShapeDtypeStruct((T, D), F32),
        compiler_params=pltpu.CompilerParams(dimension_semantics=("arbitrary",)),
        name="final_norm",
    )(x, g.reshape(1, D))


def _grid_pos_embed(rows, d):
    nf = d // 4
    freqs = jnp.exp(-math.log(10000.0) * jnp.arange(nf, dtype=F32) / nf)
    r = jnp.arange(rows, dtype=F32)[:, None] * freqs
    cl = jnp.arange(GRID_W, dtype=F32)[:, None] * freqs
    r_emb = jnp.concatenate([jnp.sin(r), jnp.cos(r)], axis=-1)
    c_emb = jnp.concatenate([jnp.sin(cl), jnp.cos(cl)], axis=-1)
    emb = jnp.concatenate([jnp.broadcast_to(r_emb[:, None], (rows, GRID_W, d // 2)),
                           jnp.broadcast_to(c_emb[None], (rows, GRID_W, d // 2))], axis=-1)
    return emb.reshape(rows * GRID_W, d)


def _block_diag(w):
    n, c = w.shape[-3], w.shape[-1]
    eye = jnp.eye(n, dtype=w.dtype)
    full = w[..., :, :, None, :] * eye[:, None, :, None]
    return full.reshape(*w.shape[:-3], n * c, n * c)


def kernel(x_prompt, x_sample, state_hgrn, state_lru, c, c_ctx, w_ada, b_ada, norm1, norm2, w_in, conv_a,
           hg_lb, hg_norm, lru_conv_w, lru_conv_b, lru_wa, lru_ba, lru_wx, lru_bx, lru_lam, sg_ln_g, sg_ln_b,
           sg_w, sg_b, w_br, w_out, w_rg, b_rg, w_re, b_re, w_gate, w_up, w_down, norm_f):
    assert x_prompt.shape == (N_CTX, L_CTX, D) and x_sample.shape == (N_LAT, L_LAT, D)

    xs = x_sample.astype(F32) + _grid_pos_embed(L_LAT // GRID_W, D)
    x = jnp.concatenate([x_prompt.astype(F32).reshape(-1, D), xs.reshape(-1, D)], axis=0)

    cond = jnp.zeros((8, D), F32).at[0].set(c_ctx.astype(F32)).at[1:1 + N_LAT].set(c.astype(F32))
    mod = _modulation(cond, w_ada, b_ada).reshape(DEPTH, 8, 6, D)

    cs = jnp.cumsum(jax.nn.softmax(hg_lb.astype(F32), axis=1), axis=1)
    lb = (cs - cs[:, :1]).reshape(2, DEPTH, 1, W)
    wax = jnp.concatenate([_block_diag(lru_wa), _block_diag(lru_wx)], axis=-1).astype(BF16)
    bax = jnp.concatenate([lru_ba, lru_bx], axis=-1).reshape(DEPTH, 2, 1, 2 * W)
    lam = lru_lam.reshape(DEPTH, 2, 1, W)
    row = lambda a: a.reshape(DEPTH, 1, a.shape[-1])
    sg_bt = jnp.swapaxes(sg_b, 1, 2)
    w_r = jnp.zeros((DEPTH, D, ROUTE_W), F32).at[:, :, :N_GROUPS].set(w_rg)
    w_r = w_r.at[:, :, EXP_LANE0:EXP_LANE0 + N_EXPERTS].set(w_re)
    b_r = jnp.zeros((DEPTH, ROUTE_W), F32).at[:, :N_GROUPS].set(b_rg)
    b_r = b_r.at[:, EXP_LANE0:EXP_LANE0 + N_EXPERTS].set(b_re)
    w_br_b = w_br.astype(BF16)
    w_out_b = w_out.astype(BF16)
    sg_w_b = sg_w.astype(BF16)

    consts = [_scan_constants(rev) for rev in (False, True)]
    G_f, G_b = (jnp.asarray(g, BF16) for g, _ in consts)
    lev_f, lev_b = (jnp.asarray(v) for _, v in consts)

    hg_states, lru_states = [], []
    for l in range(DEPTH):
        hg_init = jnp.concatenate([jnp.zeros((1, 2, N_HEADS, HD, HD), F32), state_hgrn[:, l].astype(F32)], axis=0)
        lru_init = jnp.concatenate([jnp.zeros((1, 2, W), F32), state_lru[:, l].astype(F32)], axis=0)
        lru_init = lru_init.reshape(1 + N_LAT, 2, 1, W)
        z = _in_projection(l, x, mod, row(norm1), w_in)
        o_b, h_b, hgs_b, lrus_b = _reverse_pass(l, z, G_b, lev_b, lb, lru_conv_w, row(lru_conv_b), wax, bax, lam,
                                                hg_init, lru_init)
        x, h2t, route, cnt, hgs_f, lrus_f = _forward_pass(
            l, x, mod, z, o_b, h_b, G_f, lev_f, lb, row(hg_norm), conv_a, lru_conv_w, row(lru_conv_b), wax, bax,
            lam, row(sg_ln_g), row(sg_ln_b), sg_w_b, sg_bt, w_br_b, w_out_b, row(norm2), w_r, row(b_r),
            hg_init, lru_init)
        x = _expert_pass(l, x, mod, h2t, route, cnt, w_gate, w_up, w_down)
        hg_states.append(jnp.stack([hgs_f[:CTX_TILES], hgs_b[:CTX_TILES]], axis=1))
        lru_states.append(jnp.stack([lrus_f[:CTX_TILES, 0], lrus_b[:CTX_TILES, 0]], axis=1))

    y = _final_norm(x, norm_f)
    n_ctx = N_CTX * L_CTX
    y_prompt = y[:n_ctx].reshape(N_CTX, L_CTX, D)
    y_sample = y[n_ctx:].reshape(N_LAT, L_LAT, D)
    return y_prompt, y_sample, jnp.stack(hg_states, axis=1), jnp.stack(lru_states, axis=1)
```

```python
import functools
import math

import numpy as np
import jax
import jax.numpy as jnp
from jax import lax
from jax.experimental import pallas as pl
from jax.experimental.pallas import tpu as pltpu

F32 = jnp.float32
BF16 = jnp.bfloat16

D = 1024
W = 512
DEPTH = 4
GRID_W = 64
N_HEADS = 4
HD = 128
LRU_BLOCKS = 8
LRU_BLOCK = W // LRU_BLOCKS
LRU_C = 8.0
SG_CHUNK = 128
SG_GROUPS = 4
N_GROUPS = 4
EXP_PER_GROUP = 4
N_EXPERTS = 16
D_EXPERT = 512
EPS = 1e-6
D_IN = 12 * W + 4 * D

S_A_IN, S_A_B, S_A_C, S_B_Q, S_B_FF, S_B_FB, S_B_I, S_B_G, S_C_X, S_C_Y, S_D_U, S_D_V = range(12)
GATE_BLK = (12 * W) // D

N_CTX, L_CTX = 16, 256
N_LAT, L_LAT = 2, 2048
T = N_CTX * L_CTX + N_LAT * L_LAT
TL = 256
NT = T // TL
CTX_TILES = (N_CTX * L_CTX) // TL
LAT_TILES = L_LAT // TL
LOG_TL = 8
N_LEVELS = LOG_TL
HALO = 16
ROUTE_W = 128
EXP_LANE0 = 4

VMEM_LIMIT = 56 * 1024 * 1024

TM_IN, TN_IN = 1024, 1024
TN_MOD = 1536
ROW_TILES = D // 128
TME = 256
N_ETILES = (2 * T) // TME + N_EXPERTS
N_ROWS = N_ETILES * TME


def _scan_constants(reverse):
    i = np.arange(TL)[:, None]
    s = np.arange(TL)[None, :]
    mats = []
    B = TL
    while B >= 2:
        m = (i // B) * B + B // 2 - 1
        mats.append(np.where(i > m, (s > m) & (s <= i), (s > i) & (s <= m)))
        B //= 2
    mats.append(s <= i)
    mats.append(s > i)
    if reverse:
        mats = [g[::-1, ::-1] for g in mats]
    G = np.concatenate(mats, 0).astype(np.float32)
    x = i ^ s
    hb = np.floor(np.log2(np.maximum(x, 1))).astype(np.int32)
    lev = (LOG_TL - 1) - hb
    lev = np.where(i == s, N_LEVELS, lev)
    causal = (i < s) if reverse else (i > s)
    lev = np.where(causal | (i == s), lev, N_LEVELS + 1).astype(np.int32)
    return G, lev


def _split2(x):
    hi = x.astype(BF16)
    lo = (x - hi.astype(F32)).astype(BF16)
    return hi, lo


def _dot(a, b):
    return jnp.dot(a, b, preferred_element_type=F32)


def _dot_nt(a, b):
    return lax.dot_general(a, b, (((1,), (1,)), ((), ())), preferred_element_type=F32)


def _dot_split(a, b):
    ah, al = _split2(a)
    bh, bl = _split2(b)
    return _dot(ah, bh) + (_dot(ah, bl) + _dot(al, bh))


def _log_sigmoid(x):
    return jnp.minimum(x, 0.0) - jnp.log1p(jnp.exp(-jnp.abs(x)))


def _softplus(x):
    return jnp.maximum(x, 0.0) + jnp.log1p(jnp.exp(-jnp.abs(x)))


def _tile_flags(tt):
    is_lat = tt >= CTX_TILES
    pos = tt & (LAT_TILES - 1)
    first = jnp.logical_or(jnp.logical_not(is_lat), pos == 0)
    last = jnp.logical_or(jnp.logical_not(is_lat), pos == LAT_TILES - 1)
    return first, last


def _seq_idx(t):
    return jnp.where(t < CTX_TILES, 0, 1 + (t - CTX_TILES) // LAT_TILES)


def _shifted(x, prev, nxt, k):
    rows = lax.broadcasted_iota(jnp.int32, x.shape, 0)
    if k == 0:
        return x
    if k < 0:
        y = pltpu.roll(x, -k, 0)
        for r in range(-k):
            src = prev[HALO + k + r:HALO + k + r + 1, :]
            y = jnp.where(rows == r, src, y)
        return y
    y = pltpu.roll(x, TL - k, 0)
    for r in range(k):
        src = nxt[r:r + 1, :]
        y = jnp.where(rows == TL - k + r, src, y)
    return y


def _hgrn_tile(q, zf, v, lb, G_ref, lev, s_ref, reverse):
    ls = _log_sigmoid(zf)
    bb = jnp.log1p(-lb) + ls
    la = jnp.log(lb)
    lf = jnp.maximum(la, bb) + jnp.log1p(jnp.exp(-jnp.abs(la - bb)))
    kk = (1.0 - lb) * jax.nn.sigmoid(-zf)
    end_row = 0 if reverse else TL - 1
    outs = []
    for h in range(N_HEADS):
        sl = slice(h * HD, (h + 1) * HD)
        lf_hi, lf_lo = _split2(lf[:, sl])
        g = G_ref[...]
        e_all = jnp.exp(_dot(g, lf_hi) + _dot(g, lf_lo))
        q_h, k_h, v_h = q[:, sl], kk[:, sl], v[:, sl]
        v_b = v_h.astype(BF16)
        scores = jnp.zeros((TL, TL), F32)
        for l in range(N_LEVELS):
            e = e_all[l * TL:(l + 1) * TL]
            p = _dot_nt((q_h * e).astype(BF16), (k_h * e).astype(BF16))
            scores = jnp.where(lev == l, p, scores)
        e_b = e_all[N_LEVELS * TL:(N_LEVELS + 1) * TL]
        e_e = e_all[(N_LEVELS + 1) * TL:(N_LEVELS + 2) * TL]
        st = s_ref[h]
        o = _dot(scores.astype(BF16), v_b)
        o = o + jnp.sum(q_h * k_h, axis=-1, keepdims=True) * v_h
        o = o + _dot_nt((q_h * e_b).astype(BF16), st.astype(BF16))
        upd = _dot(v_h.T.astype(BF16), (k_h * e_e).astype(BF16))
        s_ref[h] = e_b[end_row:end_row + 1, :] * st + upd
        outs.append(o)
    return jnp.concatenate(outs, axis=-1)


def _lru_tile(zx, zx_p, zx_n, cw, cb, wax, bax, lam, hl_ref, reverse):
    xc = cb + sum(cw[k:k + 1, :] * _shifted(zx, zx_p, zx_n, k - 2) for k in range(4))
    rx = _dot(xc.astype(BF16), wax) + bax
    r = jax.nn.sigmoid(rx[:, :W])
    ig = jax.nn.sigmoid(rx[:, W:])
    log_a = -LRU_C * r * _softplus(-lam)
    a = jnp.exp(log_a)
    u = jnp.sqrt(-jnp.tanh(log_a) * (1.0 + a * a)) * (ig * xc)
    rows = lax.broadcasted_iota(jnp.int32, a.shape, 0)
    d = 1
    while d < TL:
        if reverse:
            a_s, u_s, ok = pltpu.roll(a, TL - d, 0), pltpu.roll(u, TL - d, 0), rows < TL - d
        else:
            a_s, u_s, ok = pltpu.roll(a, d, 0), pltpu.roll(u, d, 0), rows >= d
        u = jnp.where(ok, a * u_s + u, u)
        a = jnp.where(ok, a * a_s, a)
        d *= 2
    hcar = hl_ref[...]
    hseq = u + a * hcar
    end_row = 0 if reverse else TL - 1
    hl_ref[...] = hseq[end_row:end_row + 1, :]
    return hseq


def _mod_kernel(c_ref, w_ref, b_ref, o_ref):
    c = c_ref[...]
    act = c * jax.nn.sigmoid(c)
    o_ref[...] = _dot_split(act, w_ref[...]) + b_ref[...]


def _modulation(cond, w_ada, b_ada):
    n = 6 * D // TN_MOD
    return pl.pallas_call(
        _mod_kernel,
        grid=(DEPTH, n),
        in_specs=[pl.BlockSpec((8, D), lambda l, j: (0, 0)),
                  pl.BlockSpec((None, D, TN_MOD), lambda l, j: (l, 0, j)),
                  pl.BlockSpec((None, 1, TN_MOD), lambda l, j: (l, 0, j))],
        out_specs=pl.BlockSpec((None, 8, TN_MOD), lambda l, j: (l, 0, j)),
        out_shape=jax.ShapeDtypeStruct((DEPTH, 8, 6 * D), F32),
        compiler_params=pltpu.CompilerParams(dimension_semantics=("arbitrary", "arbitrary"),
                                             vmem_limit_bytes=VMEM_LIMIT),
        name="modulation",
    )(cond, w_ada, b_ada.reshape(DEPTH, 1, 6 * D))


def _in_kernel(x_ref, mod_ref, g_ref, w_ref, z_ref, h_scr):
    @pl.when(pl.program_id(1) == 0)
    def _():
        x = x_ref[...]
        h = x * lax.rsqrt(jnp.mean(x * x, axis=-1, keepdims=True) + EPS) * g_ref[...]
        h = h * (1.0 + mod_ref[1:2, :]) + mod_ref[0:1, :]
        h_scr[...] = h.astype(BF16)

    z_ref[...] = _dot(h_scr[...], w_ref[...].astype(BF16)).astype(BF16)


def _in_projection(l, x, mod, norm1, w_in):
    tiles_per_seq = L_LAT // TM_IN
    ctx = (N_CTX * L_CTX) // TM_IN

    def seq(i):
        return jnp.where(i < ctx, 0, 1 + (i - ctx) // tiles_per_seq)

    return pl.pallas_call(
        _in_kernel,
        grid=(T // TM_IN, D_IN // TN_IN),
        in_specs=[pl.BlockSpec((TM_IN, D), lambda i, j: (i, 0)),
                  pl.BlockSpec((None, None, 6, D), lambda i, j: (l, seq(i), 0, 0)),
                  pl.BlockSpec((None, 1, D), lambda i, j: (l, 0, 0)),
                  pl.BlockSpec((None, D, TN_IN), lambda i, j: (l, 0, j))],
        out_specs=pl.BlockSpec((TM_IN, TN_IN), lambda i, j: (i, j)),
        out_shape=jax.ShapeDtypeStruct((T, D_IN), BF16),
        scratch_shapes=[pltpu.VMEM((TM_IN, D), BF16)],
        compiler_params=pltpu.CompilerParams(dimension_semantics=("arbitrary", "arbitrary"),
                                             vmem_limit_bytes=VMEM_LIMIT),
        name="in_projection",
    )(x, mod, norm1, w_in)


def _bwd_kernel(zq, zf, zi, zx, zx_p, zx_n, G_ref, lev_ref, lb_ref, cw_ref, cb_ref, wax_ref, bax_ref,
                lam_ref, hg0_ref, lru0_ref,
                o_ref, h_ref, hgs_ref, lrus_ref, s_scr, hl_scr):
    tt = NT - 1 - pl.program_id(0)
    first, last = _tile_flags(tt)

    @pl.when(last)
    def _():
        for h in range(N_HEADS):
            s_scr[h] = hg0_ref[h].T
        hl_scr[...] = lru0_ref[...]

    has_prev = jnp.where(first, 0.0, 1.0)
    has_next = jnp.where(last, 0.0, 1.0)
    zqv = zq[...].astype(F32)
    q = zqv * jax.nn.sigmoid(zqv)
    o_ref[...] = _hgrn_tile(q, zf[...].astype(F32), zi[...].astype(F32), lb_ref[...], G_ref, lev_ref[...],
                            s_scr, True)
    h_ref[...] = _lru_tile(zx[...].astype(F32), zx_p[...].astype(F32) * has_prev,
                           zx_n[...].astype(F32) * has_next, cw_ref[...], cb_ref[...], wax_ref[...],
                           bax_ref[...], lam_ref[...], hl_scr, True)
    for h in range(N_HEADS):
        hgs_ref[h] = s_scr[h].T
    lrus_ref[...] = hl_scr[...]


def _zslab(k, tile):
    return pl.BlockSpec((TL, W), lambda s: (tile(s), k))


def _zprev(k, tile):
    return pl.BlockSpec((HALO, W), lambda s: (jnp.maximum(tile(s) * (TL // HALO) - 1, 0), k))


def _znext(k, tile):
    return pl.BlockSpec((HALO, W), lambda s: (jnp.minimum((tile(s) + 1) * (TL // HALO), T // HALO - 1), k))


def _const(shape):
    nd = len(shape)
    return pl.BlockSpec(shape, lambda s: (0,) * nd)


def _layer_row(l, n):
    return pl.BlockSpec((None, 1, n), lambda s: (l, 0, 0))


def _reverse_pass(l, z, G, lev, lb, cw, cb, wax, bax, lam, hg_init, lru_init):
    tile = lambda s: NT - 1 - s
    state_specs = [
        pl.BlockSpec((None, None, N_HEADS, HD, HD), lambda s: (_seq_idx(tile(s)), 1, 0, 0, 0)),
        pl.BlockSpec((None, None, 1, W), lambda s: (_seq_idx(tile(s)), 1, 0, 0)),
    ]
    return pl.pallas_call(
        _bwd_kernel,
        grid=(NT,),
        in_specs=[_zslab(S_B_Q, tile), _zslab(S_B_FB, tile), _zslab(S_B_I, tile), _zslab(S_C_X, tile),
                  _zprev(S_C_X, tile), _znext(S_C_X, tile),
                  _const(G.shape), _const(lev.shape),
                  pl.BlockSpec((None, None, 1, W), lambda s: (1, l, 0, 0)),
                  pl.BlockSpec((None, 4, W), lambda s: (l, 0, 0)),
                  _layer_row(l, W),
                  pl.BlockSpec((None, None, W, 2 * W), lambda s: (l, 1, 0, 0)),
                  pl.BlockSpec((None, None, 1, 2 * W), lambda s: (l, 1, 0, 0)),
                  pl.BlockSpec((None, None, 1, W), lambda s: (l, 1, 0, 0)),
                  ] + state_specs,
        out_specs=[pl.BlockSpec((TL, W), lambda s: (tile(s), 0)),
                   pl.BlockSpec((TL, W), lambda s: (tile(s), 0)),
                   pl.BlockSpec((None, N_HEADS, HD, HD), lambda s: (tile(s), 0, 0, 0)),
                   pl.BlockSpec((None, 1, W), lambda s: (tile(s), 0, 0))],
        out_shape=[jax.ShapeDtypeStruct((T, W), F32), jax.ShapeDtypeStruct((T, W), F32),
                   jax.ShapeDtypeStruct((NT, N_HEADS, HD, HD), F32),
                   jax.ShapeDtypeStruct((NT, 1, W), F32)],
        scratch_shapes=[pltpu.VMEM((N_HEADS, HD, HD), F32), pltpu.VMEM((1, W), F32)],
        compiler_params=pltpu.CompilerParams(dimension_semantics=("arbitrary",),
                                             vmem_limit_bytes=VMEM_LIMIT),
        name="reverse_pass",
    )(z, z, z, z, z, z, G, lev, lb, cw, cb, wax, bax, lam, hg_init, lru_init)


def _route(h2, wr, br, cnt_ref):
    logits = _dot_split(h2, wr) + br
    lane = lax.broadcasted_iota(jnp.int32, logits.shape, 1)
    lane_f = lane.astype(F32)
    neg = jnp.float32(-jnp.inf)
    is_g = lane < N_GROUPS
    gl = jnp.where(is_g, logits, neg)
    gmax = jnp.max(gl, axis=-1, keepdims=True)
    g_sel = jnp.min(jnp.where(gl == gmax, lane_f, float(ROUTE_W)), axis=-1, keepdims=True)
    g_p = 1.0 / jnp.sum(jnp.where(is_g, jnp.exp(gl - gmax), 0.0), axis=-1, keepdims=True)
    e_grp = ((lane - EXP_LANE0) >> 2).astype(F32)
    in_grp = (lane >= EXP_LANE0) & (lane < EXP_LANE0 + N_EXPERTS) & (e_grp == g_sel)
    el = jnp.where(in_grp, logits, neg)
    m1 = jnp.max(el, axis=-1, keepdims=True)
    i1 = jnp.min(jnp.where(el == m1, lane_f, float(ROUTE_W)), axis=-1, keepdims=True)
    el2 = jnp.where(lane_f == i1, neg, el)
    m2 = jnp.max(el2, axis=-1, keepdims=True)
    i2 = jnp.min(jnp.where(el2 == m2, lane_f, float(ROUTE_W)), axis=-1, keepdims=True)
    t = jnp.exp(m2 - m1)
    w1 = g_p / (1.0 + t)
    w2 = w1 * t

    oh1 = (lane_f == i1).astype(F32)
    oh2 = (lane_f == i2).astype(F32)
    oh = oh1 + oh2
    r = lax.broadcasted_iota(jnp.int32, (TL, TL), 0)
    cc = lax.broadcasted_iota(jnp.int32, (TL, TL), 1)
    before = _dot(jnp.where(r > cc, 1.0, 0.0).astype(BF16), oh.astype(BF16)) + cnt_ref[...]
    rank1 = jnp.sum(oh1 * before, axis=-1, keepdims=True)
    rank2 = jnp.sum(oh2 * before, axis=-1, keepdims=True)
    cnt_ref[...] += jnp.sum(oh, axis=0, keepdims=True)
    cols = (i1 - EXP_LANE0, i2 - EXP_LANE0, w1, w2, rank1, rank2)
    out = jnp.zeros(logits.shape, F32)
    for k, col in enumerate(cols):
        out = jnp.where(lane == k, col, out)
    return out


def _fwd_kernel(x_ref, mod_ref,
                za_in, za_b, za_c, za_in_p, za_in_n, za_c_p, za_c_n,
                zq, zf, zi, zg, zx, zx_p, zx_n, zy, zu, zv, gt0, gt1, gt2, gt3,
                ob_ref, hb_ref, G_ref, lev_ref,
                lb_ref, hgn_ref, ca_ref, cw_ref, cb_ref, wax_ref, bax_ref, lam_ref,
                lng_ref, lnb_ref, sgw_ref, sgb_ref, wbr_ref, wout_ref, n2_ref, wr_ref, br_ref,
                hg0_ref, lru0_ref,
                xo_ref, h2t_ref, route_ref, cnt_ref, hgs_ref, lrus_ref,
                s_scr, hl_scr):
    tt = pl.program_id(0)
    first, last = _tile_flags(tt)

    @pl.when(tt == 0)
    def _():
        cnt_ref[...] = jnp.zeros(cnt_ref.shape, F32)

    @pl.when(first)
    def _():
        for h in range(N_HEADS):
            s_scr[h] = hg0_ref[h].T
        hl_scr[...] = lru0_ref[...]

    has_prev = jnp.where(first, 0.0, 1.0)
    has_next = jnp.where(last, 0.0, 1.0)
    f32 = lambda r: r[...].astype(F32)

    p = f32(za_c) * f32(za_in)
    p_p = f32(za_c_p) * f32(za_in_p) * has_prev
    p_n = f32(za_c_n) * f32(za_in_n) * has_next
    ca = ca_ref[...]
    ya = f32(za_b) * sum(ca[k:k + 1, :] * _shifted(p, p_p, p_n, k - 1) for k in range(3))

    zqv = f32(zq)
    q = zqv * jax.nn.sigmoid(zqv)
    o = _hgrn_tile(q, f32(zf), f32(zi), lb_ref[...], G_ref, lev_ref[...], s_scr, False) + ob_ref[...]
    zgv = f32(zg)
    gate_b = hgn_ref[...] * (zgv * jax.nn.sigmoid(zgv))
    yb = jnp.concatenate(
        [o[:, h * HD:(h + 1) * HD]
         * lax.rsqrt(jnp.mean(jnp.square(o[:, h * HD:(h + 1) * HD]), axis=-1, keepdims=True) + EPS)
         for h in range(N_HEADS)], axis=-1) * gate_b

    hf = _lru_tile(f32(zx), f32(zx_p) * has_prev, f32(zx_n) * has_next, cw_ref[...], cb_ref[...],
                   wax_ref[...], bax_ref[...], lam_ref[...], hl_scr, False)
    yc = (hf + hb_ref[...]) * jax.nn.gelu(f32(zy))

    u = jax.nn.gelu(f32(zu))
    v = jax.nn.gelu(f32(zv))
    mu = jnp.mean(v, axis=-1, keepdims=True)
    vc = v - mu
    var = jnp.mean(vc * vc, axis=-1, keepdims=True)
    vn = (vc * lax.rsqrt(var + EPS) * lng_ref[...] + lnb_ref[...]).astype(BF16)
    sgb = sgb_ref[...]
    gw = W // SG_GROUPS
    chunks = []
    for n in range(TL // SG_CHUNK):
        cols = []
        for g in range(SG_GROUPS):
            blk = vn[n * SG_CHUNK:(n + 1) * SG_CHUNK, g * gw:(g + 1) * gw]
            cols.append(_dot(sgw_ref[g], blk) + sgb[:, g:g + 1])
        chunks.append(jnp.concatenate(cols, axis=-1))
    yd = u * jnp.concatenate(chunks, axis=0)

    mix = jnp.zeros((TL, D), F32)
    for k, (y, gt) in enumerate(((ya, gt0), (yb, gt1), (yc, gt2), (yd, gt3))):
        mix = mix + jax.nn.sigmoid(f32(gt)) * _dot(y.astype(BF16), wbr_ref[k])
    x = x_ref[...] + mod_ref[2:3, :] * _dot(mix.astype(BF16), wout_ref[...])
    xo_ref[...] = x

    h2 = x * lax.rsqrt(jnp.mean(x * x, axis=-1, keepdims=True) + EPS) * n2_ref[...]
    h2 = h2 * (1.0 + mod_ref[4:5, :]) + mod_ref[3:4, :]
    for s in range(ROW_TILES):
        h2t_ref[pl.ds(s, TL, stride=ROW_TILES), :] = h2[:, s * 128:(s + 1) * 128]
    route_ref[...] = _route(h2, wr_ref[...], br_ref[...], cnt_ref)

    for h in range(N_HEADS):
        hgs_ref[h] = s_scr[h].T
    lrus_ref[...] = hl_scr[...]


def _forward_pass(l, x, mod, z, o_b, h_b, G, lev, lb, hg_norm, conv_a, cw, cb, wax, bax, lam,
                  ln_g, ln_b, sg_w, sg_bt, w_br, w_out, norm2, w_r, b_r, hg_init, lru_init):
    tile = lambda s: s
    gate = lambda k: pl.BlockSpec((TL, D), lambda s: (s, GATE_BLK + k))
    in_specs = [
        pl.BlockSpec((TL, D), lambda s: (s, 0)),
        pl.BlockSpec((None, None, 6, D), lambda s: (l, _seq_idx(s), 0, 0)),
        _zslab(S_A_IN, tile), _zslab(S_A_B, tile), _zslab(S_A_C, tile),
        _zprev(S_A_IN, tile), _znext(S_A_IN, tile), _zprev(S_A_C, tile), _znext(S_A_C, tile),
        _zslab(S_B_Q, tile), _zslab(S_B_FF, tile), _zslab(S_B_I, tile), _zslab(S_B_G, tile),
        _zslab(S_C_X, tile), _zprev(S_C_X, tile), _znext(S_C_X, tile), _zslab(S_C_Y, tile),
        _zslab(S_D_U, tile), _zslab(S_D_V, tile), gate(0), gate(1), gate(2), gate(3),
        pl.BlockSpec((TL, W), lambda s: (s, 0)), pl.BlockSpec((TL, W), lambda s: (s, 0)),
        _const(G.shape), _const(lev.shape),
        pl.BlockSpec((None, None, 1, W), lambda s: (0, l, 0, 0)),
        _layer_row(l, W),
        pl.BlockSpec((None, 3, W), lambda s: (l, 0, 0)),
        pl.BlockSpec((None, 4, W), lambda s: (l, 0, 0)),
        _layer_row(l, W),
        pl.BlockSpec((None, None, W, 2 * W), lambda s: (l, 0, 0, 0)),
        pl.BlockSpec((None, None, 1, 2 * W), lambda s: (l, 0, 0, 0)),
        pl.BlockSpec((None, None, 1, W), lambda s: (l, 0, 0, 0)),
        _layer_row(l, W), _layer_row(l, W),
        pl.BlockSpec((None, SG_GROUPS, SG_CHUNK, SG_CHUNK), lambda s: (l, 0, 0, 0)),
        pl.BlockSpec((None, SG_CHUNK, SG_GROUPS), lambda s: (l, 0, 0)),
        pl.BlockSpec((None, 4, W, D), lambda s: (l, 0, 0, 0)),
        pl.BlockSpec((None, D, D), lambda s: (l, 0, 0)),
        _layer_row(l, D),
        pl.BlockSpec((None, D, ROUTE_W), lambda s: (l, 0, 0)),
        _layer_row(l, ROUTE_W),
        pl.BlockSpec((None, None, N_HEADS, HD, HD), lambda s: (_seq_idx(s), 0, 0, 0, 0)),
        pl.BlockSpec((None, None, 1, W), lambda s: (_seq_idx(s), 0, 0, 0)),
    ]
    return pl.pallas_call(
        _fwd_kernel,
        grid=(NT,),
        in_specs=in_specs,
        out_specs=[pl.BlockSpec((TL, D), lambda s: (s, 0)),
                   pl.BlockSpec((TL * ROW_TILES, 128), lambda s: (s, 0)),
                   pl.BlockSpec((TL, ROUTE_W), lambda s: (s, 0)),
                   pl.BlockSpec((1, ROUTE_W), lambda s: (0, 0)),
                   pl.BlockSpec((None, N_HEADS, HD, HD), lambda s: (s, 0, 0, 0)),
                   pl.BlockSpec((None, 1, W), lambda s: (s, 0, 0))],
        out_shape=[jax.ShapeDtypeStruct((T, D), F32), jax.ShapeDtypeStruct((T * ROW_TILES, 128), F32),
                   jax.ShapeDtypeStruct((T, ROUTE_W), F32), jax.ShapeDtypeStruct((1, ROUTE_W), F32),
                   jax.ShapeDtypeStruct((NT, N_HEADS, HD, HD), F32),
                   jax.ShapeDtypeStruct((NT, 1, W), F32)],
        scratch_shapes=[pltpu.VMEM((N_HEADS, HD, HD), F32), pltpu.VMEM((1, W), F32)],
        compiler_params=pltpu.CompilerParams(dimension_semantics=("arbitrary",),
                                             vmem_limit_bytes=VMEM_LIMIT),
        name="forward_pass",
    )(x, mod, *([z] * 21), o_b, h_b, G, lev, lb, hg_norm, conv_a, cw, cb, wax, bax, lam,
      ln_g, ln_b, sg_w, sg_bt, w_br, w_out, norm2, w_r, b_r, hg_init, lru_init)


def _token_rows(ref, row, n=1):
    return ref.at[pl.ds(row * ROW_TILES, n * ROW_TILES), :]


def _read_rows(ref, first_row, n):
    return jnp.concatenate([ref[pl.ds(first_row * ROW_TILES + s, n, stride=ROW_TILES), :]
                            for s in range(ROW_TILES)], axis=1)


def _padded(n):
    return ((n + TME - 1) // TME) * TME


def _dispatch_kernel(pos_ref, cnt_ref, h2t_ref, xs_hbm, zero_buf, sem):
    s = pl.program_id(0)

    def copies(i):
        tok = s * TL + i
        return [pltpu.make_async_copy(_token_rows(h2t_ref, i), _token_rows(xs_hbm, pos_ref[2 * tok + k]), sem)
                for k in range(2)]

    def start(i, c):
        for cp in copies(i):
            cp.start()
        return c

    def wait(i, c):
        for cp in copies(i):
            cp.wait()
        return c

    lax.fori_loop(0, TL, start, 0, unroll=8)
    lax.fori_loop(0, TL, wait, 0, unroll=8)

    @pl.when(s == NT - 1)
    def _():
        zero_buf[...] = jnp.zeros(zero_buf.shape, F32)

        def pad_copy(j):
            return pltpu.make_async_copy(_token_rows(zero_buf, 0), _token_rows(xs_hbm, j), sem)

        def pad_start(j, c):
            pad_copy(j).start()
            return c

        def pad_wait(j, c):
            pad_copy(j).wait()
            return c

        def tail_copy(t):
            return pltpu.make_async_copy(zero_buf, _token_rows(xs_hbm, t * TME, TME), sem)

        def tail_start(t, c):
            tail_copy(t).start()
            return c

        def tail_wait(t, c):
            tail_copy(t).wait()
            return c

        seg = 0
        for e in range(N_EXPERTS):
            n = cnt_ref[e]
            lax.fori_loop(seg + n, seg + _padded(n), pad_start, 0)
            lax.fori_loop(seg + n, seg + _padded(n), pad_wait, 0)
            seg = seg + _padded(n)
        lax.fori_loop(seg // TME, N_ETILES, tail_start, 0)
        lax.fori_loop(seg // TME, N_ETILES, tail_wait, 0)


def _dispatch(pos, cnt, h2t):
    return pl.pallas_call(
        _dispatch_kernel,
        grid_spec=pltpu.PrefetchScalarGridSpec(
            num_scalar_prefetch=2, grid=(NT,),
            in_specs=[pl.BlockSpec((TL * ROW_TILES, 128), lambda s, pos, cnt: (s, 0))],
            out_specs=pl.BlockSpec(memory_space=pl.ANY),
            scratch_shapes=[pltpu.VMEM((TME * ROW_TILES, 128), F32), pltpu.SemaphoreType.DMA]),
        out_shape=jax.ShapeDtypeStruct((N_ROWS * ROW_TILES, 128), F32),
        compiler_params=pltpu.CompilerParams(dimension_semantics=("arbitrary",)),
        name="dispatch",
    )(pos, cnt, h2t)


def _expert_kernel(te_ref, tv_ref, xs_ref, wg_ref, wu_ref, wd_ref, ys_ref, wg_s, wu_s, wd_s):
    i = pl.program_id(0)

    @pl.when(tv_ref[i] == 0)
    def _():
        ys_ref[...] = jnp.zeros(ys_ref.shape, F32)

    @pl.when(tv_ref[i] == 1)
    def _():
        @pl.when(jnp.logical_or(i == 0, te_ref[i] != te_ref[jnp.maximum(i - 1, 0)]))
        def _():
            wg_s[...] = wg_ref[...].astype(BF16)
            wu_s[...] = wu_ref[...].astype(BF16)
            wd_s[...] = wd_ref[...].astype(BF16)

        x = _read_rows(xs_ref, 0, TME).astype(BF16)
        a = _dot(x, wg_s[...])
        u = _dot(x, wu_s[...])
        y = _dot((a * jax.nn.sigmoid(a) * u).astype(BF16), wd_s[...])
        for s in range(ROW_TILES):
            ys_ref[pl.ds(s, TME, stride=ROW_TILES), :] = y[:, s * 128:(s + 1) * 128]


def _grouped_experts(l, te, tv, xs, w_gate, w_up, w_down):
    rows = pl.BlockSpec((TME * ROW_TILES, 128), lambda i, te, tv: (i, 0))
    return pl.pallas_call(
        _expert_kernel,
        grid_spec=pltpu.PrefetchScalarGridSpec(
            num_scalar_prefetch=2, grid=(N_ETILES,),
            in_specs=[rows,
                      pl.BlockSpec((None, None, D, D_EXPERT), lambda i, te, tv: (l, te[i], 0, 0)),
                      pl.BlockSpec((None, None, D, D_EXPERT), lambda i, te, tv: (l, te[i], 0, 0)),
                      pl.BlockSpec((None, None, D_EXPERT, D), lambda i, te, tv: (l, te[i], 0, 0))],
            out_specs=rows,
            scratch_shapes=[pltpu.VMEM((D, D_EXPERT), BF16), pltpu.VMEM((D, D_EXPERT), BF16),
                            pltpu.VMEM((D_EXPERT, D), BF16)]),
        out_shape=jax.ShapeDtypeStruct((N_ROWS * ROW_TILES, 128), F32),
        compiler_params=pltpu.CompilerParams(dimension_semantics=("arbitrary",),
                                             vmem_limit_bytes=VMEM_LIMIT),
        name="grouped_experts",
    )(te, tv, xs, w_gate, w_up, w_down)


def _combine_kernel(pos_ref, x_ref, mod_ref, route_ref, ys_hbm, o_ref, buf, sem):
    s = pl.program_id(0)

    def copies(i):
        tok = s * TL + i
        return [pltpu.make_async_copy(_token_rows(ys_hbm, pos_ref[2 * tok + k]), _token_rows(buf, k * TL + i), sem)
                for k in range(2)]

    def start(i, c):
        for cp in copies(i):
            cp.start()
        return c

    def wait(i, c):
        for cp in copies(i):
            cp.wait()
        return c

    lax.fori_loop(0, TL, start, 0, unroll=8)
    lax.fori_loop(0, TL, wait, 0, unroll=8)
    route = route_ref[...]
    y = route[:, 2:3] * _read_rows(buf, 0, TL) + route[:, 3:4] * _read_rows(buf, TL, TL)
    o_ref[...] = x_ref[...] + mod_ref[5:6, :] * y


def _combine(l, pos, x, mod, route, ys):
    return pl.pallas_call(
        _combine_kernel,
        grid_spec=pltpu.PrefetchScalarGridSpec(
            num_scalar_prefetch=1, grid=(NT,),
            in_specs=[pl.BlockSpec((TL, D), lambda s, pos: (s, 0)),
                      pl.BlockSpec((None, None, 6, D), lambda s, pos: (l, _seq_idx(s), 0, 0)),
                      pl.BlockSpec((TL, ROUTE_W), lambda s, pos: (s, 0)),
                      pl.BlockSpec(memory_space=pl.ANY)],
            out_specs=pl.BlockSpec((TL, D), lambda s, pos: (s, 0)),
            scratch_shapes=[pltpu.VMEM((2 * TL * ROW_TILES, 128), F32), pltpu.SemaphoreType.DMA]),
        out_shape=jax.ShapeDtypeStruct((T, D), F32),
        compiler_params=pltpu.CompilerParams(dimension_semantics=("arbitrary",)),
        name="combine",
    )(pos, x, mod, route, ys)


def _expert_pass(l, x, mod, h2t, route, cnt, w_gate, w_up, w_down):
    ri = route[:, :6].astype(jnp.int32)
    cnt = cnt[0, EXP_LANE0:EXP_LANE0 + N_EXPERTS].astype(jnp.int32)
    tiles = (cnt + TME - 1) // TME
    ends = jnp.cumsum(tiles)
    seg = (ends - tiles) * TME
    pos = jnp.stack([seg[ri[:, 0]] + ri[:, 4], seg[ri[:, 1]] + ri[:, 5]], axis=1).reshape(-1)
    i = jnp.arange(N_ETILES, dtype=jnp.int32)
    tv = (i < ends[-1]).astype(jnp.int32)
    te = jnp.sum(ends[None, :] <= jnp.minimum(i, ends[-1] - 1)[:, None], axis=1, dtype=jnp.int32)
    xs = _dispatch(pos, cnt, h2t)
    ys = _grouped_experts(l, te, tv, xs, w_gate, w_up, w_down)
    return _combine(l, pos, x, mod, route, ys)


def _final_kernel(x_ref, g_ref, o_ref):
    x = x_ref[...]
    o_ref[...] = x * lax.rsqrt(jnp.mean(x * x, axis=-1, keepdims=True) + EPS) * g_ref[...]


def _final_norm(x, g):
    tm = 1024
    return pl.pallas_call(
        _final_kernel,
        grid=(T // tm,),
        in_specs=[pl.BlockSpec((tm, D), lambda i: (i, 0)), pl.BlockSpec((1, D), lambda i: (0, 0))],
        out_specs=pl.BlockSpec((tm, D), lambda i: (i, 0)),
        out_shape=jax.ShapeDtypeStruct((T, D), F32),
        compiler_params=pltpu.CompilerParams(dimension_semantics=("arbitrary",)),
        name="final_norm",
    )(x, g.reshape(1, D))


def _grid_pos_embed(rows, d):
    nf = d // 4
    freqs = jnp.exp(-math.log(10000.0) * jnp.arange(nf, dtype=F32) / nf)
    r = jnp.arange(rows, dtype=F32)[:, None] * freqs
    cl = jnp.arange(GRID_W, dtype=F32)[:, None] * freqs
    r_emb = jnp.concatenate([jnp.sin(r), jnp.cos(r)], axis=-1)
    c_emb = jnp.concatenate([jnp.sin(cl), jnp.cos(cl)], axis=-1)
    emb = jnp.concatenate([jnp.broadcast_to(r_emb[:, None], (rows, GRID_W, d // 2)),
                           jnp.broadcast_to(c_emb[None], (rows, GRID_W, d // 2))], axis=-1)
    return emb.reshape(rows * GRID_W, d)


def _block_diag(w):
    n, c = w.shape[-3], w.shape[-1]
    eye = jnp.eye(n, dtype=w.dtype)
    full = w[..., :, :, None, :] * eye[:, None, :, None]
    return full.reshape(*w.shape[:-3], n * c, n * c)


def kernel(x_prompt, x_sample, state_hgrn, state_lru, c, c_ctx, w_ada, b_ada, norm1, norm2, w_in, conv_a,
           hg_lb, hg_norm, lru_conv_w, lru_conv_b, lru_wa, lru_ba, lru_wx, lru_bx, lru_lam, sg_ln_g, sg_ln_b,
           sg_w, sg_b, w_br, w_out, w_rg, b_rg, w_re, b_re, w_gate, w_up, w_down, norm_f):
    assert x_prompt.shape == (N_CTX, L_CTX, D) and x_sample.shape == (N_LAT, L_LAT, D)

    xs = x_sample.astype(F32) + _grid_pos_embed(L_LAT // GRID_W, D)
    x = jnp.concatenate([x_prompt.astype(F32).reshape(-1, D), xs.reshape(-1, D)], axis=0)

    cond = jnp.zeros((8, D), F32).at[0].set(c_ctx.astype(F32)).at[1:1 + N_LAT].set(c.astype(F32))
    mod = _modulation(cond, w_ada, b_ada).reshape(DEPTH, 8, 6, D)

    cs = jnp.cumsum(jax.nn.softmax(hg_lb.astype(F32), axis=1), axis=1)
    lb = (cs - cs[:, :1]).reshape(2, DEPTH, 1, W)
    wax = jnp.concatenate([_block_diag(lru_wa), _block_diag(lru_wx)], axis=-1).astype(BF16)
    bax = jnp.concatenate([lru_ba, lru_bx], axis=-1).reshape(DEPTH, 2, 1, 2 * W)
    lam = lru_lam.reshape(DEPTH, 2, 1, W)
    row = lambda a: a.reshape(DEPTH, 1, a.shape[-1])
    sg_bt = jnp.swapaxes(sg_b, 1, 2)
    w_r = jnp.zeros((DEPTH, D, ROUTE_W), F32).at[:, :, :N_GROUPS].set(w_rg)
    w_r = w_r.at[:, :, EXP_LANE0:EXP_LANE0 + N_EXPERTS].set(w_re)
    b_r = jnp.zeros((DEPTH, ROUTE_W), F32).at[:, :N_GROUPS].set(b_rg)
    b_r = b_r.at[:, EXP_LANE0:EXP_LANE0 + N_EXPERTS].set(b_re)
    w_br_b = w_br.astype(BF16)
    w_out_b = w_out.astype(BF16)
    sg_w_b = sg_w.astype(BF16)

    consts = [_scan_constants(rev) for rev in (False, True)]
    G_f, G_b = (jnp.asarray(g, BF16) for g, _ in consts)
    lev_f, lev_b = (jnp.asarray(v) for _, v in consts)

    hg_states, lru_states = [], []
    for l in range(DEPTH):
        hg_init = jnp.concatenate([jnp.zeros((1, 2, N_HEADS, HD, HD), F32), state_hgrn[:, l].astype(F32)], axis=0)
        lru_init = jnp.concatenate([jnp.zeros((1, 2, W), F32), state_lru[:, l].astype(F32)], axis=0)
        lru_init = lru_init.reshape(1 + N_LAT, 2, 1, W)
        z = _in_projection(l, x, mod, row(norm1), w_in)
        o_b, h_b, hgs_b, lrus_b = _reverse_pass(l, z, G_b, lev_b, lb, lru_conv_w, row(lru_conv_b), wax, bax, lam,
                                                hg_init, lru_init)
        x, h2t, route, cnt, hgs_f, lrus_f = _forward_pass(
            l, x, mod, z, o_b, h_b, G_f, lev_f, lb, row(hg_norm), conv_a, lru_conv_w, row(lru_conv_b), wax, bax,
            lam, row(sg_ln_g), row(sg_ln_b), sg_w_b, sg_bt, w_br_b, w_out_b, row(norm2), w_r, row(b_r),
            hg_init, lru_init)
        x = _expert_pass(l, x, mod, h2t, route, cnt, w_gate, w_up, w_down)
        hg_states.append(jnp.stack([hgs_f[:CTX_TILES], hgs_b[:CTX_TILES]], axis=1))
        lru_states.append(jnp.stack([lrus_f[:CTX_TILES, 0], lrus_b[:CTX_TILES, 0]], axis=1))

    y = _final_norm(x, norm_f)
    n_ctx = N_CTX * L_CTX
    y_prompt = y[:n_ctx].reshape(N_CTX, L_CTX, D)
    y_sample = y[n_ctx:].reshape(N_LAT, L_LAT, D)
    return y_prompt, y_sample, jnp.stack(hg_states, axis=1), jnp.stack(lru_states, axis=1)
```

```python
import functools
import math

import numpy as np
import jax
import jax.numpy as jnp
from jax import lax
from jax.experimental import pallas as pl
from jax.experimental.pallas import tpu as pltpu

F32 = jnp.float32
BF16 = jnp.bfloat16

D = 1024
W = 512
DEPTH = 4
GRID_W = 64
N_HEADS = 4
HD = 128
LRU_BLOCKS = 8
LRU_BLOCK = W // LRU_BLOCKS
LRU_C = 8.0
SG_CHUNK = 128
SG_GROUPS = 4
N_GROUPS = 4
EXP_PER_GROUP = 4
N_EXPERTS = 16
D_EXPERT = 512
EPS = 1e-6
D_IN = 12 * W + 4 * D

S_A_IN, S_A_B, S_A_C, S_B_Q, S_B_FF, S_B_FB, S_B_I, S_B_G, S_C_X, S_C_Y, S_D_U, S_D_V = range(12)
GATE_BLK = (12 * W) // D

N_CTX, L_CTX = 16, 256
N_LAT, L_LAT = 2, 2048
T = N_CTX * L_CTX + N_LAT * L_LAT
TL = 256
NT = T // TL
CTX_TILES = (N_CTX * L_CTX) // TL
LAT_TILES = L_LAT // TL
HC = 128
NCH = TL // HC
LOG_HC = 7
N_LEVELS = LOG_HC
N_MXU_LEVELS = 3
SUB = 8
HALO = 16
ROUTE_W = 128
EXP_LANE0 = N_GROUPS

VMEM_LIMIT = 56 * 1024 * 1024

TM_IN, TN_IN = 1024, 1024
TN_MOD = 1536
ROW_TILES = D // 128
TME = 256
N_ETILES = (2 * T) // TME + N_EXPERTS
N_ROWS = N_ETILES * TME


def _scan_constants(reverse):
    i = np.arange(TL)[:, None]
    s = np.arange(TL)[None, :]
    mats = [(i // HC == s // HC) & (s <= i)]
    for B in (8, 4, 2):
        m = (i // B) * B + B // 2 - 1
        mats.append(np.where(i > m, (s > m) & (s <= i), (s > i) & (s <= m)))
    if reverse:
        mats = [g[::-1, ::-1] for g in mats]
    G = np.concatenate(mats, 0).astype(np.float32)
    r = np.arange(HC)[:, None]
    c = np.arange(HC)[None, :]
    lev = (LOG_HC - 1) - np.floor(np.log2(np.maximum(r ^ c, 1))).astype(np.int32)
    lev = np.where(r == c, N_LEVELS, lev)
    causal = (r < c) if reverse else (r > c)
    lev = np.where(causal | (r == c), lev, N_LEVELS + 1).astype(np.int32)
    return G, lev


def _split2(x):
    hi = x.astype(BF16)
    lo = (x - hi.astype(F32)).astype(BF16)
    return hi, lo


def _dot(a, b):
    return jnp.dot(a, b, preferred_element_type=F32)


def _dot_nt(a, b):
    return lax.dot_general(a, b, (((1,), (1,)), ((), ())), preferred_element_type=F32)


def _dot_split(a, b):
    ah, al = _split2(a)
    bh, bl = _split2(b)
    return _dot(ah, bh) + (_dot(ah, bl) + _dot(al, bh))


def _log_sigmoid(x):
    return jnp.minimum(x, 0.0) - jnp.log1p(jnp.exp(-jnp.abs(x)))


def _softplus(x):
    return jnp.maximum(x, 0.0) + jnp.log1p(jnp.exp(-jnp.abs(x)))


def _tile_flags(tt):
    is_lat = tt >= CTX_TILES
    pos = tt & (LAT_TILES - 1)
    first = jnp.logical_or(jnp.logical_not(is_lat), pos == 0)
    last = jnp.logical_or(jnp.logical_not(is_lat), pos == LAT_TILES - 1)
    return first, last


def _seq_idx(t):
    return jnp.where(t < CTX_TILES, 0, 1 + (t - CTX_TILES) // LAT_TILES)


def _shifted(x, prev, nxt, k):
    if k == 0:
        return x
    rows = lax.broadcasted_iota(jnp.int32, (SUB, x.shape[1]), 0)
    if k < 0:
        y = pltpu.roll(x, -k, 0)
        edge = y[:SUB]
        for r in range(-k):
            edge = jnp.where(rows == r, prev[HALO + k + r:HALO + k + r + 1, :], edge)
        return jnp.concatenate([edge, y[SUB:]], axis=0)
    y = pltpu.roll(x, TL - k, 0)
    edge = y[TL - SUB:]
    for r in range(k):
        edge = jnp.where(rows == SUB - k + r, nxt[r:r + 1, :], edge)
    return jnp.concatenate([y[:TL - SUB], edge], axis=0)


def _block_anchor(b, B, a):
    return jnp.concatenate([jnp.broadcast_to(b[s0 + a:s0 + a + 1, :], (B, b.shape[1]))
                            for s0 in range(0, TL, B)], axis=0)


def _hgrn_tile(q, zf, v, lb, G_ref, lev, s_ref, reverse):
    ls = _log_sigmoid(zf)
    bb = jnp.log1p(-lb) + ls
    la = jnp.log(lb)
    lf = jnp.maximum(la, bb) + jnp.log1p(jnp.exp(-jnp.abs(la - bb)))
    kk = (1.0 - lb) * jax.nn.sigmoid(-zf)

    hi = lf.astype(BF16)
    r1 = lf - hi.astype(F32)
    mid = r1.astype(BF16)
    lo = (r1 - mid.astype(F32)).astype(BF16)
    gb = G_ref[0:TL, :]
    b = _dot(gb, hi) + (_dot(gb, mid) + _dot(gb, lo))
    gs = G_ref[TL:(1 + N_MXU_LEVELS) * TL, :]
    e_small = jnp.exp(_dot(gs, hi) + _dot(gs, mid))
    e_lev = []
    for l in range(N_LEVELS - N_MXU_LEVELS):
        B = HC >> l
        anchor = _block_anchor(b, B, B // 2 if reverse else B // 2 - 1)
        e_lev.append(jnp.exp(-jnp.abs(b - anchor)))
    for l in range(N_MXU_LEVELS):
        e_lev.append(e_small[l * TL:(l + 1) * TL])
    end = 0 if reverse else HC - 1
    e_b = jnp.exp(b)
    e_e = jnp.exp(_block_anchor(b, HC, end) - b)

    outs = []
    for h in range(N_HEADS):
        sl = slice(h * HD, (h + 1) * HD)
        st = s_ref[h]
        o_h = [None] * NCH
        for c in (reversed(range(NCH)) if reverse else range(NCH)):
            rs = slice(c * HC, (c + 1) * HC)
            q_c, k_c, v_c = q[rs, sl], kk[rs, sl], v[rs, sl]
            d = jnp.zeros((HC, HC), F32)
            for l in range(N_LEVELS):
                e = e_lev[l][rs, sl]
                p = _dot_nt((q_c * e).astype(BF16), (k_c * e).astype(BF16))
                d = jnp.where(lev == l, p, d)
            o = _dot(d.astype(BF16), v_c.astype(BF16))
            o = o + jnp.sum(q_c * k_c, axis=-1, keepdims=True) * v_c
            o = o + _dot_nt((q_c * e_b[rs, sl]).astype(BF16), st.astype(BF16))
            upd = _dot(v_c.T.astype(BF16), (k_c * e_e[rs, sl]).astype(BF16))
            st = e_b[c * HC + end:c * HC + end + 1, sl] * st + upd
            o_h[c] = o
        s_ref[h] = st
        outs.append(jnp.concatenate(o_h, axis=0))
    return jnp.concatenate(outs, axis=-1)


def _lin_scan(a, u, carry, reverse):
    sub = lax.broadcasted_iota(jnp.int32, a.shape, 0) & (SUB - 1)
    d = 1
    while d < SUB:
        if reverse:
            a_s, u_s, ok = pltpu.roll(a, TL - d, 0), pltpu.roll(u, TL - d, 0), sub < SUB - d
        else:
            a_s, u_s, ok = pltpu.roll(a, d, 0), pltpu.roll(u, d, 0), sub >= d
        u = jnp.where(ok, a * u_s + u, u)
        a = jnp.where(ok, a * a_s, a)
        d *= 2
    n = TL // SUB
    blocks = [None] * n
    for g in (reversed(range(n)) if reverse else range(n)):
        blk = u[g * SUB:(g + 1) * SUB] + a[g * SUB:(g + 1) * SUB] * carry
        carry = blk[0:1] if reverse else blk[SUB - 1:SUB]
        blocks[g] = blk
    return jnp.concatenate(blocks, axis=0), carry


def _lru_tile(zx, zx_p, zx_n, cw, cb, wax, bax, lam, hl_ref, reverse):
    xc = cb + sum(cw[k:k + 1, :] * _shifted(zx, zx_p, zx_n, k - 2) for k in range(4))
    rx = _dot(xc.astype(BF16), wax) + bax
    r = jax.nn.sigmoid(rx[:, :W])
    ig = jax.nn.sigmoid(rx[:, W:])
    log_a = -LRU_C * r * _softplus(-lam)
    a = jnp.exp(log_a)
    u = jnp.sqrt(-jnp.tanh(log_a) * (1.0 + a * a)) * (ig * xc)
    hseq, carry = _lin_scan(a, u, hl_ref[...], reverse)
    hl_ref[...] = carry
    return hseq


def _mod_kernel(c_ref, w_ref, b_ref, o_ref):
    c = c_ref[...]
    act = c * jax.nn.sigmoid(c)
    o_ref[...] = _dot_split(act, w_ref[...]) + b_ref[...]


def _modulation(cond, w_ada, b_ada):
    n = 6 * D // TN_MOD
    return pl.pallas_call(
        _mod_kernel,
        grid=(DEPTH, n),
        in_specs=[pl.BlockSpec((8, D), lambda l, j: (0, 0)),
                  pl.BlockSpec((None, D, TN_MOD), lambda l, j: (l, 0, j)),
                  pl.BlockSpec((None, 1, TN_MOD), lambda l, j: (l, 0, j))],
        out_specs=pl.BlockSpec((None, 8, TN_MOD), lambda l, j: (l, 0, j)),
        out_shape=jax.ShapeDtypeStruct((DEPTH, 8, 6 * D), F32),
        compiler_params=pltpu.CompilerParams(dimension_semantics=("arbitrary", "arbitrary"),
                                             vmem_limit_bytes=VMEM_LIMIT),
        name="modulation",
    )(cond, w_ada, b_ada.reshape(DEPTH, 1, 6 * D))


def _in_kernel(x_ref, mod_ref, g_ref, w_ref, z_ref, h_scr):
    @pl.when(pl.program_id(1) == 0)
    def _():
        x = x_ref[...]
        h = x * lax.rsqrt(jnp.mean(x * x, axis=-1, keepdims=True) + EPS) * g_ref[...]
        h = h * (1.0 + mod_ref[1:2, :]) + mod_ref[0:1, :]
        h_scr[...] = h.astype(BF16)

    z_ref[...] = _dot(h_scr[...], w_ref[...].astype(BF16)).astype(BF16)


def _in_projection(l, x, mod, norm1, w_in):
    tiles_per_seq = L_LAT // TM_IN
    ctx = (N_CTX * L_CTX) // TM_IN

    def seq(i):
        return jnp.where(i < ctx, 0, 1 + (i - ctx) // tiles_per_seq)

    return pl.pallas_call(
        _in_kernel,
        grid=(T // TM_IN, D_IN // TN_IN),
        in_specs=[pl.BlockSpec((TM_IN, D), lambda i, j: (i, 0)),
                  pl.BlockSpec((None, None, 6, D), lambda i, j: (l, seq(i), 0, 0)),
                  pl.BlockSpec((None, 1, D), lambda i, j: (l, 0, 0)),
                  pl.BlockSpec((None, D, TN_IN), lambda i, j: (l, 0, j))],
        out_specs=pl.BlockSpec((TM_IN, TN_IN), lambda i, j: (i, j)),
        out_shape=jax.ShapeDtypeStruct((T, D_IN), BF16),
        scratch_shapes=[pltpu.VMEM((TM_IN, D), BF16)],
        compiler_params=pltpu.CompilerParams(dimension_semantics=("arbitrary", "arbitrary"),
                                             vmem_limit_bytes=VMEM_LIMIT),
        name="in_projection",
    )(x, mod, norm1, w_in)


def _bwd_kernel(zq, zf, zi, zx, zx_p, zx_n, G_ref, lev_ref, lb_ref, cw_ref, cb_ref, wax_ref, bax_ref,
                lam_ref, hg0_ref, lru0_ref,
                o_ref, h_ref, hgs_ref, lrus_ref, s_scr, hl_scr):
    tt = NT - 1 - pl.program_id(0)
    first, last = _tile_flags(tt)

    @pl.when(last)
    def _():
        for h in range(N_HEADS):
            s_scr[h] = hg0_ref[h].T
        hl_scr[...] = lru0_ref[...]

    has_prev = jnp.where(first, 0.0, 1.0)
    has_next = jnp.where(last, 0.0, 1.0)
    zqv = zq[...].astype(F32)
    q = zqv * jax.nn.sigmoid(zqv)
    o_ref[...] = _hgrn_tile(q, zf[...].astype(F32), zi[...].astype(F32), lb_ref[...], G_ref, lev_ref[...],
                            s_scr, True)
    h_ref[...] = _lru_tile(zx[...].astype(F32), zx_p[...].astype(F32) * has_prev,
                           zx_n[...].astype(F32) * has_next, cw_ref[...], cb_ref[...], wax_ref[...],
                           bax_ref[...], lam_ref[...], hl_scr, True)
    for h in range(N_HEADS):
        hgs_ref[h] = s_scr[h].T
    lrus_ref[...] = hl_scr[...]


def _zslab(k, tile):
    return pl.BlockSpec((TL, W), lambda s: (tile(s), k))


def _zprev(k, tile):
    return pl.BlockSpec((HALO, W), lambda s: (jnp.maximum(tile(s) * (TL // HALO) - 1, 0), k))


def _znext(k, tile):
    return pl.BlockSpec((HALO, W), lambda s: (jnp.minimum((tile(s) + 1) * (TL // HALO), T // HALO - 1), k))


def _const(shape):
    nd = len(shape)
    return pl.BlockSpec(shape, lambda s: (0,) * nd)


def _layer_row(l, n):
    return pl.BlockSpec((None, 1, n), lambda s: (l, 0, 0))


def _reverse_pass(l, z, G, lev, lb, cw, cb, wax, bax, lam, hg_init, lru_init):
    tile = lambda s: NT - 1 - s
    state_specs = [
        pl.BlockSpec((None, None, None, N_HEADS, HD, HD), lambda s: (_seq_idx(tile(s)), l, 1, 0, 0, 0)),
        pl.BlockSpec((None, None, None, 1, W), lambda s: (_seq_idx(tile(s)), l, 1, 0, 0)),
    ]
    return pl.pallas_call(
        _bwd_kernel,
        grid=(NT,),
        in_specs=[_zslab(S_B_Q, tile), _zslab(S_B_FB, tile), _zslab(S_B_I, tile), _zslab(S_C_X, tile),
                  _zprev(S_C_X, tile), _znext(S_C_X, tile),
                  _const(G.shape), _const(lev.shape),
                  pl.BlockSpec((None, None, 1, W), lambda s: (1, l, 0, 0)),
                  pl.BlockSpec((None, 4, W), lambda s: (l, 0, 0)),
                  _layer_row(l, W),
                  pl.BlockSpec((None, None, W, 2 * W), lambda s: (l, 1, 0, 0)),
                  pl.BlockSpec((None, None, 1, 2 * W), lambda s: (l, 1, 0, 0)),
                  pl.BlockSpec((None, None, 1, W), lambda s: (l, 1, 0, 0)),
                  ] + state_specs,
        out_specs=[pl.BlockSpec((TL, W), lambda s: (tile(s), 0)),
                   pl.BlockSpec((TL, W), lambda s: (tile(s), 0)),
                   pl.BlockSpec((None, N_HEADS, HD, HD), lambda s: (tile(s), 0, 0, 0)),
                   pl.BlockSpec((None, 1, W), lambda s: (tile(s), 0, 0))],
        out_shape=[jax.ShapeDtypeStruct((T, W), F32), jax.ShapeDtypeStruct((T, W), F32),
                   jax.ShapeDtypeStruct((NT, N_HEADS, HD, HD), F32),
                   jax.ShapeDtypeStruct((NT, 1, W), F32)],
        scratch_shapes=[pltpu.VMEM((N_HEADS, HD, HD), F32), pltpu.VMEM((1, W), F32)],
        compiler_params=pltpu.CompilerParams(dimension_semantics=("arbitrary",),
                                             vmem_limit_bytes=VMEM_LIMIT),
        name="reverse_pass",
    )(z, z, z, z, z, z, G, lev, lb, cw, cb, wax, bax, lam, hg_init, lru_init)


def _route(h2, wr, br, cnt_ref):
    logits = _dot_split(h2, wr) + br
    lane = lax.broadcasted_iota(jnp.int32, logits.shape, 1)
    lane_f = lane.astype(F32)
    neg = jnp.float32(-jnp.inf)
    is_g = lane < N_GROUPS
    gl = jnp.where(is_g, logits, neg)
    gmax = jnp.max(gl, axis=-1, keepdims=True)
    g_sel = jnp.min(jnp.where(gl == gmax, lane_f, float(ROUTE_W)), axis=-1, keepdims=True)
    g_p = 1.0 / jnp.sum(jnp.where(is_g, jnp.exp(gl - gmax), 0.0), axis=-1, keepdims=True)
    e_grp = ((lane - EXP_LANE0) >> 2).astype(F32)
    in_grp = (lane >= EXP_LANE0) & (lane < EXP_LANE0 + N_EXPERTS) & (e_grp == g_sel)
    el = jnp.where(in_grp, logits, neg)
    m1 = jnp.max(el, axis=-1, keepdims=True)
    i1 = jnp.min(jnp.where(el == m1, lane_f, float(ROUTE_W)), axis=-1, keepdims=True)
    el2 = jnp.where(lane_f == i1, neg, el)
    m2 = jnp.max(el2, axis=-1, keepdims=True)
    i2 = jnp.min(jnp.where(el2 == m2, lane_f, float(ROUTE_W)), axis=-1, keepdims=True)
    t = jnp.exp(m2 - m1)
    w1 = g_p / (1.0 + t)
    w2 = w1 * t

    oh1 = (lane_f == i1).astype(F32)
    oh2 = (lane_f == i2).astype(F32)
    oh = oh1 + oh2
    r = lax.broadcasted_iota(jnp.int32, (TL, TL), 0)
    cc = lax.broadcasted_iota(jnp.int32, (TL, TL), 1)
    before = _dot(jnp.where(r > cc, 1.0, 0.0).astype(BF16), oh.astype(BF16)) + cnt_ref[...]
    rank1 = jnp.sum(oh1 * before, axis=-1, keepdims=True)
    rank2 = jnp.sum(oh2 * before, axis=-1, keepdims=True)
    cnt_ref[...] += jnp.sum(oh, axis=0, keepdims=True)
    cols = (i1 - EXP_LANE0, i2 - EXP_LANE0, w1, w2, rank1, rank2)
    out = jnp.zeros(logits.shape, F32)
    for k, col in enumerate(cols):
        out = jnp.where(lane == k, col, out)
    return out


def _fwd_kernel(x_ref, mod_ref,
                za_in, za_b, za_c, za_in_p, za_in_n, za_c_p, za_c_n,
                zq, zf, zi, zg, zx, zx_p, zx_n, zy, zu, zv, gt0, gt1, gt2, gt3,
                ob_ref, hb_ref, G_ref, lev_ref,
                lb_ref, hgn_ref, ca_ref, cw_ref, cb_ref, wax_ref, bax_ref, lam_ref,
                lng_ref, lnb_ref, sgw_ref, sgb_ref, wbr_ref, wout_ref, n2_ref, wr_ref, br_ref,
                hg0_ref, lru0_ref,
                xo_ref, h2t_ref, route_ref, cnt_ref, hgs_ref, lrus_ref,
                s_scr, hl_scr):
    tt = pl.program_id(0)
    first, last = _tile_flags(tt)

    @pl.when(tt == 0)
    def _():
        cnt_ref[...] = jnp.zeros(cnt_ref.shape, F32)

    @pl.when(first)
    def _():
        for h in range(N_HEADS):
            s_scr[h] = hg0_ref[h].T
        hl_scr[...] = lru0_ref[...]

    has_prev = jnp.where(first, 0.0, 1.0)
    has_next = jnp.where(last, 0.0, 1.0)
    f32 = lambda r: r[...].astype(F32)

    p = f32(za_c) * f32(za_in)
    p_p = f32(za_c_p) * f32(za_in_p) * has_prev
    p_n = f32(za_c_n) * f32(za_in_n) * has_next
    ca = ca_ref[...]
    ya = f32(za_b) * sum(ca[k:k + 1, :] * _shifted(p, p_p, p_n, k - 1) for k in range(3))

    zqv = f32(zq)
    q = zqv * jax.nn.sigmoid(zqv)
    o = _hgrn_tile(q, f32(zf), f32(zi), lb_ref[...], G_ref, lev_ref[...], s_scr, False) + ob_ref[...]
    zgv = f32(zg)
    gate_b = hgn_ref[...] * (zgv * jax.nn.sigmoid(zgv))
    yb = jnp.concatenate(
        [o[:, h * HD:(h + 1) * HD]
         * lax.rsqrt(jnp.mean(jnp.square(o[:, h * HD:(h + 1) * HD]), axis=-1, keepdims=True) + EPS)
         for h in range(N_HEADS)], axis=-1) * gate_b

    hf = _lru_tile(f32(zx), f32(zx_p) * has_prev, f32(zx_n) * has_next, cw_ref[...], cb_ref[...],
                   wax_ref[...], bax_ref[...], lam_ref[...], hl_scr, False)
    yc = (hf + hb_ref[...]) * jax.nn.gelu(f32(zy))

    u = jax.nn.gelu(f32(zu))
    v = jax.nn.gelu(f32(zv))
    mu = jnp.mean(v, axis=-1, keepdims=True)
    vc = v - mu
    var = jnp.mean(vc * vc, axis=-1, keepdims=True)
    vn = (vc * lax.rsqrt(var + EPS) * lng_ref[...] + lnb_ref[...]).astype(BF16)
    sgb = sgb_ref[...]
    gw = W // SG_GROUPS
    chunks = []
    for n in range(TL // SG_CHUNK):
        cols = []
        for g in range(SG_GROUPS):
            blk = vn[n * SG_CHUNK:(n + 1) * SG_CHUNK, g * gw:(g + 1) * gw]
            cols.append(_dot(sgw_ref[g], blk) + sgb[:, g:g + 1])
        chunks.append(jnp.concatenate(cols, axis=-1))
    yd = u * jnp.concatenate(chunks, axis=0)

    mix = jnp.zeros((TL, D), F32)
    for k, (y, gt) in enumerate(((ya, gt0), (yb, gt1), (yc, gt2), (yd, gt3))):
        mix = mix + jax.nn.sigmoid(f32(gt)) * _dot(y.astype(BF16), wbr_ref[k])
    x = x_ref[...] + mod_ref[2:3, :] * _dot(mix.astype(BF16), wout_ref[...])
    xo_ref[...] = x

    h2 = x * lax.rsqrt(jnp.mean(x * x, axis=-1, keepdims=True) + EPS) * n2_ref[...]
    h2 = h2 * (1.0 + mod_ref[4:5, :]) + mod_ref[3:4, :]
    for s in range(ROW_TILES):
        h2t_ref[pl.ds(s, TL, stride=ROW_TILES), :] = h2[:, s * 128:(s + 1) * 128]
    route_ref[...] = _route(h2, wr_ref[...], br_ref[...], cnt_ref)

    for h in range(N_HEADS):
        hgs_ref[h] = s_scr[h].T
    lrus_ref[...] = hl_scr[...]


def _forward_pass(l, x, mod, z, o_b, h_b, G, lev, lb, hg_norm, conv_a, cw, cb, wax, bax, lam,
                  ln_g, ln_b, sg_w, sg_bt, w_br, w_out, norm2, w_r, b_r, hg_init, lru_init):
    tile = lambda s: s
    gate = lambda k: pl.BlockSpec((TL, D), lambda s: (s, GATE_BLK + k))
    in_specs = [
        pl.BlockSpec((TL, D), lambda s: (s, 0)),
        pl.BlockSpec((None, None, 6, D), lambda s: (l, _seq_idx(s), 0, 0)),
        _zslab(S_A_IN, tile), _zslab(S_A_B, tile), _zslab(S_A_C, tile),
        _zprev(S_A_IN, tile), _znext(S_A_IN, tile), _zprev(S_A_C, tile), _znext(S_A_C, tile),
        _zslab(S_B_Q, tile), _zslab(S_B_FF, tile), _zslab(S_B_I, tile), _zslab(S_B_G, tile),
        _zslab(S_C_X, tile), _zprev(S_C_X, tile), _znext(S_C_X, tile), _zslab(S_C_Y, tile),
        _zslab(S_D_U, tile), _zslab(S_D_V, tile), gate(0), gate(1), gate(2), gate(3),
        pl.BlockSpec((TL, W), lambda s: (s, 0)), pl.BlockSpec((TL, W), lambda s: (s, 0)),
        _const(G.shape), _const(lev.shape),
        pl.BlockSpec((None, None, 1, W), lambda s: (0, l, 0, 0)),
        _layer_row(l, W),
        pl.BlockSpec((None, 3, W), lambda s: (l, 0, 0)),
        pl.BlockSpec((None, 4, W), lambda s: (l, 0, 0)),
        _layer_row(l, W),
        pl.BlockSpec((None, None, W, 2 * W), lambda s: (l, 0, 0, 0)),
        pl.BlockSpec((None, None, 1, 2 * W), lambda s: (l, 0, 0, 0)),
        pl.BlockSpec((None, None, 1, W), lambda s: (l, 0, 0, 0)),
        _layer_row(l, W), _layer_row(l, W),
        pl.BlockSpec((None, SG_GROUPS, SG_CHUNK, SG_CHUNK), lambda s: (l, 0, 0, 0)),
        pl.BlockSpec((None, SG_CHUNK, SG_GROUPS), lambda s: (l, 0, 0)),
        pl.BlockSpec((None, 4, W, D), lambda s: (l, 0, 0, 0)),
        pl.BlockSpec((None, D, D), lambda s: (l, 0, 0)),
        _layer_row(l, D),
        pl.BlockSpec((None, D, ROUTE_W), lambda s: (l, 0, 0)),
        _layer_row(l, ROUTE_W),
        pl.BlockSpec((None, None, None, N_HEADS, HD, HD), lambda s: (_seq_idx(s), l, 0, 0, 0, 0)),
        pl.BlockSpec((None, None, None, 1, W), lambda s: (_seq_idx(s), l, 0, 0, 0)),
    ]
    return pl.pallas_call(
        _fwd_kernel,
        grid=(NT,),
        in_specs=in_specs,
        out_specs=[pl.BlockSpec((TL, D), lambda s: (s, 0)),
                   pl.BlockSpec((TL * ROW_TILES, 128), lambda s: (s, 0)),
                   pl.BlockSpec((TL, ROUTE_W), lambda s: (s, 0)),
                   pl.BlockSpec((1, ROUTE_W), lambda s: (0, 0)),
                   pl.BlockSpec((None, N_HEADS, HD, HD), lambda s: (s, 0, 0, 0)),
                   pl.BlockSpec((None, 1, W), lambda s: (s, 0, 0))],
        out_shape=[jax.ShapeDtypeStruct((T, D), F32), jax.ShapeDtypeStruct((T * ROW_TILES, 128), F32),
                   jax.ShapeDtypeStruct((T, ROUTE_W), F32), jax.ShapeDtypeStruct((1, ROUTE_W), F32),
                   jax.ShapeDtypeStruct((NT, N_HEADS, HD, HD), F32),
                   jax.ShapeDtypeStruct((NT, 1, W), F32)],
        scratch_shapes=[pltpu.VMEM((N_HEADS, HD, HD), F32), pltpu.VMEM((1, W), F32)],
        compiler_params=pltpu.CompilerParams(dimension_semantics=("arbitrary",),
                                             vmem_limit_bytes=VMEM_LIMIT),
        name="forward_pass",
    )(x, mod, *([z] * 21), o_b, h_b, G, lev, lb, hg_norm, conv_a, cw, cb, wax, bax, lam,
      ln_g, ln_b, sg_w, sg_bt, w_br, w_out, norm2, w_r, b_r, hg_init, lru_init)


def _token_rows(ref, row, n=1):
    return ref.at[pl.ds(row * ROW_TILES, n * ROW_TILES), :]


def _read_rows(ref, first_row, n):
    return jnp.concatenate([ref[pl.ds(first_row * ROW_TILES + s, n, stride=ROW_TILES), :]
                            for s in range(ROW_TILES)], axis=1)


def _padded(n):
    return ((n + TME - 1) // TME) * TME


def _segment_starts(cnt_ref, seg_ref):
    seg = 0
    for e in range(N_EXPERTS):
        seg_ref[e] = seg
        seg = seg + _padded(cnt_ref[e])
    return seg


def _pair_row(meta_ref, seg_ref, tok, k):
    return seg_ref[meta_ref[4 * tok + k]] + meta_ref[4 * tok + 2 + k]


def _wait_rows(hbm, sem, n):
    pltpu.make_async_copy(_token_rows(hbm, 0, n), _token_rows(hbm, 0, n), sem).wait()


def _dispatch_kernel(meta_ref, cnt_ref, h2t_ref, xs_hbm, zero_buf, seg_ref, sem):
    s = pl.program_id(0)
    total = _segment_starts(cnt_ref, seg_ref)

    def start(i, c):
        for k in range(2):
            row = _pair_row(meta_ref, seg_ref, s * TL + i, k)
            pltpu.make_async_copy(_token_rows(h2t_ref, i), _token_rows(xs_hbm, row), sem).start()
        return c

    lax.fori_loop(0, TL, start, 0, unroll=8)
    _wait_rows(xs_hbm, sem, 2 * TL)

    @pl.when(s == NT - 1)
    def _():
        zero_buf[...] = jnp.zeros(zero_buf.shape, F32)

        def pad_copy(j):
            return pltpu.make_async_copy(_token_rows(zero_buf, 0), _token_rows(xs_hbm, j), sem)

        def pad_start(j, c):
            pad_copy(j).start()
            return c

        def pad_wait(j, c):
            pad_copy(j).wait()
            return c

        def tail_copy(t):
            return pltpu.make_async_copy(zero_buf, _token_rows(xs_hbm, t * TME, TME), sem)

        def tail_start(t, c):
            tail_copy(t).start()
            return c

        def tail_wait(t, c):
            tail_copy(t).wait()
            return c

        for e in range(N_EXPERTS):
            n = cnt_ref[e]
            lax.fori_loop(seg_ref[e] + n, seg_ref[e] + _padded(n), pad_start, 0)
            lax.fori_loop(seg_ref[e] + n, seg_ref[e] + _padded(n), pad_wait, 0)
        lax.fori_loop(total // TME, N_ETILES, tail_start, 0)
        lax.fori_loop(total // TME, N_ETILES, tail_wait, 0)


def _dispatch(meta, cnt, h2t):
    return pl.pallas_call(
        _dispatch_kernel,
        grid_spec=pltpu.PrefetchScalarGridSpec(
            num_scalar_prefetch=2, grid=(NT,),
            in_specs=[pl.BlockSpec((TL * ROW_TILES, 128), lambda s, meta, cnt: (s, 0))],
            out_specs=pl.BlockSpec(memory_space=pl.ANY),
            scratch_shapes=[pltpu.VMEM((TME * ROW_TILES, 128), F32), pltpu.SMEM((N_EXPERTS,), jnp.int32),
                            pltpu.SemaphoreType.DMA]),
        out_shape=jax.ShapeDtypeStruct((N_ROWS * ROW_TILES, 128), F32),
        compiler_params=pltpu.CompilerParams(dimension_semantics=("arbitrary",)),
        name="dispatch",
    )(meta, cnt, h2t)


def _expert_kernel(te_ref, tv_ref, xs_ref, wg_ref, wu_ref, wd_ref, ys_ref, wg_s, wu_s, wd_s):
    i = pl.program_id(0)

    @pl.when(tv_ref[i] == 0)
    def _():
        ys_ref[...] = jnp.zeros(ys_ref.shape, F32)

    @pl.when(tv_ref[i] == 1)
    def _():
        @pl.when(jnp.logical_or(i == 0, te_ref[i] != te_ref[jnp.maximum(i - 1, 0)]))
        def _():
            wg_s[...] = wg_ref[...].astype(BF16)
            wu_s[...] = wu_ref[...].astype(BF16)
            wd_s[...] = wd_ref[...].astype(BF16)

        x = _read_rows(xs_ref, 0, TME).astype(BF16)
        a = _dot(x, wg_s[...])
        u = _dot(x, wu_s[...])
        y = _dot((a * jax.nn.sigmoid(a) * u).astype(BF16), wd_s[...])
        for s in range(ROW_TILES):
            ys_ref[pl.ds(s, TME, stride=ROW_TILES), :] = y[:, s * 128:(s + 1) * 128]


def _grouped_experts(l, te, tv, xs, w_gate, w_up, w_down):
    rows = pl.BlockSpec((TME * ROW_TILES, 128), lambda i, te, tv: (i, 0))
    return pl.pallas_call(
        _expert_kernel,
        grid_spec=pltpu.PrefetchScalarGridSpec(
            num_scalar_prefetch=2, grid=(N_ETILES,),
            in_specs=[rows,
                      pl.BlockSpec((None, None, D, D_EXPERT), lambda i, te, tv: (l, te[i], 0, 0)),
                      pl.BlockSpec((None, None, D, D_EXPERT), lambda i, te, tv: (l, te[i], 0, 0)),
                      pl.BlockSpec((None, None, D_EXPERT, D), lambda i, te, tv: (l, te[i], 0, 0))],
            out_specs=rows,
            scratch_shapes=[pltpu.VMEM((D, D_EXPERT), BF16), pltpu.VMEM((D, D_EXPERT), BF16),
                            pltpu.VMEM((D_EXPERT, D), BF16)]),
        out_shape=jax.ShapeDtypeStruct((N_ROWS * ROW_TILES, 128), F32),
        compiler_params=pltpu.CompilerParams(dimension_semantics=("arbitrary",),
                                             vmem_limit_bytes=VMEM_LIMIT),
        name="grouped_experts",
    )(te, tv, xs, w_gate, w_up, w_down)


def _combine_kernel(meta_ref, cnt_ref, x_ref, mod_ref, route_ref, ys_hbm, o_ref, buf, seg_ref, sem):
    s = pl.program_id(0)
    _segment_starts(cnt_ref, seg_ref)

    def start(i, c):
        for k in range(2):
            row = _pair_row(meta_ref, seg_ref, s * TL + i, k)
            pltpu.make_async_copy(_token_rows(ys_hbm, row), _token_rows(buf, k * TL + i), sem).start()
        return c

    lax.fori_loop(0, TL, start, 0, unroll=8)
    _wait_rows(ys_hbm, sem, 2 * TL)
    route = route_ref[...]
    y = route[:, 2:3] * _read_rows(buf, 0, TL) + route[:, 3:4] * _read_rows(buf, TL, TL)
    o_ref[...] = x_ref[...] + mod_ref[5:6, :] * y


def _combine(l, meta, cnt, x, mod, route, ys):
    return pl.pallas_call(
        _combine_kernel,
        grid_spec=pltpu.PrefetchScalarGridSpec(
            num_scalar_prefetch=2, grid=(NT,),
            in_specs=[pl.BlockSpec((TL, D), lambda s, meta, cnt: (s, 0)),
                      pl.BlockSpec((None, None, 6, D), lambda s, meta, cnt: (l, _seq_idx(s), 0, 0)),
                      pl.BlockSpec((TL, ROUTE_W), lambda s, meta, cnt: (s, 0)),
                      pl.BlockSpec(memory_space=pl.ANY)],
            out_specs=pl.BlockSpec((TL, D), lambda s, meta, cnt: (s, 0)),
            scratch_shapes=[pltpu.VMEM((2 * TL * ROW_TILES, 128), F32), pltpu.SMEM((N_EXPERTS,), jnp.int32),
                            pltpu.SemaphoreType.DMA]),
        out_shape=jax.ShapeDtypeStruct((T, D), F32),
        compiler_params=pltpu.CompilerParams(dimension_semantics=("arbitrary",)),
        name="combine",
    )(meta, cnt, x, mod, route, ys)


def _expert_pass(l, x, mod, h2t, route, cnt, w_gate, w_up, w_down):
    meta = jnp.concatenate([route[:, 0:2], route[:, 4:6]], axis=1).astype(jnp.int32).reshape(-1)
    cnt = cnt[0, EXP_LANE0:EXP_LANE0 + N_EXPERTS].astype(jnp.int32)
    ends = jnp.cumsum((cnt + TME - 1) // TME)
    i = jnp.arange(N_ETILES, dtype=jnp.int32)
    tv = (i < ends[-1]).astype(jnp.int32)
    te = jnp.sum(ends[None, :] <= jnp.minimum(i, ends[-1] - 1)[:, None], axis=1, dtype=jnp.int32)
    xs = _dispatch(meta, cnt, h2t)
    ys = _grouped_experts(l, te, tv, xs, w_gate, w_up, w_down)
    return _combine(l, meta, cnt, x, mod, route, ys)


def _final_kernel(x_ref, g_ref, o_ref):
    x = x_ref[...]
    o_ref[...] = x * lax.rsqrt(jnp.mean(x * x, axis=-1, keepdims=True) + EPS) * g_ref[...]


def _final_norm(x, g):
    tm = 1024
    return pl.pallas_call(
        _final_kernel,
        grid=(T // tm,),
        in_specs=[pl.BlockSpec((tm, D), lambda i: (i, 0)), pl.BlockSpec((1, D), lambda i: (0, 0))],
        out_specs=pl.BlockSpec((tm, D), lambda i: (i, 0)),
        out_shape=jax.ShapeDtypeStruct((T, D), F32),
        compiler_params=pltpu.CompilerParams(dimension_semantics=("arbitrary",)),
        name="final_norm",
    )(x, g.reshape(1, D))


def _grid_pos_embed(rows, d):
    nf = d // 4
    freqs = jnp.exp(-math.log(10000.0) * jnp.arange(nf, dtype=F32) / nf)
    r = jnp.arange(rows, dtype=F32)[:, None] * freqs
    cl = jnp.arange(GRID_W, dtype=F32)[:, None] * freqs
    r_emb = jnp.concatenate([jnp.sin(r), jnp.cos(r)], axis=-1)
    c_emb = jnp.concatenate([jnp.sin(cl), jnp.cos(cl)], axis=-1)
    emb = jnp.concatenate([jnp.broadcast_to(r_emb[:, None], (rows, GRID_W, d // 2)),
                           jnp.broadcast_to(c_emb[None], (rows, GRID_W, d // 2))], axis=-1)
    return emb.reshape(rows * GRID_W, d)


def _block_diag(w):
    n, c = w.shape[-3], w.shape[-1]
    eye = jnp.eye(n, dtype=w.dtype)
    full = w[..., :, :, None, :] * eye[:, None, :, None]
    return full.reshape(*w.shape[:-3], n * c, n * c)


def kernel(x_prompt, x_sample, state_hgrn, state_lru, c, c_ctx, w_ada, b_ada, norm1, norm2, w_in, conv_a,
           hg_lb, hg_norm, lru_conv_w, lru_conv_b, lru_wa, lru_ba, lru_wx, lru_bx, lru_lam, sg_ln_g, sg_ln_b,
           sg_w, sg_b, w_br, w_out, w_rg, b_rg, w_re, b_re, w_gate, w_up, w_down, norm_f):
    assert x_prompt.shape == (N_CTX, L_CTX, D) and x_sample.shape == (N_LAT, L_LAT, D)

    xs = x_sample.astype(F32) + _grid_pos_embed(L_LAT // GRID_W, D)
    x = jnp.concatenate([x_prompt.astype(F32).reshape(-1, D), xs.reshape(-1, D)], axis=0)

    cond = jnp.concatenate([c_ctx.astype(F32)[None], c.astype(F32), jnp.zeros((8 - 1 - N_LAT, D), F32)], axis=0)
    mod = _modulation(cond, w_ada, b_ada).reshape(DEPTH, 8, 6, D)

    cs = jnp.cumsum(jax.nn.softmax(hg_lb.astype(F32), axis=1), axis=1)
    lb = (cs - cs[:, :1]).reshape(2, DEPTH, 1, W)
    wax = jnp.concatenate([_block_diag(lru_wa), _block_diag(lru_wx)], axis=-1).astype(BF16)
    bax = jnp.concatenate([lru_ba, lru_bx], axis=-1).reshape(DEPTH, 2, 1, 2 * W)
    lam = lru_lam.reshape(DEPTH, 2, 1, W)
    row = lambda a: a.reshape(DEPTH, 1, a.shape[-1])
    sg_bt = jnp.swapaxes(sg_b, 1, 2)
    rpad = ROUTE_W - N_GROUPS - N_EXPERTS
    w_r = jnp.concatenate([w_rg, w_re, jnp.zeros((DEPTH, D, rpad), F32)], axis=-1)
    b_r = jnp.concatenate([b_rg, b_re, jnp.zeros((DEPTH, rpad), F32)], axis=-1)
    w_br_b = w_br.astype(BF16)
    w_out_b = w_out.astype(BF16)
    sg_w_b = sg_w.astype(BF16)

    consts = [_scan_constants(rev) for rev in (False, True)]
    G_f, G_b = (jnp.asarray(g, BF16) for g, _ in consts)
    lev_f, lev_b = (jnp.asarray(v) for _, v in consts)

    hg_init = jnp.concatenate([jnp.zeros((1,) + state_hgrn.shape[1:], F32), state_hgrn.astype(F32)], axis=0)
    lru_init = jnp.concatenate([jnp.zeros((1,) + state_lru.shape[1:], F32), state_lru.astype(F32)], axis=0)
    lru_init = lru_init.reshape(1 + N_LAT, DEPTH, 2, 1, W)

    hg_states, lru_states = [], []
    for l in range(DEPTH):
        z = _in_projection(l, x, mod, row(norm1), w_in)
        o_b, h_b, hgs_b, lrus_b = _reverse_pass(l, z, G_b, lev_b, lb, lru_conv_w, row(lru_conv_b), wax, bax, lam,
                                                hg_init, lru_init)
        x, h2t, route, cnt, hgs_f, lrus_f = _forward_pass(
            l, x, mod, z, o_b, h_b, G_f, lev_f, lb, row(hg_norm), conv_a, lru_conv_w, row(lru_conv_b), wax, bax,
            lam, row(sg_ln_g), row(sg_ln_b), sg_w_b, sg_bt, w_br_b, w_out_b, row(norm2), w_r, row(b_r),
            hg_init, lru_init)
        x = _expert_pass(l, x, mod, h2t, route, cnt, w_gate, w_up, w_down)
        hg_states.append(jnp.stack([hgs_f[:CTX_TILES], hgs_b[:CTX_TILES]], axis=1))
        lru_states.append(jnp.stack([lrus_f[:CTX_TILES, 0], lrus_b[:CTX_TILES, 0]], axis=1))

    y = _final_norm(x, norm_f)
    n_ctx = N_CTX * L_CTX
    y_prompt = y[:n_ctx].reshape(N_CTX, L_CTX, D)
    y_sample = y[n_ctx:].reshape(N_LAT, L_LAT, D)
    return y_prompt, y_sample, jnp.stack(hg_states, axis=1), jnp.stack(lru_states, axis=1)
```

```python
import functools
import math

import numpy as np
import jax
import jax.numpy as jnp
from jax import lax
from jax.experimental import pallas as pl
from jax.experimental.pallas import tpu as pltpu

F32 = jnp.float32
BF16 = jnp.bfloat16

D = 1024
W = 512
DEPTH = 4
GRID_W = 64
N_HEADS = 4
HD = 128
LRU_BLOCKS = 8
LRU_BLOCK = W // LRU_BLOCKS
LRU_C = 8.0
SG_CHUNK = 128
SG_GROUPS = 4
N_GROUPS = 4
EXP_PER_GROUP = 4
N_EXPERTS = 16
D_EXPERT = 512
EPS = 1e-6
LOG2E = 1.4426950408889634
D_IN = 12 * W + 4 * D

S_A_IN, S_A_B, S_A_C, S_B_Q, S_B_FF, S_B_FB, S_B_I, S_B_G, S_C_X, S_C_Y, S_D_U, S_D_V = range(12)
GATE_BLK = (12 * W) // D

N_CTX, L_CTX = 16, 256
N_LAT, L_LAT = 2, 2048
T = N_CTX * L_CTX + N_LAT * L_LAT
TL = 256
NT = T // TL
CTX_TILES = (N_CTX * L_CTX) // TL
LAT_TILES = L_LAT // TL
HC = 128
NCH = TL // HC
LOG_HC = 7
N_LEVELS = LOG_HC
N_MXU_LEVELS = 3
SUB = 8
HALO = 16
ROUTE_W = 128
EXP_LANE0 = N_GROUPS

VMEM_LIMIT = 56 * 1024 * 1024

TM_IN, TN_IN = 2048, 1024
TN_MOD = 1536
ROW_TILES = D // 128
TME = 256
PAIRS = EXP_PER_GROUP * (EXP_PER_GROUP - 1) // 2
N_CLASSES = N_GROUPS * PAIRS
N_ETILES = T // TME + N_CLASSES
N_ROWS = N_ETILES * TME
PAIR_LO = [a for a in range(EXP_PER_GROUP) for b in range(a + 1, EXP_PER_GROUP)]
PAIR_HI = [b for a in range(EXP_PER_GROUP) for b in range(a + 1, EXP_PER_GROUP)]


def _scan_constants(reverse):
    i = np.arange(TL)[:, None]
    s = np.arange(TL)[None, :]
    mats = [(i // HC == s // HC) & (s <= i)]
    for B in (8, 4, 2):
        m = (i // B) * B + B // 2 - 1
        mats.append(np.where(i > m, (s > m) & (s <= i), (s > i) & (s <= m)))
    if reverse:
        mats = [g[::-1, ::-1] for g in mats]
    G = np.concatenate(mats, 0).astype(np.float32)
    r = np.arange(HC)[:, None]
    c = np.arange(HC)[None, :]
    lev = (LOG_HC - 1) - np.floor(np.log2(np.maximum(r ^ c, 1))).astype(np.int32)
    lev = np.where(r == c, N_LEVELS, lev)
    causal = (r < c) if reverse else (r > c)
    lev = np.where(causal | (r == c), lev, N_LEVELS + 1).astype(np.int32)
    return G, lev


def _split2(x):
    hi = x.astype(BF16)
    lo = (x - hi.astype(F32)).astype(BF16)
    return hi, lo


def _dot(a, b):
    return jnp.dot(a, b, preferred_element_type=F32)


def _dot_nt(a, b):
    return lax.dot_general(a, b, (((1,), (1,)), ((), ())), preferred_element_type=F32)


def _dot_split(a, b):
    ah, al = _split2(a)
    bh, bl = _split2(b)
    return _dot(ah, bh) + (_dot(ah, bl) + _dot(al, bh))


def _log_sigmoid(x):
    return jnp.minimum(x, 0.0) - jnp.log1p(jnp.exp(-jnp.abs(x)))


def _softplus(x):
    return jnp.maximum(x, 0.0) + jnp.log1p(jnp.exp(-jnp.abs(x)))


def _tile_flags(tt):
    is_lat = tt >= CTX_TILES
    pos = tt & (LAT_TILES - 1)
    first = jnp.logical_or(jnp.logical_not(is_lat), pos == 0)
    last = jnp.logical_or(jnp.logical_not(is_lat), pos == LAT_TILES - 1)
    return first, last


def _seq_idx(t):
    return jnp.where(t < CTX_TILES, 0, 1 + (t - CTX_TILES) // LAT_TILES)


def _shifted(x, prev, nxt, k):
    if k == 0:
        return x
    rows = lax.broadcasted_iota(jnp.int32, (SUB, x.shape[1]), 0)
    if k < 0:
        y = pltpu.roll(x, -k, 0)
        edge = y[:SUB]
        for r in range(-k):
            edge = jnp.where(rows == r, prev[HALO + k + r:HALO + k + r + 1, :], edge)
        return jnp.concatenate([edge, y[SUB:]], axis=0)
    y = pltpu.roll(x, TL - k, 0)
    edge = y[TL - SUB:]
    for r in range(k):
        edge = jnp.where(rows == SUB - k + r, nxt[r:r + 1, :], edge)
    return jnp.concatenate([y[:TL - SUB], edge], axis=0)


def _block_anchor(b, B, a):
    return jnp.concatenate([jnp.broadcast_to(b[s0 + a:s0 + a + 1, :], (B, b.shape[1]))
                            for s0 in range(0, TL, B)], axis=0)


def _hgrn_tile(q, zf, v, lb, G_ref, lev, s_ref, reverse):
    ls = _log_sigmoid(zf)
    bb = jnp.log1p(-lb) + ls
    la = jnp.log(lb)
    lf = jnp.maximum(la, bb) + jnp.log1p(jnp.exp(-jnp.abs(la - bb)))
    kk = (1.0 - lb) * jax.nn.sigmoid(-zf)

    lf = lf * LOG2E
    hi, mid = _split2(lf)
    g = G_ref[...]
    sums = _dot(g, hi) + _dot(g, mid)
    b = sums[0:TL]
    e_lev = []
    row = lax.broadcasted_iota(jnp.int32, (TL, HD), 0)
    for l in range(N_LEVELS - N_MXU_LEVELS):
        B = HC >> l
        anchor = _block_anchor(b, B, B // 2 if reverse else B // 2 - 1)
        later = ((row & (B // 2)) == 0) if reverse else ((row & (B // 2)) != 0)
        sign = jnp.concatenate([jnp.where(later, 1.0, -1.0)] * N_HEADS, axis=1)
        e_lev.append(jnp.exp2((b - anchor) * sign).astype(BF16))
    for l in range(N_MXU_LEVELS):
        e_lev.append(jnp.exp2(sums[(1 + l) * TL:(2 + l) * TL]).astype(BF16))
    end = 0 if reverse else HC - 1
    e_b = jnp.exp2(b)
    e_e = jnp.exp2(_block_anchor(b, HC, end) - b)
    masks = [lev == l for l in range(N_LEVELS)]
    q_b = q.astype(BF16)
    k_b = kk.astype(BF16)

    outs = []
    for h in range(N_HEADS):
        sl = slice(h * HD, (h + 1) * HD)
        st = s_ref[h]
        o_h = [None] * NCH
        for c in (reversed(range(NCH)) if reverse else range(NCH)):
            rs = slice(c * HC, (c + 1) * HC)
            q_c, k_c, v_c = q[rs, sl], kk[rs, sl], v[rs, sl]
            d = jnp.zeros((HC, HC), F32)
            for l in range(N_LEVELS):
                e = e_lev[l][rs, sl]
                p = _dot_nt(q_b[rs, sl] * e, k_b[rs, sl] * e)
                d = jnp.where(masks[l], p, d)
            o = _dot(d.astype(BF16), v_c.astype(BF16))
            o = o + jnp.sum(q_c * k_c, axis=-1, keepdims=True) * v_c
            o = o + _dot_nt((q_c * e_b[rs, sl]).astype(BF16), st.astype(BF16))
            upd = _dot(v_c.T.astype(BF16), (k_c * e_e[rs, sl]).astype(BF16))
            st = e_b[c * HC + end:c * HC + end + 1, sl] * st + upd
            o_h[c] = o
        s_ref[h] = st
        outs.append(jnp.concatenate(o_h, axis=0))
    return jnp.concatenate(outs, axis=-1)


def _lin_scan(a, u, carry, reverse):
    sub = lax.broadcasted_iota(jnp.int32, a.shape, 0) & (SUB - 1)
    d = 1
    while d < SUB:
        if reverse:
            a_s, u_s, ok = pltpu.roll(a, TL - d, 0), pltpu.roll(u, TL - d, 0), sub < SUB - d
        else:
            a_s, u_s, ok = pltpu.roll(a, d, 0), pltpu.roll(u, d, 0), sub >= d
        u = jnp.where(ok, a * u_s + u, u)
        a = jnp.where(ok, a * a_s, a)
        d *= 2
    n = TL // SUB
    blocks = [None] * n
    for g in (reversed(range(n)) if reverse else range(n)):
        blk = u[g * SUB:(g + 1) * SUB] + a[g * SUB:(g + 1) * SUB] * carry
        carry = blk[0:1] if reverse else blk[SUB - 1:SUB]
        blocks[g] = blk
    return jnp.concatenate(blocks, axis=0), carry


def _lru_tile(zx, zx_p, zx_n, cw, cb, wax, bax, lam, hl_ref, reverse):
    xc = cb + sum(cw[k:k + 1, :] * _shifted(zx, zx_p, zx_n, k - 2) for k in range(4))
    rx = _dot(xc.astype(BF16), wax) + bax
    r = jax.nn.sigmoid(rx[:, :W])
    ig = jax.nn.sigmoid(rx[:, W:])
    log_a = -LRU_C * r * _softplus(-lam)
    a = jnp.exp(log_a)
    u = jnp.sqrt(-jnp.tanh(log_a) * (1.0 + a * a)) * (ig * xc)
    hseq, carry = _lin_scan(a, u, hl_ref[...], reverse)
    hl_ref[...] = carry
    return hseq


def _mod_kernel(c_ref, w_ref, b_ref, o_ref):
    c = c_ref[...]
    act = c * jax.nn.sigmoid(c)
    o_ref[...] = _dot_split(act, w_ref[...]) + b_ref[...]


def _modulation(cond, w_ada, b_ada):
    n = 6 * D // TN_MOD
    return pl.pallas_call(
        _mod_kernel,
        grid=(DEPTH, n),
        in_specs=[pl.BlockSpec((8, D), lambda l, j: (0, 0)),
                  pl.BlockSpec((None, D, TN_MOD), lambda l, j: (l, 0, j)),
                  pl.BlockSpec((None, 1, TN_MOD), lambda l, j: (l, 0, j))],
        out_specs=pl.BlockSpec((None, 8, TN_MOD), lambda l, j: (l, 0, j)),
        out_shape=jax.ShapeDtypeStruct((DEPTH, 8, 6 * D), F32),
        compiler_params=pltpu.CompilerParams(dimension_semantics=("arbitrary", "arbitrary"),
                                             vmem_limit_bytes=VMEM_LIMIT),
        name="modulation",
    )(cond, w_ada, b_ada.reshape(DEPTH, 1, 6 * D))


def _in_kernel(x_ref, mod_ref, g_ref, w_ref, z_ref, h_scr):
    @pl.when(pl.program_id(1) == 0)
    def _():
        x = x_ref[...]
        h = x * lax.rsqrt(jnp.mean(x * x, axis=-1, keepdims=True) + EPS) * g_ref[...]
        h = h * (1.0 + mod_ref[1:2, :]) + mod_ref[0:1, :]
        h_scr[...] = h.astype(BF16)

    z_ref[...] = _dot(h_scr[...], w_ref[...].astype(BF16)).astype(BF16)


def _in_projection(l, x, mod, norm1, w_in):
    tiles_per_seq = L_LAT // TM_IN
    ctx = (N_CTX * L_CTX) // TM_IN

    def seq(i):
        return jnp.where(i < ctx, 0, 1 + (i - ctx) // tiles_per_seq)

    return pl.pallas_call(
        _in_kernel,
        grid=(T // TM_IN, D_IN // TN_IN),
        in_specs=[pl.BlockSpec((TM_IN, D), lambda i, j: (i, 0)),
                  pl.BlockSpec((None, None, 6, D), lambda i, j: (l, seq(i), 0, 0)),
                  pl.BlockSpec((None, 1, D), lambda i, j: (l, 0, 0)),
                  pl.BlockSpec((None, D, TN_IN), lambda i, j: (l, 0, j))],
        out_specs=pl.BlockSpec((TM_IN, TN_IN), lambda i, j: (i, j)),
        out_shape=jax.ShapeDtypeStruct((T, D_IN), BF16),
        scratch_shapes=[pltpu.VMEM((TM_IN, D), BF16)],
        compiler_params=pltpu.CompilerParams(dimension_semantics=("arbitrary", "arbitrary"),
                                             vmem_limit_bytes=VMEM_LIMIT),
        name="in_projection",
    )(x, mod, norm1, w_in)


def _bwd_kernel(zq, zf, zi, zx, zx_p, zx_n, G_ref, lev_ref, lb_ref, cw_ref, cb_ref, wax_ref, bax_ref,
                lam_ref, hg0_ref, lru0_ref,
                o_ref, h_ref, hgs_ref, lrus_ref, s_scr, hl_scr):
    tt = NT - 1 - pl.program_id(0)
    first, last = _tile_flags(tt)

    @pl.when(last)
    def _():
        for h in range(N_HEADS):
            s_scr[h] = hg0_ref[h].T
        hl_scr[...] = lru0_ref[...]

    has_prev = jnp.where(first, 0.0, 1.0)
    has_next = jnp.where(last, 0.0, 1.0)
    zqv = zq[...].astype(F32)
    q = zqv * jax.nn.sigmoid(zqv)
    o_ref[...] = _hgrn_tile(q, zf[...].astype(F32), zi[...].astype(F32), lb_ref[...], G_ref, lev_ref[...],
                            s_scr, True)
    h_ref[...] = _lru_tile(zx[...].astype(F32), zx_p[...].astype(F32) * has_prev,
                           zx_n[...].astype(F32) * has_next, cw_ref[...], cb_ref[...], wax_ref[...],
                           bax_ref[...], lam_ref[...], hl_scr, True)
    for h in range(N_HEADS):
        hgs_ref[h] = s_scr[h].T
    lrus_ref[...] = hl_scr[...]


def _zslab(k, tile):
    return pl.BlockSpec((TL, W), lambda s: (tile(s), k))


def _zprev(k, tile):
    return pl.BlockSpec((HALO, W), lambda s: (jnp.maximum(tile(s) * (TL // HALO) - 1, 0), k))


def _znext(k, tile):
    return pl.BlockSpec((HALO, W), lambda s: (jnp.minimum((tile(s) + 1) * (TL // HALO), T // HALO - 1), k))


def _const(shape):
    nd = len(shape)
    return pl.BlockSpec(shape, lambda s: (0,) * nd)


def _layer_row(l, n):
    return pl.BlockSpec((None, 1, n), lambda s: (l, 0, 0))


def _reverse_pass(l, z, G, lev, lb, cw, cb, wax, bax, lam, hg_init, lru_init):
    tile = lambda s: NT - 1 - s
    state_specs = [
        pl.BlockSpec((None, None, None, N_HEADS, HD, HD), lambda s: (_seq_idx(tile(s)), l, 1, 0, 0, 0)),
        pl.BlockSpec((None, None, None, 1, W), lambda s: (_seq_idx(tile(s)), l, 1, 0, 0)),
    ]
    return pl.pallas_call(
        _bwd_kernel,
        grid=(NT,),
        in_specs=[_zslab(S_B_Q, tile), _zslab(S_B_FB, tile), _zslab(S_B_I, tile), _zslab(S_C_X, tile),
                  _zprev(S_C_X, tile), _znext(S_C_X, tile),
                  _const(G.shape), _const(lev.shape),
                  pl.BlockSpec((None, None, 1, W), lambda s: (1, l, 0, 0)),
                  pl.BlockSpec((None, 4, W), lambda s: (l, 0, 0)),
                  _layer_row(l, W),
                  pl.BlockSpec((None, None, W, 2 * W), lambda s: (l, 1, 0, 0)),
                  pl.BlockSpec((None, None, 1, 2 * W), lambda s: (l, 1, 0, 0)),
                  pl.BlockSpec((None, None, 1, W), lambda s: (l, 1, 0, 0)),
                  ] + state_specs,
        out_specs=[pl.BlockSpec((TL, W), lambda s: (tile(s), 0)),
                   pl.BlockSpec((TL, W), lambda s: (tile(s), 0)),
                   pl.BlockSpec((None, N_HEADS, HD, HD), lambda s: (tile(s), 0, 0, 0)),
                   pl.BlockSpec((None, 1, W), lambda s: (tile(s), 0, 0))],
        out_shape=[jax.ShapeDtypeStruct((T, W), F32), jax.ShapeDtypeStruct((T, W), F32),
                   jax.ShapeDtypeStruct((NT, N_HEADS, HD, HD), F32),
                   jax.ShapeDtypeStruct((NT, 1, W), F32)],
        scratch_shapes=[pltpu.VMEM((N_HEADS, HD, HD), F32), pltpu.VMEM((1, W), F32)],
        compiler_params=pltpu.CompilerParams(dimension_semantics=("arbitrary",),
                                             vmem_limit_bytes=VMEM_LIMIT),
        name="reverse_pass",
    )(z, z, z, z, z, z, G, lev, lb, cw, cb, wax, bax, lam, hg_init, lru_init)


def _route(h2, wr, br, cnt_ref):
    logits = _dot_split(h2, wr) + br
    lane = lax.broadcasted_iota(jnp.int32, logits.shape, 1)
    lane_f = lane.astype(F32)
    neg = jnp.float32(-jnp.inf)
    is_g = lane < N_GROUPS
    gl = jnp.where(is_g, logits, neg)
    gmax = jnp.max(gl, axis=-1, keepdims=True)
    g_sel = jnp.min(jnp.where(gl == gmax, lane_f, float(ROUTE_W)), axis=-1, keepdims=True)
    g_p = 1.0 / jnp.sum(jnp.where(is_g, jnp.exp(gl - gmax), 0.0), axis=-1, keepdims=True)
    e_grp = ((lane - EXP_LANE0) >> 2).astype(F32)
    in_grp = (lane >= EXP_LANE0) & (lane < EXP_LANE0 + N_EXPERTS) & (e_grp == g_sel)
    el = jnp.where(in_grp, logits, neg)
    m1 = jnp.max(el, axis=-1, keepdims=True)
    i1 = jnp.min(jnp.where(el == m1, lane_f, float(ROUTE_W)), axis=-1, keepdims=True)
    el2 = jnp.where(lane_f == i1, neg, el)
    m2 = jnp.max(el2, axis=-1, keepdims=True)
    i2 = jnp.min(jnp.where(el2 == m2, lane_f, float(ROUTE_W)), axis=-1, keepdims=True)
    t = jnp.exp(m2 - m1)
    w1 = g_p / (1.0 + t)
    w2 = w1 * t

    first_is_lo = i1 < i2
    lo = jnp.minimum(i1, i2) - EXP_LANE0 - EXP_PER_GROUP * g_sel
    hi = jnp.maximum(i1, i2) - EXP_LANE0 - EXP_PER_GROUP * g_sel
    cls = g_sel * PAIRS + lo * (2 * EXP_PER_GROUP - 1 - lo) * 0.5 + (hi - lo - 1.0)
    oh = (lane_f == cls).astype(F32)
    r = lax.broadcasted_iota(jnp.int32, (TL, TL), 0)
    cc = lax.broadcasted_iota(jnp.int32, (TL, TL), 1)
    before = _dot(jnp.where(r > cc, 1.0, 0.0).astype(BF16), oh.astype(BF16)) + cnt_ref[...]
    rank = jnp.sum(oh * before, axis=-1, keepdims=True)
    cnt_ref[...] += jnp.sum(oh, axis=0, keepdims=True)
    cols = (cls, rank, jnp.where(first_is_lo, w1, w2), jnp.where(first_is_lo, w2, w1))
    out = jnp.zeros(logits.shape, F32)
    for k, col in enumerate(cols):
        out = jnp.where(lane == k, col, out)
    return out


def _fwd_kernel(x_ref, mod_ref,
                za_in, za_b, za_c, za_in_p, za_in_n, za_c_p, za_c_n,
                zq, zf, zi, zg, zx, zx_p, zx_n, zy, zu, zv, gt0, gt1, gt2, gt3,
                ob_ref, hb_ref, G_ref, lev_ref,
                lb_ref, hgn_ref, ca_ref, cw_ref, cb_ref, wax_ref, bax_ref, lam_ref,
                lng_ref, lnb_ref, sgw_ref, sgb_ref, wbr_ref, wout_ref, n2_ref, wr_ref, br_ref,
                hg0_ref, lru0_ref,
                xo_ref, h2t_ref, route_ref, cnt_ref, hgs_ref, lrus_ref,
                s_scr, hl_scr):
    tt = pl.program_id(0)
    first, last = _tile_flags(tt)

    @pl.when(tt == 0)
    def _():
        cnt_ref[...] = jnp.zeros(cnt_ref.shape, F32)

    @pl.when(first)
    def _():
        for h in range(N_HEADS):
            s_scr[h] = hg0_ref[h].T
        hl_scr[...] = lru0_ref[...]

    has_prev = jnp.where(first, 0.0, 1.0)
    has_next = jnp.where(last, 0.0, 1.0)
    f32 = lambda r: r[...].astype(F32)

    p = f32(za_c) * f32(za_in)
    p_p = f32(za_c_p) * f32(za_in_p) * has_prev
    p_n = f32(za_c_n) * f32(za_in_n) * has_next
    ca = ca_ref[...]
    ya = f32(za_b) * sum(ca[k:k + 1, :] * _shifted(p, p_p, p_n, k - 1) for k in range(3))

    zqv = f32(zq)
    q = zqv * jax.nn.sigmoid(zqv)
    o = _hgrn_tile(q, f32(zf), f32(zi), lb_ref[...], G_ref, lev_ref[...], s_scr, False) + ob_ref[...]
    zgv = f32(zg)
    gate_b = hgn_ref[...] * (zgv * jax.nn.sigmoid(zgv))
    yb = jnp.concatenate(
        [o[:, h * HD:(h + 1) * HD]
         * lax.rsqrt(jnp.mean(jnp.square(o[:, h * HD:(h + 1) * HD]), axis=-1, keepdims=True) + EPS)
         for h in range(N_HEADS)], axis=-1) * gate_b

    hf = _lru_tile(f32(zx), f32(zx_p) * has_prev, f32(zx_n) * has_next, cw_ref[...], cb_ref[...],
                   wax_ref[...], bax_ref[...], lam_ref[...], hl_scr, False)
    yc = (hf + hb_ref[...]) * jax.nn.gelu(f32(zy))

    u = jax.nn.gelu(f32(zu))
    v = jax.nn.gelu(f32(zv))
    mu = jnp.mean(v, axis=-1, keepdims=True)
    vc = v - mu
    var = jnp.mean(vc * vc, axis=-1, keepdims=True)
    vn = (vc * lax.rsqrt(var + EPS) * lng_ref[...] + lnb_ref[...]).astype(BF16)
    sgb = sgb_ref[...]
    gw = W // SG_GROUPS
    chunks = []
    for n in range(TL // SG_CHUNK):
        cols = []
        for g in range(SG_GROUPS):
            blk = vn[n * SG_CHUNK:(n + 1) * SG_CHUNK, g * gw:(g + 1) * gw]
            cols.append(_dot(sgw_ref[g], blk) + sgb[:, g:g + 1])
        chunks.append(jnp.concatenate(cols, axis=-1))
    yd = u * jnp.concatenate(chunks, axis=0)

    mix = jnp.zeros((TL, D), F32)
    for k, (y, gt) in enumerate(((ya, gt0), (yb, gt1), (yc, gt2), (yd, gt3))):
        mix = mix + jax.nn.sigmoid(f32(gt)) * _dot(y.astype(BF16), wbr_ref[k])
    x = x_ref[...] + mod_ref[2:3, :] * _dot(mix.astype(BF16), wout_ref[...])
    xo_ref[...] = x

    h2 = x * lax.rsqrt(jnp.mean(x * x, axis=-1, keepdims=True) + EPS) * n2_ref[...]
    h2 = h2 * (1.0 + mod_ref[4:5, :]) + mod_ref[3:4, :]
    for s in range(ROW_TILES):
        h2t_ref[pl.ds(s, TL, stride=ROW_TILES), :] = h2[:, s * 128:(s + 1) * 128]
    route_ref[...] = _route(h2, wr_ref[...], br_ref[...], cnt_ref)

    for h in range(N_HEADS):
        hgs_ref[h] = s_scr[h].T
    lrus_ref[...] = hl_scr[...]


def _forward_pass(l, x, mod, z, o_b, h_b, G, lev, lb, hg_norm, conv_a, cw, cb, wax, bax, lam,
                  ln_g, ln_b, sg_w, sg_bt, w_br, w_out, norm2, w_r, b_r, hg_init, lru_init):
    tile = lambda s: s
    gate = lambda k: pl.BlockSpec((TL, D), lambda s: (s, GATE_BLK + k))
    in_specs = [
        pl.BlockSpec((TL, D), lambda s: (s, 0)),
        pl.BlockSpec((None, None, 6, D), lambda s: (l, _seq_idx(s), 0, 0)),
        _zslab(S_A_IN, tile), _zslab(S_A_B, tile), _zslab(S_A_C, tile),
        _zprev(S_A_IN, tile), _znext(S_A_IN, tile), _zprev(S_A_C, tile), _znext(S_A_C, tile),
        _zslab(S_B_Q, tile), _zslab(S_B_FF, tile), _zslab(S_B_I, tile), _zslab(S_B_G, tile),
        _zslab(S_C_X, tile), _zprev(S_C_X, tile), _znext(S_C_X, tile), _zslab(S_C_Y, tile),
        _zslab(S_D_U, tile), _zslab(S_D_V, tile), gate(0), gate(1), gate(2), gate(3),
        pl.BlockSpec((TL, W), lambda s: (s, 0)), pl.BlockSpec((TL, W), lambda s: (s, 0)),
        _const(G.shape), _const(lev.shape),
        pl.BlockSpec((None, None, 1, W), lambda s: (0, l, 0, 0)),
        _layer_row(l, W),
        pl.BlockSpec((None, 3, W), lambda s: (l, 0, 0)),
        pl.BlockSpec((None, 4, W), lambda s: (l, 0, 0)),
        _layer_row(l, W),
        pl.BlockSpec((None, None, W, 2 * W), lambda s: (l, 0, 0, 0)),
        pl.BlockSpec((None, None, 1, 2 * W), lambda s: (l, 0, 0, 0)),
        pl.BlockSpec((None, None, 1, W), lambda s: (l, 0, 0, 0)),
        _layer_row(l, W), _layer_row(l, W),
        pl.BlockSpec((None, SG_GROUPS, SG_CHUNK, SG_CHUNK), lambda s: (l, 0, 0, 0)),
        pl.BlockSpec((None, SG_CHUNK, SG_GROUPS), lambda s: (l, 0, 0)),
        pl.BlockSpec((None, 4, W, D), lambda s: (l, 0, 0, 0)),
        pl.BlockSpec((None, D, D), lambda s: (l, 0, 0)),
        _layer_row(l, D),
        pl.BlockSpec((None, D, ROUTE_W), lambda s: (l, 0, 0)),
        _layer_row(l, ROUTE_W),
        pl.BlockSpec((None, None, None, N_HEADS, HD, HD), lambda s: (_seq_idx(s), l, 0, 0, 0, 0)),
        pl.BlockSpec((None, None, None, 1, W), lambda s: (_seq_idx(s), l, 0, 0, 0)),
    ]
    return pl.pallas_call(
        _fwd_kernel,
        grid=(NT,),
        in_specs=in_specs,
        out_specs=[pl.BlockSpec((TL, D), lambda s: (s, 0)),
                   pl.BlockSpec((TL * ROW_TILES, 128), lambda s: (s, 0)),
                   pl.BlockSpec((TL, ROUTE_W), lambda s: (s, 0)),
                   pl.BlockSpec((1, ROUTE_W), lambda s: (0, 0)),
                   pl.BlockSpec((None, N_HEADS, HD, HD), lambda s: (s, 0, 0, 0)),
                   pl.BlockSpec((None, 1, W), lambda s: (s, 0, 0))],
        out_shape=[jax.ShapeDtypeStruct((T, D), F32), jax.ShapeDtypeStruct((T * ROW_TILES, 128), F32),
                   jax.ShapeDtypeStruct((T, ROUTE_W), F32), jax.ShapeDtypeStruct((1, ROUTE_W), F32),
                   jax.ShapeDtypeStruct((NT, N_HEADS, HD, HD), F32),
                   jax.ShapeDtypeStruct((NT, 1, W), F32)],
        scratch_shapes=[pltpu.VMEM((N_HEADS, HD, HD), F32), pltpu.VMEM((1, W), F32)],
        compiler_params=pltpu.CompilerParams(dimension_semantics=("arbitrary",),
                                             vmem_limit_bytes=VMEM_LIMIT),
        name="forward_pass",
    )(x, mod, *([z] * 21), o_b, h_b, G, lev, lb, hg_norm, conv_a, cw, cb, wax, bax, lam,
      ln_g, ln_b, sg_w, sg_bt, w_br, w_out, norm2, w_r, b_r, hg_init, lru_init)


def _token_rows(ref, row, n=1, tiles=ROW_TILES):
    return ref.at[pl.ds(row * tiles, n * tiles), :]


def _read_rows(ref, n, tiles=ROW_TILES, first=0):
    return jnp.concatenate([ref[pl.ds(first + s, n, stride=tiles), :] for s in range(ROW_TILES)], axis=1)


def _write_rows(ref, y, tiles=ROW_TILES, first=0):
    for s in range(ROW_TILES):
        ref[pl.ds(first + s, y.shape[0], stride=tiles), :] = y[:, s * 128:(s + 1) * 128]


def _padded(n):
    return ((n + TME - 1) // TME) * TME


def _segment_starts(cnt_ref, seg_ref):
    seg = 0
    for c in range(N_CLASSES):
        seg_ref[c] = seg
        seg = seg + _padded(cnt_ref[c])
    return seg


def _token_slot(meta_ref, seg_ref, tok):
    return seg_ref[meta_ref[2 * tok]] + meta_ref[2 * tok + 1]


def _wait_rows(hbm, sem, n, tiles=ROW_TILES):
    pltpu.make_async_copy(_token_rows(hbm, 0, n, tiles), _token_rows(hbm, 0, n, tiles), sem).wait()


def _dispatch_kernel(meta_ref, cnt_ref, h2t_ref, xs_hbm, zero_buf, seg_ref, sem):
    s = pl.program_id(0)
    total = _segment_starts(cnt_ref, seg_ref)

    def start(i, c):
        row = _token_slot(meta_ref, seg_ref, s * TL + i)
        pltpu.make_async_copy(_token_rows(h2t_ref, i), _token_rows(xs_hbm, row), sem).start()
        return c

    lax.fori_loop(0, TL, start, 0, unroll=8)
    _wait_rows(xs_hbm, sem, TL)

    @pl.when(s == NT - 1)
    def _():
        zero_buf[...] = jnp.zeros(zero_buf.shape, F32)

        def pad_copy(j):
            return pltpu.make_async_copy(_token_rows(zero_buf, 0), _token_rows(xs_hbm, j), sem)

        def pad_start(j, c):
            pad_copy(j).start()
            return c

        def pad_wait(j, c):
            pad_copy(j).wait()
            return c

        def tail_copy(t):
            return pltpu.make_async_copy(zero_buf, _token_rows(xs_hbm, t * TME, TME), sem)

        def tail_start(t, c):
            tail_copy(t).start()
            return c

        def tail_wait(t, c):
            tail_copy(t).wait()
            return c

        for c in range(N_CLASSES):
            n = cnt_ref[c]
            lax.fori_loop(seg_ref[c] + n, seg_ref[c] + _padded(n), pad_start, 0)
            lax.fori_loop(seg_ref[c] + n, seg_ref[c] + _padded(n), pad_wait, 0)
        lax.fori_loop(total // TME, N_ETILES, tail_start, 0)
        lax.fori_loop(total // TME, N_ETILES, tail_wait, 0)


def _dispatch(meta, cnt, h2t):
    return pl.pallas_call(
        _dispatch_kernel,
        grid_spec=pltpu.PrefetchScalarGridSpec(
            num_scalar_prefetch=2, grid=(NT,),
            in_specs=[pl.BlockSpec((TL * ROW_TILES, 128), lambda s, meta, cnt: (s, 0))],
            out_specs=pl.BlockSpec(memory_space=pl.ANY),
            scratch_shapes=[pltpu.VMEM((TME * ROW_TILES, 128), F32), pltpu.SMEM((N_CLASSES,), jnp.int32),
                            pltpu.SemaphoreType.DMA]),
        out_shape=jax.ShapeDtypeStruct((N_ROWS * ROW_TILES, 128), F32),
        compiler_params=pltpu.CompilerParams(dimension_semantics=("arbitrary",)),
        name="dispatch",
    )(meta, cnt, h2t)


def _expert_kernel(lo_ref, hi_ref, tv_ref, xs_ref, wg0, wu0, wd0, wg1, wu1, wd1, ys_ref,
                   sg0, su0, sd0, sg1, su1, sd1):
    i = pl.program_id(0)
    prev = jnp.maximum(i - 1, 0)

    @pl.when(tv_ref[i] == 0)
    def _():
        ys_ref[...] = jnp.zeros(ys_ref.shape, F32)

    @pl.when(tv_ref[i] == 1)
    def _():
        sets = ((lo_ref, (wg0, wu0, wd0), (sg0, su0, sd0)), (hi_ref, (wg1, wu1, wd1), (sg1, su1, sd1)))
        for e_ref, w_refs, s_refs in sets:
            @pl.when(jnp.logical_or(i == 0, e_ref[i] != e_ref[prev]))
            def _():
                for w, sc in zip(w_refs, s_refs):
                    sc[...] = w[...].astype(BF16)

        x = _read_rows(xs_ref, TME).astype(BF16)
        for k, (_, _, (sg, su, sd)) in enumerate(sets):
            a = _dot(x, sg[...])
            u = _dot(x, su[...])
            y = _dot((a * jax.nn.sigmoid(a) * u).astype(BF16), sd[...])
            _write_rows(ys_ref, y, 2 * ROW_TILES, k * ROW_TILES)


def _grouped_experts(l, e_lo, e_hi, tv, xs, w_gate, w_up, w_down):
    def wspec(shape, which):
        return pl.BlockSpec((None, None) + shape, lambda i, lo, hi, tv: (l, (lo, hi)[which][i], 0, 0))

    wspecs = [wspec(shape, which) for which in range(2)
              for shape in ((D, D_EXPERT), (D, D_EXPERT), (D_EXPERT, D))]
    return pl.pallas_call(
        _expert_kernel,
        grid_spec=pltpu.PrefetchScalarGridSpec(
            num_scalar_prefetch=3, grid=(N_ETILES,),
            in_specs=[pl.BlockSpec((TME * ROW_TILES, 128), lambda i, lo, hi, tv: (i, 0))] + wspecs,
            out_specs=pl.BlockSpec((TME * 2 * ROW_TILES, 128), lambda i, lo, hi, tv: (i, 0)),
            scratch_shapes=[pltpu.VMEM(shape, BF16) for _ in range(2)
                            for shape in ((D, D_EXPERT), (D, D_EXPERT), (D_EXPERT, D))]),
        out_shape=jax.ShapeDtypeStruct((N_ROWS * 2 * ROW_TILES, 128), F32),
        compiler_params=pltpu.CompilerParams(dimension_semantics=("arbitrary",),
                                             vmem_limit_bytes=VMEM_LIMIT),
        name="grouped_experts",
    )(e_lo, e_hi, tv, xs, w_gate, w_up, w_down, w_gate, w_up, w_down)


def _combine_kernel(meta_ref, cnt_ref, x_ref, mod_ref, route_ref, ys_hbm, o_ref, buf, seg_ref, sem):
    s = pl.program_id(0)
    _segment_starts(cnt_ref, seg_ref)

    def start(i, c):
        row = _token_slot(meta_ref, seg_ref, s * TL + i)
        pltpu.make_async_copy(_token_rows(ys_hbm, row, 1, 2 * ROW_TILES), _token_rows(buf, i, 1, 2 * ROW_TILES),
                              sem).start()
        return c

    lax.fori_loop(0, TL, start, 0, unroll=8)
    _wait_rows(ys_hbm, sem, TL, 2 * ROW_TILES)
    route = route_ref[...]
    y = (route[:, 2:3] * _read_rows(buf, TL, 2 * ROW_TILES, 0)
         + route[:, 3:4] * _read_rows(buf, TL, 2 * ROW_TILES, ROW_TILES))
    o_ref[...] = x_ref[...] + mod_ref[5:6, :] * y


def _combine(l, meta, cnt, x, mod, route, ys):
    return pl.pallas_call(
        _combine_kernel,
        grid_spec=pltpu.PrefetchScalarGridSpec(
            num_scalar_prefetch=2, grid=(NT,),
            in_specs=[pl.BlockSpec((TL, D), lambda s, meta, cnt: (s, 0)),
                      pl.BlockSpec((None, None, 6, D), lambda s, meta, cnt: (l, _seq_idx(s), 0, 0)),
                      pl.BlockSpec((TL, ROUTE_W), lambda s, meta, cnt: (s, 0)),
                      pl.BlockSpec(memory_space=pl.ANY)],
            out_specs=pl.BlockSpec((TL, D), lambda s, meta, cnt: (s, 0)),
            scratch_shapes=[pltpu.VMEM((TL * 2 * ROW_TILES, 128), F32), pltpu.SMEM((N_CLASSES,), jnp.int32),
                            pltpu.SemaphoreType.DMA]),
        out_shape=jax.ShapeDtypeStruct((T, D), F32),
        compiler_params=pltpu.CompilerParams(dimension_semantics=("arbitrary",)),
        name="combine",
    )(meta, cnt, x, mod, route, ys)


def _expert_pass(l, x, mod, h2t, route, cnt, w_gate, w_up, w_down):
    meta = route[:, 0:2].astype(jnp.int32).reshape(-1)
    cnt = cnt[0, :N_CLASSES].astype(jnp.int32)
    ends = jnp.cumsum((cnt + TME - 1) // TME)
    i = jnp.arange(N_ETILES, dtype=jnp.int32)
    tv = (i < ends[-1]).astype(jnp.int32)
    tc = jnp.sum(ends[None, :] <= jnp.minimum(i, ends[-1] - 1)[:, None], axis=1, dtype=jnp.int32)
    grp, pair = tc // PAIRS, tc % PAIRS
    e_lo = grp * EXP_PER_GROUP + jnp.asarray(PAIR_LO, jnp.int32)[pair]
    e_hi = grp * EXP_PER_GROUP + jnp.asarray(PAIR_HI, jnp.int32)[pair]
    xs = _dispatch(meta, cnt, h2t)
    ys = _grouped_experts(l, e_lo, e_hi, tv, xs, w_gate, w_up, w_down)
    return _combine(l, meta, cnt, x, mod, route, ys)


def _final_kernel(x_ref, g_ref, o_ref):
    x = x_ref[...]
    o_ref[...] = x * lax.rsqrt(jnp.mean(x * x, axis=-1, keepdims=True) + EPS) * g_ref[...]


def _final_norm(x, g, first_row, rows):
    tm = 1024
    return pl.pallas_call(
        _final_kernel,
        grid=(rows // tm,),
        in_specs=[pl.BlockSpec((tm, D), lambda i: (first_row // tm + i, 0)), pl.BlockSpec((1, D), lambda i: (0, 0))],
        out_specs=pl.BlockSpec((tm, D), lambda i: (i, 0)),
        out_shape=jax.ShapeDtypeStruct((rows, D), F32),
        compiler_params=pltpu.CompilerParams(dimension_semantics=("arbitrary",)),
        name="final_norm",
    )(x, g.reshape(1, D))


def _grid_pos_embed(rows, d):
    nf = d // 4
    freqs = jnp.exp(-math.log(10000.0) * jnp.arange(nf, dtype=F32) / nf)
    r = jnp.arange(rows, dtype=F32)[:, None] * freqs
    cl = jnp.arange(GRID_W, dtype=F32)[:, None] * freqs
    r_emb = jnp.concatenate([jnp.sin(r), jnp.cos(r)], axis=-1)
    c_emb = jnp.concatenate([jnp.sin(cl), jnp.cos(cl)], axis=-1)
    emb = jnp.concatenate([jnp.broadcast_to(r_emb[:, None], (rows, GRID_W, d // 2)),
                           jnp.broadcast_to(c_emb[None], (rows, GRID_W, d // 2))], axis=-1)
    return emb.reshape(rows * GRID_W, d)


def _block_diag(w):
    n, c = w.shape[-3], w.shape[-1]
    eye = jnp.eye(n, dtype=w.dtype)
    full = w[..., :, :, None, :] * eye[:, None, :, None]
    return full.reshape(*w.shape[:-3], n * c, n * c)


def kernel(x_prompt, x_sample, state_hgrn, state_lru, c, c_ctx, w_ada, b_ada, norm1, norm2, w_in, conv_a,
           hg_lb, hg_norm, lru_conv_w, lru_conv_b, lru_wa, lru_ba, lru_wx, lru_bx, lru_lam, sg_ln_g, sg_ln_b,
           sg_w, sg_b, w_br, w_out, w_rg, b_rg, w_re, b_re, w_gate, w_up, w_down, norm_f):
    assert x_prompt.shape == (N_CTX, L_CTX, D) and x_sample.shape == (N_LAT, L_LAT, D)

    xs = x_sample.astype(F32) + _grid_pos_embed(L_LAT // GRID_W, D)
    x = jnp.concatenate([x_prompt.astype(F32).reshape(-1, D), xs.reshape(-1, D)], axis=0)

    cond = jnp.concatenate([c_ctx.astype(F32)[None], c.astype(F32), jnp.zeros((8 - 1 - N_LAT, D), F32)], axis=0)
    mod = _modulation(cond, w_ada, b_ada).reshape(DEPTH, 8, 6, D)

    cs = jnp.cumsum(jax.nn.softmax(hg_lb.astype(F32), axis=1), axis=1)
    lb = (cs - cs[:, :1]).reshape(2, DEPTH, 1, W)
    wax = jnp.concatenate([_block_diag(lru_wa), _block_diag(lru_wx)], axis=-1).astype(BF16)
    bax = jnp.concatenate([lru_ba, lru_bx], axis=-1).reshape(DEPTH, 2, 1, 2 * W)
    lam = lru_lam.reshape(DEPTH, 2, 1, W)
    row = lambda a: a.reshape(DEPTH, 1, a.shape[-1])
    sg_bt = jnp.swapaxes(sg_b, 1, 2)
    rpad = ROUTE_W - N_GROUPS - N_EXPERTS
    w_r = jnp.concatenate([w_rg, w_re, jnp.zeros((DEPTH, D, rpad), F32)], axis=-1)
    b_r = jnp.concatenate([b_rg, b_re, jnp.zeros((DEPTH, rpad), F32)], axis=-1)
    w_br_b = w_br.astype(BF16)
    w_out_b = w_out.astype(BF16)
    sg_w_b = sg_w.astype(BF16)

    consts = [_scan_constants(rev) for rev in (False, True)]
    G_f, G_b = (jnp.asarray(g, BF16) for g, _ in consts)
    lev_f, lev_b = (jnp.asarray(v) for _, v in consts)

    hg_init = jnp.concatenate([jnp.zeros((1,) + state_hgrn.shape[1:], F32), state_hgrn.astype(F32)], axis=0)
    lru_init = jnp.concatenate([jnp.zeros((1,) + state_lru.shape[1:], F32), state_lru.astype(F32)], axis=0)
    lru_init = lru_init.reshape(1 + N_LAT, DEPTH, 2, 1, W)

    hg_states, lru_states = [], []
    for l in range(DEPTH):
        z = _in_projection(l, x, mod, row(norm1), w_in)
        o_b, h_b, hgs_b, lrus_b = _reverse_pass(l, z, G_b, lev_b, lb, lru_conv_w, row(lru_conv_b), wax, bax, lam,
                                                hg_init, lru_init)
        x, h2t, route, cnt, hgs_f, lrus_f = _forward_pass(
            l, x, mod, z, o_b, h_b, G_f, lev_f, lb, row(hg_norm), conv_a, lru_conv_w, row(lru_conv_b), wax, bax,
            lam, row(sg_ln_g), row(sg_ln_b), sg_w_b, sg_bt, w_br_b, w_out_b, row(norm2), w_r, row(b_r),
            hg_init, lru_init)
        x = _expert_pass(l, x, mod, h2t, route, cnt, w_gate, w_up, w_down)
        hg_states.append(jnp.stack([hgs_f[:CTX_TILES], hgs_b[:CTX_TILES]], axis=1))
        lru_states.append(jnp.stack([lrus_f[:CTX_TILES, 0], lrus_b[:CTX_TILES, 0]], axis=1))

    n_ctx = N_CTX * L_CTX
    y_prompt = _final_norm(x, norm_f, 0, n_ctx).reshape(N_CTX, L_CTX, D)
    y_sample = _final_norm(x, norm_f, n_ctx, T - n_ctx).reshape(N_LAT, L_LAT, D)
    return y_prompt, y_sample, jnp.stack(hg_states, axis=1), jnp.stack(lru_states, axis=1)
```

```python
import functools
import math

import numpy as np
import jax
import jax.numpy as jnp
from jax import lax
from jax.experimental import pallas as pl
from jax.experimental.pallas import tpu as pltpu

F32 = jnp.float32
BF16 = jnp.bfloat16

D = 1024
W = 512
DEPTH = 4
GRID_W = 64
N_HEADS = 4
HD = 128
LRU_BLOCKS = 8
LRU_BLOCK = W // LRU_BLOCKS
LRU_C = 8.0
SG_CHUNK = 128
SG_GROUPS = 4
N_GROUPS = 4
EXP_PER_GROUP = 4
N_EXPERTS = 16
D_EXPERT = 512
EPS = 1e-6
LOG2E = 1.4426950408889634
D_IN = 12 * W + 4 * D

S_A_IN, S_A_B, S_A_C, S_B_Q, S_B_FF, S_B_FB, S_B_I, S_B_G, S_C_X, S_C_Y, S_D_U, S_D_V = range(12)
GATE_BLK = (12 * W) // D

N_CTX, L_CTX = 16, 256
N_LAT, L_LAT = 2, 2048
T = N_CTX * L_CTX + N_LAT * L_LAT
TL = 256
NT = T // TL
CTX_TILES = (N_CTX * L_CTX) // TL
LAT_TILES = L_LAT // TL
HC = 128
NCH = TL // HC
LOG_HC = 7
N_LEVELS = LOG_HC
N_MXU_LEVELS = 3
SUB = 8
HALO = 16
ROUTE_W = 128
EXP_LANE0 = N_GROUPS

VMEM_LIMIT = 56 * 1024 * 1024

TM_IN, TN_IN = 2048, 1024
TN_MOD = 1536
ROW_TILES = D // 128
TME = 256
PAIRS = EXP_PER_GROUP * (EXP_PER_GROUP - 1) // 2
N_CLASSES = N_GROUPS * PAIRS
N_ETILES = T // TME + N_CLASSES
N_ROWS = N_ETILES * TME
PAIR_LO = [a for a in range(EXP_PER_GROUP) for b in range(a + 1, EXP_PER_GROUP)]
PAIR_HI = [b for a in range(EXP_PER_GROUP) for b in range(a + 1, EXP_PER_GROUP)]


def _scan_constants(reverse):
    i = np.arange(TL)[:, None]
    s = np.arange(TL)[None, :]
    mats = [(i // HC == s // HC) & (s <= i)]
    for B in (8, 4, 2):
        m = (i // B) * B + B // 2 - 1
        mats.append(np.where(i > m, (s > m) & (s <= i), (s > i) & (s <= m)))
    if reverse:
        mats = [g[::-1, ::-1] for g in mats]
    G = np.concatenate(mats, 0).astype(np.float32)
    r = np.arange(HC)[:, None]
    c = np.arange(HC)[None, :]
    lev = (LOG_HC - 1) - np.floor(np.log2(np.maximum(r ^ c, 1))).astype(np.int32)
    lev = np.where(r == c, N_LEVELS, lev)
    causal = (r < c) if reverse else (r > c)
    lev = np.where(causal | (r == c), lev, N_LEVELS + 1).astype(np.int32)
    return G, lev


def _split2(x):
    hi = x.astype(BF16)
    lo = (x - hi.astype(F32)).astype(BF16)
    return hi, lo


def _dot(a, b):
    return jnp.dot(a, b, preferred_element_type=F32)


def _dot_nt(a, b):
    return lax.dot_general(a, b, (((1,), (1,)), ((), ())), preferred_element_type=F32)


def _dot_split(a, b):
    ah, al = _split2(a)
    bh, bl = _split2(b)
    return _dot(ah, bh) + (_dot(ah, bl) + _dot(al, bh))


def _log_sigmoid(x):
    return jnp.minimum(x, 0.0) - jnp.log1p(jnp.exp(-jnp.abs(x)))


def _softplus(x):
    return jnp.maximum(x, 0.0) + jnp.log1p(jnp.exp(-jnp.abs(x)))


def _tile_flags(tt):
    is_lat = tt >= CTX_TILES
    pos = tt & (LAT_TILES - 1)
    first = jnp.logical_or(jnp.logical_not(is_lat), pos == 0)
    last = jnp.logical_or(jnp.logical_not(is_lat), pos == LAT_TILES - 1)
    return first, last


def _seq_idx(t):
    return jnp.where(t < CTX_TILES, 0, 1 + (t - CTX_TILES) // LAT_TILES)


def _shifted(x, prev, nxt, k):
    if k == 0:
        return x
    rows = lax.broadcasted_iota(jnp.int32, (SUB, x.shape[1]), 0)
    if k < 0:
        y = pltpu.roll(x, -k, 0)
        edge = y[:SUB]
        for r in range(-k):
            edge = jnp.where(rows == r, prev[HALO + k + r:HALO + k + r + 1, :], edge)
        return jnp.concatenate([edge, y[SUB:]], axis=0)
    y = pltpu.roll(x, TL - k, 0)
    edge = y[TL - SUB:]
    for r in range(k):
        edge = jnp.where(rows == SUB - k + r, nxt[r:r + 1, :], edge)
    return jnp.concatenate([y[:TL - SUB], edge], axis=0)


def _block_anchor(b, B, a):
    return jnp.concatenate([jnp.broadcast_to(b[s0 + a:s0 + a + 1, :], (B, b.shape[1]))
                            for s0 in range(0, TL, B)], axis=0)


def _hgrn_tile(q, zf, v, lb, G_ref, lev, s_ref, reverse):
    ls = _log_sigmoid(zf)
    bb = jnp.log1p(-lb) + ls
    la = jnp.log(lb)
    lf = jnp.maximum(la, bb) + jnp.log1p(jnp.exp(-jnp.abs(la - bb)))
    kk = (1.0 - lb) * jax.nn.sigmoid(-zf)

    lf = lf * LOG2E
    hi, mid = _split2(lf)
    g = G_ref[...]
    sums = _dot(g, hi) + _dot(g, mid)
    b = sums[0:TL]
    e_lev = []
    row = lax.broadcasted_iota(jnp.int32, (TL, HD), 0)
    for l in range(N_LEVELS - N_MXU_LEVELS):
        B = HC >> l
        anchor = _block_anchor(b, B, B // 2 if reverse else B // 2 - 1)
        later = ((row & (B // 2)) == 0) if reverse else ((row & (B // 2)) != 0)
        sign = jnp.concatenate([jnp.where(later, 1.0, -1.0)] * N_HEADS, axis=1)
        e_lev.append(jnp.exp2((b - anchor) * sign).astype(BF16))
    for l in range(N_MXU_LEVELS):
        e_lev.append(jnp.exp2(sums[(1 + l) * TL:(2 + l) * TL]).astype(BF16))
    end = 0 if reverse else HC - 1
    e_b = jnp.exp2(b)
    e_e = jnp.exp2(_block_anchor(b, HC, end) - b)
    masks = [lev == l for l in range(N_LEVELS)]
    q_b = q.astype(BF16)
    k_b = kk.astype(BF16)

    outs = []
    for h in range(N_HEADS):
        sl = slice(h * HD, (h + 1) * HD)
        st = s_ref[h]
        o_h = [None] * NCH
        for c in (reversed(range(NCH)) if reverse else range(NCH)):
            rs = slice(c * HC, (c + 1) * HC)
            q_c, k_c, v_c = q[rs, sl], kk[rs, sl], v[rs, sl]
            d = jnp.zeros((HC, HC), F32)
            for l in range(N_LEVELS):
                e = e_lev[l][rs, sl]
                p = _dot_nt(q_b[rs, sl] * e, k_b[rs, sl] * e)
                d = jnp.where(masks[l], p, d)
            o = _dot(d.astype(BF16), v_c.astype(BF16))
            o = o + jnp.sum(q_c * k_c, axis=-1, keepdims=True) * v_c
            o = o + _dot_nt((q_c * e_b[rs, sl]).astype(BF16), st.astype(BF16))
            upd = _dot(v_c.T.astype(BF16), (k_c * e_e[rs, sl]).astype(BF16))
            st = e_b[c * HC + end:c * HC + end + 1, sl] * st + upd
            o_h[c] = o
        s_ref[h] = st
        outs.append(jnp.concatenate(o_h, axis=0))
    return jnp.concatenate(outs, axis=-1)


def _lin_scan(a, u, carry, reverse):
    sub = lax.broadcasted_iota(jnp.int32, a.shape, 0) & (SUB - 1)
    d = 1
    while d < SUB:
        if reverse:
            a_s, u_s, ok = pltpu.roll(a, TL - d, 0), pltpu.roll(u, TL - d, 0), sub < SUB - d
        else:
            a_s, u_s, ok = pltpu.roll(a, d, 0), pltpu.roll(u, d, 0), sub >= d
        u = jnp.where(ok, a * u_s + u, u)
        a = jnp.where(ok, a * a_s, a)
        d *= 2
    n = TL // SUB
    blocks = [None] * n
    for g in (reversed(range(n)) if reverse else range(n)):
        blk = u[g * SUB:(g + 1) * SUB] + a[g * SUB:(g + 1) * SUB] * carry
        carry = blk[0:1] if reverse else blk[SUB - 1:SUB]
        blocks[g] = blk
    return jnp.concatenate(blocks, axis=0), carry


def _lru_tile(zx, zx_p, zx_n, cw, cb, wax, bax, lam, hl_ref, reverse):
    xc = cb + sum(cw[k:k + 1, :] * _shifted(zx, zx_p, zx_n, k - 2) for k in range(4))
    rx = _dot(xc.astype(BF16), wax) + bax
    r = jax.nn.sigmoid(rx[:, :W])
    ig = jax.nn.sigmoid(rx[:, W:])
    log_a = -LRU_C * r * _softplus(-lam)
    a = jnp.exp(log_a)
    u = jnp.sqrt(-jnp.tanh(log_a) * (1.0 + a * a)) * (ig * xc)
    hseq, carry = _lin_scan(a, u, hl_ref[...], reverse)
    hl_ref[...] = carry
    return hseq


def _mod_kernel(c_ref, w_ref, b_ref, o_ref):
    c = c_ref[...]
    act = c * jax.nn.sigmoid(c)
    o_ref[...] = _dot_split(act, w_ref[...]) + b_ref[...]


def _modulation(cond, w_ada, b_ada):
    n = 6 * D // TN_MOD
    return pl.pallas_call(
        _mod_kernel,
        grid=(DEPTH, n),
        in_specs=[pl.BlockSpec((8, D), lambda l, j: (0, 0)),
                  pl.BlockSpec((None, D, TN_MOD), lambda l, j: (l, 0, j)),
                  pl.BlockSpec((None, 1, TN_MOD), lambda l, j: (l, 0, j))],
        out_specs=pl.BlockSpec((None, 8, TN_MOD), lambda l, j: (l, 0, j)),
        out_shape=jax.ShapeDtypeStruct((DEPTH, 8, 6 * D), F32),
        compiler_params=pltpu.CompilerParams(dimension_semantics=("arbitrary", "arbitrary"),
                                             vmem_limit_bytes=VMEM_LIMIT),
        name="modulation",
    )(cond, w_ada, b_ada.reshape(DEPTH, 1, 6 * D))


def _in_kernel(x_ref, mod_ref, g_ref, w_ref, z_ref, h_scr):
    @pl.when(pl.program_id(1) == 0)
    def _():
        x = x_ref[...]
        h = x * lax.rsqrt(jnp.mean(x * x, axis=-1, keepdims=True) + EPS) * g_ref[...]
        h = h * (1.0 + mod_ref[1:2, :]) + mod_ref[0:1, :]
        h_scr[...] = h.astype(BF16)

    z_ref[...] = _dot(h_scr[...], w_ref[...].astype(BF16)).astype(BF16)


def _in_projection(l, x, mod, norm1, w_in):
    tiles_per_seq = L_LAT // TM_IN
    ctx = (N_CTX * L_CTX) // TM_IN

    def seq(i):
        return jnp.where(i < ctx, 0, 1 + (i - ctx) // tiles_per_seq)

    return pl.pallas_call(
        _in_kernel,
        grid=(T // TM_IN, D_IN // TN_IN),
        in_specs=[pl.BlockSpec((TM_IN, D), lambda i, j: (i, 0)),
                  pl.BlockSpec((None, None, 6, D), lambda i, j: (l, seq(i), 0, 0)),
                  pl.BlockSpec((None, 1, D), lambda i, j: (l, 0, 0)),
                  pl.BlockSpec((None, D, TN_IN), lambda i, j: (l, 0, j))],
        out_specs=pl.BlockSpec((TM_IN, TN_IN), lambda i, j: (i, j)),
        out_shape=jax.ShapeDtypeStruct((T, D_IN), BF16),
        scratch_shapes=[pltpu.VMEM((TM_IN, D), BF16)],
        compiler_params=pltpu.CompilerParams(dimension_semantics=("arbitrary", "arbitrary"),
                                             vmem_limit_bytes=VMEM_LIMIT),
        name="in_projection",
    )(x, mod, norm1, w_in)


def _bwd_kernel(zq, zf, zi, zx, zx_p, zx_n, G_ref, lev_ref, lb_ref, cw_ref, cb_ref, wax_ref, bax_ref,
                lam_ref, hg0_ref, lru0_ref,
                o_ref, h_ref, hgs_ref, lrus_ref, s_scr, hl_scr):
    tt = NT - 1 - pl.program_id(0)
    first, last = _tile_flags(tt)

    @pl.when(last)
    def _():
        for h in range(N_HEADS):
            s_scr[h] = hg0_ref[h].T
        hl_scr[...] = lru0_ref[...]

    has_prev = jnp.where(first, 0.0, 1.0)
    has_next = jnp.where(last, 0.0, 1.0)
    zqv = zq[...].astype(F32)
    q = zqv * jax.nn.sigmoid(zqv)
    o_ref[...] = _hgrn_tile(q, zf[...].astype(F32), zi[...].astype(F32), lb_ref[...], G_ref, lev_ref[...],
                            s_scr, True)
    h_ref[...] = _lru_tile(zx[...].astype(F32), zx_p[...].astype(F32) * has_prev,
                           zx_n[...].astype(F32) * has_next, cw_ref[...], cb_ref[...], wax_ref[...],
                           bax_ref[...], lam_ref[...], hl_scr, True)
    for h in range(N_HEADS):
        hgs_ref[h] = s_scr[h].T
    lrus_ref[...] = hl_scr[...]


def _zslab(k, tile):
    return pl.BlockSpec((TL, W), lambda s: (tile(s), k))


def _zprev(k, tile):
    return pl.BlockSpec((HALO, W), lambda s: (jnp.maximum(tile(s) * (TL // HALO) - 1, 0), k))


def _znext(k, tile):
    return pl.BlockSpec((HALO, W), lambda s: (jnp.minimum((tile(s) + 1) * (TL // HALO), T // HALO - 1), k))


def _const(shape):
    nd = len(shape)
    return pl.BlockSpec(shape, lambda s: (0,) * nd)


def _layer_row(l, n):
    return pl.BlockSpec((None, 1, n), lambda s: (l, 0, 0))


def _reverse_pass(l, z, G, lev, lb, cw, cb, wax, bax, lam, hg_init, lru_init):
    tile = lambda s: NT - 1 - s
    state_specs = [
        pl.BlockSpec((None, None, None, N_HEADS, HD, HD), lambda s: (_seq_idx(tile(s)), l, 1, 0, 0, 0)),
        pl.BlockSpec((None, None, None, 1, W), lambda s: (_seq_idx(tile(s)), l, 1, 0, 0)),
    ]
    return pl.pallas_call(
        _bwd_kernel,
        grid=(NT,),
        in_specs=[_zslab(S_B_Q, tile), _zslab(S_B_FB, tile), _zslab(S_B_I, tile), _zslab(S_C_X, tile),
                  _zprev(S_C_X, tile), _znext(S_C_X, tile),
                  _const(G.shape), _const(lev.shape),
                  pl.BlockSpec((None, None, 1, W), lambda s: (1, l, 0, 0)),
                  pl.BlockSpec((None, 4, W), lambda s: (l, 0, 0)),
                  _layer_row(l, W),
                  pl.BlockSpec((None, None, W, 2 * W), lambda s: (l, 1, 0, 0)),
                  pl.BlockSpec((None, None, 1, 2 * W), lambda s: (l, 1, 0, 0)),
                  pl.BlockSpec((None, None, 1, W), lambda s: (l, 1, 0, 0)),
                  ] + state_specs,
        out_specs=[pl.BlockSpec((TL, W), lambda s: (tile(s), 0)),
                   pl.BlockSpec((TL, W), lambda s: (tile(s), 0)),
                   pl.BlockSpec((None, N_HEADS, HD, HD), lambda s: (tile(s), 0, 0, 0)),
                   pl.BlockSpec((None, 1, W), lambda s: (tile(s), 0, 0))],
        out_shape=[jax.ShapeDtypeStruct((T, W), F32), jax.ShapeDtypeStruct((T, W), F32),
                   jax.ShapeDtypeStruct((NT, N_HEADS, HD, HD), F32),
                   jax.ShapeDtypeStruct((NT, 1, W), F32)],
        scratch_shapes=[pltpu.VMEM((N_HEADS, HD, HD), F32), pltpu.VMEM((1, W), F32)],
        compiler_params=pltpu.CompilerParams(dimension_semantics=("arbitrary",),
                                             vmem_limit_bytes=VMEM_LIMIT),
        name="reverse_pass",
    )(z, z, z, z, z, z, G, lev, lb, cw, cb, wax, bax, lam, hg_init, lru_init)


def _route(h2, wr, br, cnt_ref):
    logits = _dot_split(h2, wr) + br
    lane = lax.broadcasted_iota(jnp.int32, logits.shape, 1)
    lane_f = lane.astype(F32)
    neg = jnp.float32(-jnp.inf)
    is_g = lane < N_GROUPS
    gl = jnp.where(is_g, logits, neg)
    gmax = jnp.max(gl, axis=-1, keepdims=True)
    g_sel = jnp.min(jnp.where(gl == gmax, lane_f, float(ROUTE_W)), axis=-1, keepdims=True)
    g_p = 1.0 / jnp.sum(jnp.where(is_g, jnp.exp(gl - gmax), 0.0), axis=-1, keepdims=True)
    e_grp = ((lane - EXP_LANE0) >> 2).astype(F32)
    in_grp = (lane >= EXP_LANE0) & (lane < EXP_LANE0 + N_EXPERTS) & (e_grp == g_sel)
    el = jnp.where(in_grp, logits, neg)
    m1 = jnp.max(el, axis=-1, keepdims=True)
    i1 = jnp.min(jnp.where(el == m1, lane_f, float(ROUTE_W)), axis=-1, keepdims=True)
    el2 = jnp.where(lane_f == i1, neg, el)
    m2 = jnp.max(el2, axis=-1, keepdims=True)
    i2 = jnp.min(jnp.where(el2 == m2, lane_f, float(ROUTE_W)), axis=-1, keepdims=True)
    t = jnp.exp(m2 - m1)
    w1 = g_p / (1.0 + t)
    w2 = w1 * t

    first_is_lo = i1 < i2
    lo = jnp.minimum(i1, i2) - EXP_LANE0 - EXP_PER_GROUP * g_sel
    hi = jnp.maximum(i1, i2) - EXP_LANE0 - EXP_PER_GROUP * g_sel
    cls = g_sel * PAIRS + lo * (2 * EXP_PER_GROUP - 1 - lo) * 0.5 + (hi - lo - 1.0)
    oh = (lane_f == cls).astype(F32)
    r = lax.broadcasted_iota(jnp.int32, (TL, TL), 0)
    cc = lax.broadcasted_iota(jnp.int32, (TL, TL), 1)
    before = _dot(jnp.where(r > cc, 1.0, 0.0).astype(BF16), oh.astype(BF16)) + cnt_ref[...]
    rank = jnp.sum(oh * before, axis=-1, keepdims=True)
    cnt_ref[...] += jnp.sum(oh, axis=0, keepdims=True)
    cols = (cls, rank, jnp.where(first_is_lo, w1, w2), jnp.where(first_is_lo, w2, w1))
    out = jnp.zeros(logits.shape, F32)
    for k, col in enumerate(cols):
        out = jnp.where(lane == k, col, out)
    return out


def _fwd_kernel(x_ref, mod_ref,
                za_in, za_b, za_c, za_in_p, za_in_n, za_c_p, za_c_n,
                zq, zf, zi, zg, zx, zx_p, zx_n, zy, zu, zv, gt0, gt1, gt2, gt3,
                ob_ref, hb_ref, G_ref, lev_ref,
                lb_ref, hgn_ref, ca_ref, cw_ref, cb_ref, wax_ref, bax_ref, lam_ref,
                lng_ref, lnb_ref, sgw_ref, sgb_ref, wbr_ref, wout_ref, n2_ref, wr_ref, br_ref,
                hg0_ref, lru0_ref,
                xo_ref, h2t_ref, route_ref, cnt_ref, hgs_ref, lrus_ref,
                s_scr, hl_scr):
    tt = pl.program_id(0)
    first, last = _tile_flags(tt)

    @pl.when(tt == 0)
    def _():
        cnt_ref[...] = jnp.zeros(cnt_ref.shape, F32)

    @pl.when(first)
    def _():
        for h in range(N_HEADS):
            s_scr[h] = hg0_ref[h].T
        hl_scr[...] = lru0_ref[...]

    has_prev = jnp.where(first, 0.0, 1.0)
    has_next = jnp.where(last, 0.0, 1.0)
    f32 = lambda r: r[...].astype(F32)

    p = f32(za_c) * f32(za_in)
    p_p = f32(za_c_p) * f32(za_in_p) * has_prev
    p_n = f32(za_c_n) * f32(za_in_n) * has_next
    ca = ca_ref[...]
    ya = f32(za_b) * sum(ca[k:k + 1, :] * _shifted(p, p_p, p_n, k - 1) for k in range(3))

    zqv = f32(zq)
    q = zqv * jax.nn.sigmoid(zqv)
    o = _hgrn_tile(q, f32(zf), f32(zi), lb_ref[...], G_ref, lev_ref[...], s_scr, False) + ob_ref[...]
    zgv = f32(zg)
    gate_b = hgn_ref[...] * (zgv * jax.nn.sigmoid(zgv))
    yb = jnp.concatenate(
        [o[:, h * HD:(h + 1) * HD]
         * lax.rsqrt(jnp.mean(jnp.square(o[:, h * HD:(h + 1) * HD]), axis=-1, keepdims=True) + EPS)
         for h in range(N_HEADS)], axis=-1) * gate_b

    hf = _lru_tile(f32(zx), f32(zx_p) * has_prev, f32(zx_n) * has_next, cw_ref[...], cb_ref[...],
                   wax_ref[...], bax_ref[...], lam_ref[...], hl_scr, False)
    yc = (hf + hb_ref[...]) * jax.nn.gelu(f32(zy))

    u = jax.nn.gelu(f32(zu))
    v = jax.nn.gelu(f32(zv))
    mu = jnp.mean(v, axis=-1, keepdims=True)
    vc = v - mu
    var = jnp.mean(vc * vc, axis=-1, keepdims=True)
    vn = (vc * lax.rsqrt(var + EPS) * lng_ref[...] + lnb_ref[...]).astype(BF16)
    sgb = sgb_ref[...]
    gw = W // SG_GROUPS
    chunks = []
    for n in range(TL // SG_CHUNK):
        cols = []
        for g in range(SG_GROUPS):
            blk = vn[n * SG_CHUNK:(n + 1) * SG_CHUNK, g * gw:(g + 1) * gw]
            cols.append(_dot(sgw_ref[g], blk) + sgb[:, g:g + 1])
        chunks.append(jnp.concatenate(cols, axis=-1))
    yd = u * jnp.concatenate(chunks, axis=0)

    mix = jnp.zeros((TL, D), F32)
    for k, (y, gt) in enumerate(((ya, gt0), (yb, gt1), (yc, gt2), (yd, gt3))):
        mix = mix + jax.nn.sigmoid(f32(gt)) * _dot(y.astype(BF16), wbr_ref[k])
    x = x_ref[...] + mod_ref[2:3, :] * _dot(mix.astype(BF16), wout_ref[...])
    xo_ref[...] = x

    h2 = x * lax.rsqrt(jnp.mean(x * x, axis=-1, keepdims=True) + EPS) * n2_ref[...]
    h2 = h2 * (1.0 + mod_ref[4:5, :]) + mod_ref[3:4, :]
    for s in range(ROW_TILES):
        h2t_ref[pl.ds(s, TL, stride=ROW_TILES), :] = h2[:, s * 128:(s + 1) * 128]
    route_ref[...] = _route(h2, wr_ref[...], br_ref[...], cnt_ref)

    for h in range(N_HEADS):
        hgs_ref[h] = s_scr[h].T
    lrus_ref[...] = hl_scr[...]


def _forward_pass(l, x, mod, z, o_b, h_b, G, lev, lb, hg_norm, conv_a, cw, cb, wax, bax, lam,
                  ln_g, ln_b, sg_w, sg_bt, w_br, w_out, norm2, w_r, b_r, hg_init, lru_init):
    tile = lambda s: s
    gate = lambda k: pl.BlockSpec((TL, D), lambda s: (s, GATE_BLK + k))
    in_specs = [
        pl.BlockSpec((TL, D), lambda s: (s, 0)),
        pl.BlockSpec((None, None, 6, D), lambda s: (l, _seq_idx(s), 0, 0)),
        _zslab(S_A_IN, tile), _zslab(S_A_B, tile), _zslab(S_A_C, tile),
        _zprev(S_A_IN, tile), _znext(S_A_IN, tile), _zprev(S_A_C, tile), _znext(S_A_C, tile),
        _zslab(S_B_Q, tile), _zslab(S_B_FF, tile), _zslab(S_B_I, tile), _zslab(S_B_G, tile),
        _zslab(S_C_X, tile), _zprev(S_C_X, tile), _znext(S_C_X, tile), _zslab(S_C_Y, tile),
        _zslab(S_D_U, tile), _zslab(S_D_V, tile), gate(0), gate(1), gate(2), gate(3),
        pl.BlockSpec((TL, W), lambda s: (s, 0)), pl.BlockSpec((TL, W), lambda s: (s, 0)),
        _const(G.shape), _const(lev.shape),
        pl.BlockSpec((None, None, 1, W), lambda s: (0, l, 0, 0)),
        _layer_row(l, W),
        pl.BlockSpec((None, 3, W), lambda s: (l, 0, 0)),
        pl.BlockSpec((None, 4, W), lambda s: (l, 0, 0)),
        _layer_row(l, W),
        pl.BlockSpec((None, None, W, 2 * W), lambda s: (l, 0, 0, 0)),
        pl.BlockSpec((None, None, 1, 2 * W), lambda s: (l, 0, 0, 0)),
        pl.BlockSpec((None, None, 1, W), lambda s: (l, 0, 0, 0)),
        _layer_row(l, W), _layer_row(l, W),
        pl.BlockSpec((None, SG_GROUPS, SG_CHUNK, SG_CHUNK), lambda s: (l, 0, 0, 0)),
        pl.BlockSpec((None, SG_CHUNK, SG_GROUPS), lambda s: (l, 0, 0)),
        pl.BlockSpec((None, 4, W, D), lambda s: (l, 0, 0, 0)),
        pl.BlockSpec((None, D, D), lambda s: (l, 0, 0)),
        _layer_row(l, D),
        pl.BlockSpec((None, D, ROUTE_W), lambda s: (l, 0, 0)),
        _layer_row(l, ROUTE_W),
        pl.BlockSpec((None, None, None, N_HEADS, HD, HD), lambda s: (_seq_idx(s), l, 0, 0, 0, 0)),
        pl.BlockSpec((None, None, None, 1, W), lambda s: (_seq_idx(s), l, 0, 0, 0)),
    ]
    return pl.pallas_call(
        _fwd_kernel,
        grid=(NT,),
        in_specs=in_specs,
        out_specs=[pl.BlockSpec((TL, D), lambda s: (s, 0)),
                   pl.BlockSpec((TL * ROW_TILES, 128), lambda s: (s, 0)),
                   pl.BlockSpec((TL, ROUTE_W), lambda s: (s, 0)),
                   pl.BlockSpec((1, ROUTE_W), lambda s: (0, 0)),
                   pl.BlockSpec((None, N_HEADS, HD, HD), lambda s: (s, 0, 0, 0)),
                   pl.BlockSpec((None, 1, W), lambda s: (s, 0, 0))],
        out_shape=[jax.ShapeDtypeStruct((T, D), F32), jax.ShapeDtypeStruct((T * ROW_TILES, 128), F32),
                   jax.ShapeDtypeStruct((T, ROUTE_W), F32), jax.ShapeDtypeStruct((1, ROUTE_W), F32),
                   jax.ShapeDtypeStruct((NT, N_HEADS, HD, HD), F32),
                   jax.ShapeDtypeStruct((NT, 1, W), F32)],
        scratch_shapes=[pltpu.VMEM((N_HEADS, HD, HD), F32), pltpu.VMEM((1, W), F32)],
        compiler_params=pltpu.CompilerParams(dimension_semantics=("arbitrary",),
                                             vmem_limit_bytes=VMEM_LIMIT),
        name="forward_pass",
    )(x, mod, *([z] * 21), o_b, h_b, G, lev, lb, hg_norm, conv_a, cw, cb, wax, bax, lam,
      ln_g, ln_b, sg_w, sg_bt, w_br, w_out, norm2, w_r, b_r, hg_init, lru_init)


def _token_rows(ref, row, n=1, tiles=ROW_TILES):
    return ref.at[pl.ds(row * tiles, n * tiles), :]


def _read_rows(ref, n, tiles=ROW_TILES, first=0):
    return jnp.concatenate([ref[pl.ds(first + s, n, stride=tiles), :] for s in range(ROW_TILES)], axis=1)


def _write_rows(ref, y, tiles=ROW_TILES, first=0):
    for s in range(ROW_TILES):
        ref[pl.ds(first + s, y.shape[0], stride=tiles), :] = y[:, s * 128:(s + 1) * 128]


def _wait_rows(hbm, sem, n, tiles=ROW_TILES):
    pltpu.make_async_copy(_token_rows(hbm, 0, n, tiles), _token_rows(hbm, 0, n, tiles), sem).wait()


def _expert_kernel(tok_ref, nv_ref, lo_ref, hi_ref, tv_ref, h2t_hbm, wg0, wu0, wd0, wg1, wu1, wd1, y2_hbm,
                   xbuf, ybuf, gsem, ssem, sg0, su0, sd0, sg1, su1, sd1):
    i = pl.program_id(0)
    b = i % 2

    def gather(t, buf):
        def start(r, c):
            tok = tok_ref[t * TME + r]
            tok = jnp.where(tok < T, tok, 0)
            pltpu.make_async_copy(_token_rows(h2t_hbm, tok), _token_rows(xbuf.at[buf], r), gsem.at[buf]).start()
            return c

        lax.fori_loop(0, TME, start, 0, unroll=8)

    def scatter(t, buf):
        def start(r, c):
            tok = tok_ref[t * TME + r]

            @pl.when(tok < T)
            def _():
                pltpu.make_async_copy(_token_rows(ybuf.at[buf], r, 1, 2 * ROW_TILES),
                                      _token_rows(y2_hbm, tok, 1, 2 * ROW_TILES), ssem.at[buf]).start()

            return c

        lax.fori_loop(0, TME, start, 0, unroll=8)

    @pl.when(i == 0)
    def _():
        gather(0, 0)

    nxt = jnp.minimum(i + 1, N_ETILES - 1)

    @pl.when(jnp.logical_and(i + 1 < N_ETILES, tv_ref[nxt] == 1))
    def _():
        gather(i + 1, 1 - b)

    @pl.when(tv_ref[i] == 1)
    def _():
        prev = jnp.maximum(i - 1, 0)
        sets = ((lo_ref, (wg0, wu0, wd0), (sg0, su0, sd0)), (hi_ref, (wg1, wu1, wd1), (sg1, su1, sd1)))
        for e_ref, w_refs, s_refs in sets:
            @pl.when(jnp.logical_or(i == 0, e_ref[i] != e_ref[prev]))
            def _():
                for w, sc in zip(w_refs, s_refs):
                    sc[...] = w[...].astype(BF16)

        _wait_rows(h2t_hbm, gsem.at[b], TME)

        @pl.when(i >= 2)
        def _():
            _wait_rows(y2_hbm, ssem.at[b], nv_ref[jnp.maximum(i - 2, 0)], 2 * ROW_TILES)

        x = _read_rows(xbuf.at[b], TME).astype(BF16)
        for k, (_, _, (sg, su, sd)) in enumerate(sets):
            a = _dot(x, sg[...])
            u = _dot(x, su[...])
            y = _dot((a * jax.nn.sigmoid(a) * u).astype(BF16), sd[...])
            _write_rows(ybuf.at[b], y, 2 * ROW_TILES, k * ROW_TILES)
        scatter(i, b)

        @pl.when(jnp.logical_or(i == N_ETILES - 1, tv_ref[nxt] == 0))
        def _():
            _wait_rows(y2_hbm, ssem.at[b], nv_ref[i], 2 * ROW_TILES)

            @pl.when(i >= 1)
            def _():
                _wait_rows(y2_hbm, ssem.at[1 - b], nv_ref[prev], 2 * ROW_TILES)


def _grouped_experts(l, tok, nv, e_lo, e_hi, tv, h2t, w_gate, w_up, w_down):
    def wspec(shape, which):
        return pl.BlockSpec((None, None) + shape, lambda i, tok, nv, lo, hi, tv: (l, (lo, hi)[which][i], 0, 0))

    wshapes = ((D, D_EXPERT), (D, D_EXPERT), (D_EXPERT, D))
    return pl.pallas_call(
        _expert_kernel,
        grid_spec=pltpu.PrefetchScalarGridSpec(
            num_scalar_prefetch=5, grid=(N_ETILES,),
            in_specs=[pl.BlockSpec(memory_space=pl.ANY)]
            + [wspec(shape, which) for which in range(2) for shape in wshapes],
            out_specs=pl.BlockSpec(memory_space=pl.ANY),
            scratch_shapes=[pltpu.VMEM((2, TME * ROW_TILES, 128), F32), pltpu.VMEM((2, TME * 2 * ROW_TILES, 128), F32),
                            pltpu.SemaphoreType.DMA((2,)), pltpu.SemaphoreType.DMA((2,))]
            + [pltpu.VMEM(shape, BF16) for _ in range(2) for shape in wshapes]),
        out_shape=jax.ShapeDtypeStruct((T * 2 * ROW_TILES, 128), F32),
        compiler_params=pltpu.CompilerParams(dimension_semantics=("arbitrary",),
                                             vmem_limit_bytes=VMEM_LIMIT),
        name="grouped_experts",
    )(tok, nv, e_lo, e_hi, tv, h2t, w_gate, w_up, w_down, w_gate, w_up, w_down)


def _combine_kernel(x_ref, mod_ref, route_ref, y2_ref, o_ref):
    route = route_ref[...]
    y = (route[:, 2:3] * _read_rows(y2_ref, TL, 2 * ROW_TILES, 0)
         + route[:, 3:4] * _read_rows(y2_ref, TL, 2 * ROW_TILES, ROW_TILES))
    o_ref[...] = x_ref[...] + mod_ref[5:6, :] * y


def _combine(l, x, mod, route, y2):
    return pl.pallas_call(
        _combine_kernel,
        grid=(NT,),
        in_specs=[pl.BlockSpec((TL, D), lambda s: (s, 0)),
                  pl.BlockSpec((None, None, 6, D), lambda s: (l, _seq_idx(s), 0, 0)),
                  pl.BlockSpec((TL, ROUTE_W), lambda s: (s, 0)),
                  pl.BlockSpec((TL * 2 * ROW_TILES, 128), lambda s: (s, 0))],
        out_specs=pl.BlockSpec((TL, D), lambda s: (s, 0)),
        out_shape=jax.ShapeDtypeStruct((T, D), F32),
        compiler_params=pltpu.CompilerParams(dimension_semantics=("arbitrary",)),
        name="combine",
    )(x, mod, route, y2)


def _expert_pass(l, x, mod, h2t, route, cnt, w_gate, w_up, w_down):
    cls, rank = route[:, 0].astype(jnp.int32), route[:, 1].astype(jnp.int32)
    cnt = cnt[0, :N_CLASSES].astype(jnp.int32)
    tiles = (cnt + TME - 1) // TME
    ends = jnp.cumsum(tiles)
    seg = (ends - tiles) * TME
    slot = jnp.sum(jnp.where(cls[:, None] == jnp.arange(N_CLASSES, dtype=jnp.int32)[None, :], seg[None, :], 0),
                   axis=1) + rank
    tok = jnp.full((N_ROWS,), T, jnp.int32).at[slot].set(jnp.arange(T, dtype=jnp.int32), unique_indices=True)
    nv = jnp.sum((tok < T).reshape(N_ETILES, TME), axis=1, dtype=jnp.int32)
    i = jnp.arange(N_ETILES, dtype=jnp.int32)
    tv = (i < ends[-1]).astype(jnp.int32)
    tc = jnp.sum(ends[None, :] <= jnp.minimum(i, ends[-1] - 1)[:, None], axis=1, dtype=jnp.int32)
    grp, pair = tc // PAIRS, tc % PAIRS
    e_lo = grp * EXP_PER_GROUP + jnp.asarray(PAIR_LO, jnp.int32)[pair]
    e_hi = grp * EXP_PER_GROUP + jnp.asarray(PAIR_HI, jnp.int32)[pair]
    y2 = _grouped_experts(l, tok, nv, e_lo, e_hi, tv, h2t, w_gate, w_up, w_down)
    return _combine(l, x, mod, route, y2)


def _final_kernel(x_ref, g_ref, o_ref):
    x = x_ref[...]
    o_ref[...] = x * lax.rsqrt(jnp.mean(x * x, axis=-1, keepdims=True) + EPS) * g_ref[...]


def _final_norm(x, g, first_row, rows):
    tm = 1024
    return pl.pallas_call(
        _final_kernel,
        grid=(rows // tm,),
        in_specs=[pl.BlockSpec((tm, D), lambda i: (first_row // tm + i, 0)), pl.BlockSpec((1, D), lambda i: (0, 0))],
        out_specs=pl.BlockSpec((tm, D), lambda i: (i, 0)),
        out_shape=jax.ShapeDtypeStruct((rows, D), F32),
        compiler_params=pltpu.CompilerParams(dimension_semantics=("arbitrary",)),
        name="final_norm",
    )(x, g.reshape(1, D))


def _grid_pos_embed(rows, d):
    nf = d // 4
    freqs = jnp.exp(-math.log(10000.0) * jnp.arange(nf, dtype=F32) / nf)
    r = jnp.arange(rows, dtype=F32)[:, None] * freqs
    cl = jnp.arange(GRID_W, dtype=F32)[:, None] * freqs
    r_emb = jnp.concatenate([jnp.sin(r), jnp.cos(r)], axis=-1)
    c_emb = jnp.concatenate([jnp.sin(cl), jnp.cos(cl)], axis=-1)
    emb = jnp.concatenate([jnp.broadcast_to(r_emb[:, None], (rows, GRID_W, d // 2)),
                           jnp.broadcast_to(c_emb[None], (rows, GRID_W, d // 2))], axis=-1)
    return emb.reshape(rows * GRID_W, d)


def _block_diag(w):
    n, c = w.shape[-3], w.shape[-1]
    eye = jnp.eye(n, dtype=w.dtype)
    full = w[..., :, :, None, :] * eye[:, None, :, None]
    return full.reshape(*w.shape[:-3], n * c, n * c)


def kernel(x_prompt, x_sample, state_hgrn, state_lru, c, c_ctx, w_ada, b_ada, norm1, norm2, w_in, conv_a,
           hg_lb, hg_norm, lru_conv_w, lru_conv_b, lru_wa, lru_ba, lru_wx, lru_bx, lru_lam, sg_ln_g, sg_ln_b,
           sg_w, sg_b, w_br, w_out, w_rg, b_rg, w_re, b_re, w_gate, w_up, w_down, norm_f):
    assert x_prompt.shape == (N_CTX, L_CTX, D) and x_sample.shape == (N_LAT, L_LAT, D)

    xs = x_sample.astype(F32) + _grid_pos_embed(L_LAT // GRID_W, D)
    x = jnp.concatenate([x_prompt.astype(F32).reshape(-1, D), xs.reshape(-1, D)], axis=0)

    cond = jnp.concatenate([c_ctx.astype(F32)[None], c.astype(F32), jnp.zeros((8 - 1 - N_LAT, D), F32)], axis=0)
    mod = _modulation(cond, w_ada, b_ada).reshape(DEPTH, 8, 6, D)

    cs = jnp.cumsum(jax.nn.softmax(hg_lb.astype(F32), axis=1), axis=1)
    lb = (cs - cs[:, :1]).reshape(2, DEPTH, 1, W)
    wax = jnp.concatenate([_block_diag(lru_wa), _block_diag(lru_wx)], axis=-1).astype(BF16)
    bax = jnp.concatenate([lru_ba, lru_bx], axis=-1).reshape(DEPTH, 2, 1, 2 * W)
    lam = lru_lam.reshape(DEPTH, 2, 1, W)
    row = lambda a: a.reshape(DEPTH, 1, a.shape[-1])
    sg_bt = jnp.swapaxes(sg_b, 1, 2)
    rpad = ROUTE_W - N_GROUPS - N_EXPERTS
    w_r = jnp.concatenate([w_rg, w_re, jnp.zeros((DEPTH, D, rpad), F32)], axis=-1)
    b_r = jnp.concatenate([b_rg, b_re, jnp.zeros((DEPTH, rpad), F32)], axis=-1)
    w_br_b = w_br.astype(BF16)
    w_out_b = w_out.astype(BF16)
    sg_w_b = sg_w.astype(BF16)

    consts = [_scan_constants(rev) for rev in (False, True)]
    G_f, G_b = (jnp.asarray(g, BF16) for g, _ in consts)
    lev_f, lev_b = (jnp.asarray(v) for _, v in consts)

    hg_init = jnp.concatenate([jnp.zeros((1,) + state_hgrn.shape[1:], F32), state_hgrn.astype(F32)], axis=0)
    lru_init = jnp.concatenate([jnp.zeros((1,) + state_lru.shape[1:], F32), state_lru.astype(F32)], axis=0)
    lru_init = lru_init.reshape(1 + N_LAT, DEPTH, 2, 1, W)

    hg_states, lru_states = [], []
    for l in range(DEPTH):
        z = _in_projection(l, x, mod, row(norm1), w_in)
        o_b, h_b, hgs_b, lrus_b = _reverse_pass(l, z, G_b, lev_b, lb, lru_conv_w, row(lru_conv_b), wax, bax, lam,
                                                hg_init, lru_init)
        x, h2t, route, cnt, hgs_f, lrus_f = _forward_pass(
            l, x, mod, z, o_b, h_b, G_f, lev_f, lb, row(hg_norm), conv_a, lru_conv_w, row(lru_conv_b), wax, bax,
            lam, row(sg_ln_g), row(sg_ln_b), sg_w_b, sg_bt, w_br_b, w_out_b, row(norm2), w_r, row(b_r),
            hg_init, lru_init)
        x = _expert_pass(l, x, mod, h2t, route, cnt, w_gate, w_up, w_down)
        hg_states.append(jnp.stack([hgs_f[:CTX_TILES], hgs_b[:CTX_TILES]], axis=1))
        lru_states.append(jnp.stack([lrus_f[:CTX_TILES, 0], lrus_b[:CTX_TILES, 0]], axis=1))

    n_ctx = N_CTX * L_CTX
    y_prompt = _final_norm(x, norm_f, 0, n_ctx).reshape(N_CTX, L_CTX, D)
    y_sample = _final_norm(x, norm_f, n_ctx, T - n_ctx).reshape(N_LAT, L_LAT, D)
    return y_prompt, y_sample, jnp.stack(hg_states, axis=1), jnp.stack(lru_states, axis=1)
```

```python
import functools
import math

import numpy as np
import jax
import jax.numpy as jnp
from jax import lax
from jax.experimental import pallas as pl
from jax.experimental.pallas import tpu as pltpu

F32 = jnp.float32
BF16 = jnp.bfloat16

D = 1024
W = 512
DEPTH = 4
GRID_W = 64
N_HEADS = 4
HD = 128
LRU_BLOCKS = 8
LRU_BLOCK = W // LRU_BLOCKS
LRU_C = 8.0
SG_CHUNK = 128
SG_GROUPS = 4
N_GROUPS = 4
EXP_PER_GROUP = 4
N_EXPERTS = 16
D_EXPERT = 512
EPS = 1e-6
LOG2E = 1.4426950408889634
D_IN = 12 * W + 4 * D

S_A_IN, S_A_B, S_A_C, S_B_Q, S_B_FF, S_B_FB, S_B_I, S_B_G, S_C_X, S_C_Y, S_D_U, S_D_V = range(12)
GATE_BLK = (12 * W) // D

N_CTX, L_CTX = 16, 256
N_LAT, L_LAT = 2, 2048
T = N_CTX * L_CTX + N_LAT * L_LAT
TL = 256
NT = T // TL
CTX_TILES = (N_CTX * L_CTX) // TL
LAT_TILES = L_LAT // TL
HC = 128
NCH = TL // HC
LOG_HC = 7
N_LEVELS = LOG_HC
N_MXU_LEVELS = 3
SUB = 8
HALO = 16
ROUTE_W = 128
EXP_LANE0 = N_GROUPS

VMEM_LIMIT = 56 * 1024 * 1024

TM_IN, TN_IN = 2048, 1024
TN_MOD = 1536
ROW_TILES = D // 128
TME = 256
TD = 1024
PAIRS = EXP_PER_GROUP * (EXP_PER_GROUP - 1) // 2
N_CLASSES = N_GROUPS * PAIRS
N_ETILES = T // TME + N_CLASSES
N_ROWS = N_ETILES * TME
PAIR_LO = [a for a in range(EXP_PER_GROUP) for b in range(a + 1, EXP_PER_GROUP)]
PAIR_HI = [b for a in range(EXP_PER_GROUP) for b in range(a + 1, EXP_PER_GROUP)]


def _scan_constants(reverse):
    i = np.arange(TL)[:, None]
    s = np.arange(TL)[None, :]
    mats = [(i // HC == s // HC) & (s <= i)]
    for B in (8, 4, 2):
        m = (i // B) * B + B // 2 - 1
        mats.append(np.where(i > m, (s > m) & (s <= i), (s > i) & (s <= m)))
    if reverse:
        mats = [g[::-1, ::-1] for g in mats]
    G = np.concatenate(mats, 0).astype(np.float32)
    r = np.arange(HC)[:, None]
    c = np.arange(HC)[None, :]
    lev = (LOG_HC - 1) - np.floor(np.log2(np.maximum(r ^ c, 1))).astype(np.int32)
    lev = np.where(r == c, N_LEVELS, lev)
    causal = (r < c) if reverse else (r > c)
    lev = np.where(causal | (r == c), lev, N_LEVELS + 1).astype(np.int32)
    return G, lev


def _split2(x):
    hi = x.astype(BF16)
    lo = (x - hi.astype(F32)).astype(BF16)
    return hi, lo


def _dot(a, b):
    return jnp.dot(a, b, preferred_element_type=F32)


def _dot_nt(a, b):
    return lax.dot_general(a, b, (((1,), (1,)), ((), ())), preferred_element_type=F32)


def _dot_split(a, b):
    ah, al = _split2(a)
    bh, bl = _split2(b)
    return _dot(ah, bh) + (_dot(ah, bl) + _dot(al, bh))


def _log_sigmoid(x):
    return jnp.minimum(x, 0.0) - jnp.log1p(jnp.exp(-jnp.abs(x)))


def _softplus(x):
    return jnp.maximum(x, 0.0) + jnp.log1p(jnp.exp(-jnp.abs(x)))


def _tile_flags(tt):
    is_lat = tt >= CTX_TILES
    pos = tt & (LAT_TILES - 1)
    first = jnp.logical_or(jnp.logical_not(is_lat), pos == 0)
    last = jnp.logical_or(jnp.logical_not(is_lat), pos == LAT_TILES - 1)
    return first, last


def _seq_idx(t):
    return jnp.where(t < CTX_TILES, 0, 1 + (t - CTX_TILES) // LAT_TILES)


def _shifted(x, prev, nxt, k):
    if k == 0:
        return x
    rows = lax.broadcasted_iota(jnp.int32, (SUB, x.shape[1]), 0)
    if k < 0:
        y = pltpu.roll(x, -k, 0)
        edge = y[:SUB]
        for r in range(-k):
            edge = jnp.where(rows == r, prev[HALO + k + r:HALO + k + r + 1, :], edge)
        return jnp.concatenate([edge, y[SUB:]], axis=0)
    y = pltpu.roll(x, TL - k, 0)
    edge = y[TL - SUB:]
    for r in range(k):
        edge = jnp.where(rows == SUB - k + r, nxt[r:r + 1, :], edge)
    return jnp.concatenate([y[:TL - SUB], edge], axis=0)


def _block_anchor(b, B, a):
    return jnp.concatenate([jnp.broadcast_to(b[s0 + a:s0 + a + 1, :], (B, b.shape[1]))
                            for s0 in range(0, TL, B)], axis=0)


def _hgrn_tile(q, zf, v, lb, G_ref, lev, s_ref, reverse):
    ls = _log_sigmoid(zf)
    bb = jnp.log1p(-lb) + ls
    la = jnp.log(lb)
    lf = jnp.maximum(la, bb) + jnp.log1p(jnp.exp(-jnp.abs(la - bb)))
    kk = (1.0 - lb) * jax.nn.sigmoid(-zf)

    lf = lf * LOG2E
    hi, mid = _split2(lf)
    g = G_ref[...]
    sums = _dot(g, hi) + _dot(g, mid)
    b = sums[0:TL]
    e_lev = []
    row = lax.broadcasted_iota(jnp.int32, (TL, HD), 0)
    for l in range(N_LEVELS - N_MXU_LEVELS):
        B = HC >> l
        anchor = _block_anchor(b, B, B // 2 if reverse else B // 2 - 1)
        later = ((row & (B // 2)) == 0) if reverse else ((row & (B // 2)) != 0)
        sign = jnp.concatenate([jnp.where(later, 1.0, -1.0)] * N_HEADS, axis=1)
        e_lev.append(jnp.exp2((b - anchor) * sign).astype(BF16))
    for l in range(N_MXU_LEVELS):
        e_lev.append(jnp.exp2(sums[(1 + l) * TL:(2 + l) * TL]).astype(BF16))
    end = 0 if reverse else HC - 1
    e_b = jnp.exp2(b)
    e_e = jnp.exp2(_block_anchor(b, HC, end) - b)
    masks = [lev == l for l in range(N_LEVELS)]
    q_b = q.astype(BF16)
    k_b = kk.astype(BF16)

    outs = []
    for h in range(N_HEADS):
        sl = slice(h * HD, (h + 1) * HD)
        st = s_ref[h]
        o_h = [None] * NCH
        for c in (reversed(range(NCH)) if reverse else range(NCH)):
            rs = slice(c * HC, (c + 1) * HC)
            q_c, k_c, v_c = q[rs, sl], kk[rs, sl], v[rs, sl]
            d = jnp.zeros((HC, HC), F32)
            for l in range(N_LEVELS):
                e = e_lev[l][rs, sl]
                p = _dot_nt(q_b[rs, sl] * e, k_b[rs, sl] * e)
                d = jnp.where(masks[l], p, d)
            o = _dot(d.astype(BF16), v_c.astype(BF16))
            o = o + jnp.sum(q_c * k_c, axis=-1, keepdims=True) * v_c
            o = o + _dot_nt((q_c * e_b[rs, sl]).astype(BF16), st.astype(BF16))
            upd = _dot(v_c.T.astype(BF16), (k_c * e_e[rs, sl]).astype(BF16))
            st = e_b[c * HC + end:c * HC + end + 1, sl] * st + upd
            o_h[c] = o
        s_ref[h] = st
        outs.append(jnp.concatenate(o_h, axis=0))
    return jnp.concatenate(outs, axis=-1)


def _lin_scan(a, u, carry, reverse):
    sub = lax.broadcasted_iota(jnp.int32, a.shape, 0) & (SUB - 1)
    d = 1
    while d < SUB:
        if reverse:
            a_s, u_s, ok = pltpu.roll(a, TL - d, 0), pltpu.roll(u, TL - d, 0), sub < SUB - d
        else:
            a_s, u_s, ok = pltpu.roll(a, d, 0), pltpu.roll(u, d, 0), sub >= d
        u = jnp.where(ok, a * u_s + u, u)
        a = jnp.where(ok, a * a_s, a)
        d *= 2
    n = TL // SUB
    blocks = [None] * n
    for g in (reversed(range(n)) if reverse else range(n)):
        blk = u[g * SUB:(g + 1) * SUB] + a[g * SUB:(g + 1) * SUB] * carry
        carry = blk[0:1] if reverse else blk[SUB - 1:SUB]
        blocks[g] = blk
    return jnp.concatenate(blocks, axis=0), carry


def _lru_tile(zx, zx_p, zx_n, cw, cb, wax, bax, lam, hl_ref, reverse):
    xc = cb + sum(cw[k:k + 1, :] * _shifted(zx, zx_p, zx_n, k - 2) for k in range(4))
    rx = _dot(xc.astype(BF16), wax) + bax
    r = jax.nn.sigmoid(rx[:, :W])
    ig = jax.nn.sigmoid(rx[:, W:])
    log_a = -LRU_C * r * _softplus(-lam)
    a = jnp.exp(log_a)
    u = jnp.sqrt(-jnp.tanh(log_a) * (1.0 + a * a)) * (ig * xc)
    hseq, carry = _lin_scan(a, u, hl_ref[...], reverse)
    hl_ref[...] = carry
    return hseq


def _mod_kernel(c_ref, w_ref, b_ref, o_ref):
    c = c_ref[...]
    act = c * jax.nn.sigmoid(c)
    o_ref[...] = _dot_split(act, w_ref[...]) + b_ref[...]


def _modulation(cond, w_ada, b_ada):
    n = 6 * D // TN_MOD
    return pl.pallas_call(
        _mod_kernel,
        grid=(DEPTH, n),
        in_specs=[pl.BlockSpec((8, D), lambda l, j: (0, 0)),
                  pl.BlockSpec((None, D, TN_MOD), lambda l, j: (l, 0, j)),
                  pl.BlockSpec((None, 1, TN_MOD), lambda l, j: (l, 0, j))],
        out_specs=pl.BlockSpec((None, 8, TN_MOD), lambda l, j: (l, 0, j)),
        out_shape=jax.ShapeDtypeStruct((DEPTH, 8, 6 * D), F32),
        compiler_params=pltpu.CompilerParams(dimension_semantics=("arbitrary", "arbitrary"),
                                             vmem_limit_bytes=VMEM_LIMIT),
        name="modulation",
    )(cond, w_ada, b_ada.reshape(DEPTH, 1, 6 * D))


def _in_kernel(x_ref, mod_ref, g_ref, w_ref, z_ref, h_scr):
    @pl.when(pl.program_id(1) == 0)
    def _():
        x = x_ref[...]
        h = x * lax.rsqrt(jnp.mean(x * x, axis=-1, keepdims=True) + EPS) * g_ref[...]
        h = h * (1.0 + mod_ref[1:2, :]) + mod_ref[0:1, :]
        h_scr[...] = h.astype(BF16)

    z_ref[...] = _dot(h_scr[...], w_ref[...].astype(BF16)).astype(BF16)


def _in_projection(l, x, mod, norm1, w_in):
    tiles_per_seq = L_LAT // TM_IN
    ctx = (N_CTX * L_CTX) // TM_IN

    def seq(i):
        return jnp.where(i < ctx, 0, 1 + (i - ctx) // tiles_per_seq)

    return pl.pallas_call(
        _in_kernel,
        grid=(T // TM_IN, D_IN // TN_IN),
        in_specs=[pl.BlockSpec((TM_IN, D), lambda i, j: (i, 0)),
                  pl.BlockSpec((None, None, 6, D), lambda i, j: (l, seq(i), 0, 0)),
                  pl.BlockSpec((None, 1, D), lambda i, j: (l, 0, 0)),
                  pl.BlockSpec((None, D, TN_IN), lambda i, j: (l, 0, j))],
        out_specs=pl.BlockSpec((TM_IN, TN_IN), lambda i, j: (i, j)),
        out_shape=jax.ShapeDtypeStruct((T, D_IN), BF16),
        scratch_shapes=[pltpu.VMEM((TM_IN, D), BF16)],
        compiler_params=pltpu.CompilerParams(dimension_semantics=("arbitrary", "arbitrary"),
                                             vmem_limit_bytes=VMEM_LIMIT),
        name="in_projection",
    )(x, mod, norm1, w_in)


def _bwd_kernel(zq, zf, zi, zx, zx_p, zx_n, G_ref, lev_ref, lb_ref, cw_ref, cb_ref, wax_ref, bax_ref,
                lam_ref, hg0_ref, lru0_ref,
                o_ref, h_ref, hgs_ref, lrus_ref, s_scr, hl_scr):
    tt = NT - 1 - pl.program_id(0)
    first, last = _tile_flags(tt)

    @pl.when(last)
    def _():
        for h in range(N_HEADS):
            s_scr[h] = hg0_ref[h].T
        hl_scr[...] = lru0_ref[...]

    has_prev = jnp.where(first, 0.0, 1.0)
    has_next = jnp.where(last, 0.0, 1.0)
    zqv = zq[...].astype(F32)
    q = zqv * jax.nn.sigmoid(zqv)
    o_ref[...] = _hgrn_tile(q, zf[...].astype(F32), zi[...].astype(F32), lb_ref[...], G_ref, lev_ref[...],
                            s_scr, True)
    h_ref[...] = _lru_tile(zx[...].astype(F32), zx_p[...].astype(F32) * has_prev,
                           zx_n[...].astype(F32) * has_next, cw_ref[...], cb_ref[...], wax_ref[...],
                           bax_ref[...], lam_ref[...], hl_scr, True)
    for h in range(N_HEADS):
        hgs_ref[h] = s_scr[h].T
    lrus_ref[...] = hl_scr[...]


def _zslab(k, tile):
    return pl.BlockSpec((TL, W), lambda s: (tile(s), k))


def _zprev(k, tile):
    return pl.BlockSpec((HALO, W), lambda s: (jnp.maximum(tile(s) * (TL // HALO) - 1, 0), k))


def _znext(k, tile):
    return pl.BlockSpec((HALO, W), lambda s: (jnp.minimum((tile(s) + 1) * (TL // HALO), T // HALO - 1), k))


def _const(shape):
    nd = len(shape)
    return pl.BlockSpec(shape, lambda s: (0,) * nd)


def _layer_row(l, n):
    return pl.BlockSpec((None, 1, n), lambda s: (l, 0, 0))


def _reverse_pass(l, z, G, lev, lb, cw, cb, wax, bax, lam, hg_init, lru_init):
    tile = lambda s: NT - 1 - s
    state_specs = [
        pl.BlockSpec((None, None, None, N_HEADS, HD, HD), lambda s: (_seq_idx(tile(s)), l, 1, 0, 0, 0)),
        pl.BlockSpec((None, None, None, 1, W), lambda s: (_seq_idx(tile(s)), l, 1, 0, 0)),
    ]
    return pl.pallas_call(
        _bwd_kernel,
        grid=(NT,),
        in_specs=[_zslab(S_B_Q, tile), _zslab(S_B_FB, tile), _zslab(S_B_I, tile), _zslab(S_C_X, tile),
                  _zprev(S_C_X, tile), _znext(S_C_X, tile),
                  _const(G.shape), _const(lev.shape),
                  pl.BlockSpec((None, None, 1, W), lambda s: (1, l, 0, 0)),
                  pl.BlockSpec((None, 4, W), lambda s: (l, 0, 0)),
                  _layer_row(l, W),
                  pl.BlockSpec((None, None, W, 2 * W), lambda s: (l, 1, 0, 0)),
                  pl.BlockSpec((None, None, 1, 2 * W), lambda s: (l, 1, 0, 0)),
                  pl.BlockSpec((None, None, 1, W), lambda s: (l, 1, 0, 0)),
                  ] + state_specs,
        out_specs=[pl.BlockSpec((TL, W), lambda s: (tile(s), 0)),
                   pl.BlockSpec((TL, W), lambda s: (tile(s), 0)),
                   pl.BlockSpec((None, N_HEADS, HD, HD), lambda s: (tile(s), 0, 0, 0)),
                   pl.BlockSpec((None, 1, W), lambda s: (tile(s), 0, 0))],
        out_shape=[jax.ShapeDtypeStruct((T, W), F32), jax.ShapeDtypeStruct((T, W), F32),
                   jax.ShapeDtypeStruct((NT, N_HEADS, HD, HD), F32),
                   jax.ShapeDtypeStruct((NT, 1, W), F32)],
        scratch_shapes=[pltpu.VMEM((N_HEADS, HD, HD), F32), pltpu.VMEM((1, W), F32)],
        compiler_params=pltpu.CompilerParams(dimension_semantics=("arbitrary",),
                                             vmem_limit_bytes=VMEM_LIMIT),
        name="reverse_pass",
    )(z, z, z, z, z, z, G, lev, lb, cw, cb, wax, bax, lam, hg_init, lru_init)


def _route(h2, wr, br, cnt_ref):
    logits = _dot_split(h2, wr) + br
    lane = lax.broadcasted_iota(jnp.int32, logits.shape, 1)
    lane_f = lane.astype(F32)
    neg = jnp.float32(-jnp.inf)
    is_g = lane < N_GROUPS
    gl = jnp.where(is_g, logits, neg)
    gmax = jnp.max(gl, axis=-1, keepdims=True)
    g_sel = jnp.min(jnp.where(gl == gmax, lane_f, float(ROUTE_W)), axis=-1, keepdims=True)
    g_p = 1.0 / jnp.sum(jnp.where(is_g, jnp.exp(gl - gmax), 0.0), axis=-1, keepdims=True)
    e_grp = ((lane - EXP_LANE0) >> 2).astype(F32)
    in_grp = (lane >= EXP_LANE0) & (lane < EXP_LANE0 + N_EXPERTS) & (e_grp == g_sel)
    el = jnp.where(in_grp, logits, neg)
    m1 = jnp.max(el, axis=-1, keepdims=True)
    i1 = jnp.min(jnp.where(el == m1, lane_f, float(ROUTE_W)), axis=-1, keepdims=True)
    el2 = jnp.where(lane_f == i1, neg, el)
    m2 = jnp.max(el2, axis=-1, keepdims=True)
    i2 = jnp.min(jnp.where(el2 == m2, lane_f, float(ROUTE_W)), axis=-1, keepdims=True)
    t = jnp.exp(m2 - m1)
    w1 = g_p / (1.0 + t)
    w2 = w1 * t

    first_is_lo = i1 < i2
    lo = jnp.minimum(i1, i2) - EXP_LANE0 - EXP_PER_GROUP * g_sel
    hi = jnp.maximum(i1, i2) - EXP_LANE0 - EXP_PER_GROUP * g_sel
    cls = g_sel * PAIRS + lo * (2 * EXP_PER_GROUP - 1 - lo) * 0.5 + (hi - lo - 1.0)
    oh = (lane_f == cls).astype(F32)
    r = lax.broadcasted_iota(jnp.int32, (TL, TL), 0)
    cc = lax.broadcasted_iota(jnp.int32, (TL, TL), 1)
    before = _dot(jnp.where(r > cc, 1.0, 0.0).astype(BF16), oh.astype(BF16)) + cnt_ref[...]
    rank = jnp.sum(oh * before, axis=-1, keepdims=True)
    cnt_ref[...] += jnp.sum(oh, axis=0, keepdims=True)
    cols = (cls, rank, jnp.where(first_is_lo, w1, w2), jnp.where(first_is_lo, w2, w1))
    out = jnp.zeros(logits.shape, F32)
    for k, col in enumerate(cols):
        out = jnp.where(lane == k, col, out)
    return out


def _fwd_kernel(x_ref, mod_ref,
                za_in, za_b, za_c, za_in_p, za_in_n, za_c_p, za_c_n,
                zq, zf, zi, zg, zx, zx_p, zx_n, zy, zu, zv, gt0, gt1, gt2, gt3,
                ob_ref, hb_ref, G_ref, lev_ref,
                lb_ref, hgn_ref, ca_ref, cw_ref, cb_ref, wax_ref, bax_ref, lam_ref,
                lng_ref, lnb_ref, sgw_ref, sgb_ref, wbr_ref, wout_ref, n2_ref, wr_ref, br_ref,
                hg0_ref, lru0_ref,
                xo_ref, h2t_ref, route_ref, cnt_ref, hgs_ref, lrus_ref,
                s_scr, hl_scr):
    tt = pl.program_id(0)
    first, last = _tile_flags(tt)

    @pl.when(tt == 0)
    def _():
        cnt_ref[...] = jnp.zeros(cnt_ref.shape, F32)

    @pl.when(first)
    def _():
        for h in range(N_HEADS):
            s_scr[h] = hg0_ref[h].T
        hl_scr[...] = lru0_ref[...]

    has_prev = jnp.where(first, 0.0, 1.0)
    has_next = jnp.where(last, 0.0, 1.0)
    f32 = lambda r: r[...].astype(F32)

    p = f32(za_c) * f32(za_in)
    p_p = f32(za_c_p) * f32(za_in_p) * has_prev
    p_n = f32(za_c_n) * f32(za_in_n) * has_next
    ca = ca_ref[...]
    ya = f32(za_b) * sum(ca[k:k + 1, :] * _shifted(p, p_p, p_n, k - 1) for k in range(3))

    zqv = f32(zq)
    q = zqv * jax.nn.sigmoid(zqv)
    o = _hgrn_tile(q, f32(zf), f32(zi), lb_ref[...], G_ref, lev_ref[...], s_scr, False) + ob_ref[...]
    zgv = f32(zg)
    gate_b = hgn_ref[...] * (zgv * jax.nn.sigmoid(zgv))
    yb = jnp.concatenate(
        [o[:, h * HD:(h + 1) * HD]
         * lax.rsqrt(jnp.mean(jnp.square(o[:, h * HD:(h + 1) * HD]), axis=-1, keepdims=True) + EPS)
         for h in range(N_HEADS)], axis=-1) * gate_b

    hf = _lru_tile(f32(zx), f32(zx_p) * has_prev, f32(zx_n) * has_next, cw_ref[...], cb_ref[...],
                   wax_ref[...], bax_ref[...], lam_ref[...], hl_scr, False)
    yc = (hf + hb_ref[...]) * jax.nn.gelu(f32(zy))

    u = jax.nn.gelu(f32(zu))
    v = jax.nn.gelu(f32(zv))
    mu = jnp.mean(v, axis=-1, keepdims=True)
    vc = v - mu
    var = jnp.mean(vc * vc, axis=-1, keepdims=True)
    vn = (vc * lax.rsqrt(var + EPS) * lng_ref[...] + lnb_ref[...]).astype(BF16)
    sgb = sgb_ref[...]
    gw = W // SG_GROUPS
    chunks = []
    for n in range(TL // SG_CHUNK):
        cols = []
        for g in range(SG_GROUPS):
            blk = vn[n * SG_CHUNK:(n + 1) * SG_CHUNK, g * gw:(g + 1) * gw]
            cols.append(_dot(sgw_ref[g], blk) + sgb[:, g:g + 1])
        chunks.append(jnp.concatenate(cols, axis=-1))
    yd = u * jnp.concatenate(chunks, axis=0)

    mix = jnp.zeros((TL, D), F32)
    for k, (y, gt) in enumerate(((ya, gt0), (yb, gt1), (yc, gt2), (yd, gt3))):
        mix = mix + jax.nn.sigmoid(f32(gt)) * _dot(y.astype(BF16), wbr_ref[k])
    x = x_ref[...] + mod_ref[2:3, :] * _dot(mix.astype(BF16), wout_ref[...])
    xo_ref[...] = x

    h2 = x * lax.rsqrt(jnp.mean(x * x, axis=-1, keepdims=True) + EPS) * n2_ref[...]
    h2 = h2 * (1.0 + mod_ref[4:5, :]) + mod_ref[3:4, :]
    for s in range(ROW_TILES):
        h2t_ref[pl.ds(s, TL, stride=ROW_TILES), :] = h2[:, s * 128:(s + 1) * 128]
    route_ref[...] = _route(h2, wr_ref[...], br_ref[...], cnt_ref)

    for h in range(N_HEADS):
        hgs_ref[h] = s_scr[h].T
    lrus_ref[...] = hl_scr[...]


def _forward_pass(l, x, mod, z, o_b, h_b, G, lev, lb, hg_norm, conv_a, cw, cb, wax, bax, lam,
                  ln_g, ln_b, sg_w, sg_bt, w_br, w_out, norm2, w_r, b_r, hg_init, lru_init):
    tile = lambda s: s
    gate = lambda k: pl.BlockSpec((TL, D), lambda s: (s, GATE_BLK + k))
    in_specs = [
        pl.BlockSpec((TL, D), lambda s: (s, 0)),
        pl.BlockSpec((None, None, 6, D), lambda s: (l, _seq_idx(s), 0, 0)),
        _zslab(S_A_IN, tile), _zslab(S_A_B, tile), _zslab(S_A_C, tile),
        _zprev(S_A_IN, tile), _znext(S_A_IN, tile), _zprev(S_A_C, tile), _znext(S_A_C, tile),
        _zslab(S_B_Q, tile), _zslab(S_B_FF, tile), _zslab(S_B_I, tile), _zslab(S_B_G, tile),
        _zslab(S_C_X, tile), _zprev(S_C_X, tile), _znext(S_C_X, tile), _zslab(S_C_Y, tile),
        _zslab(S_D_U, tile), _zslab(S_D_V, tile), gate(0), gate(1), gate(2), gate(3),
        pl.BlockSpec((TL, W), lambda s: (s, 0)), pl.BlockSpec((TL, W), lambda s: (s, 0)),
        _const(G.shape), _const(lev.shape),
        pl.BlockSpec((None, None, 1, W), lambda s: (0, l, 0, 0)),
        _layer_row(l, W),
        pl.BlockSpec((None, 3, W), lambda s: (l, 0, 0)),
        pl.BlockSpec((None, 4, W), lambda s: (l, 0, 0)),
        _layer_row(l, W),
        pl.BlockSpec((None, None, W, 2 * W), lambda s: (l, 0, 0, 0)),
        pl.BlockSpec((None, None, 1, 2 * W), lambda s: (l, 0, 0, 0)),
        pl.BlockSpec((None, None, 1, W), lambda s: (l, 0, 0, 0)),
        _layer_row(l, W), _layer_row(l, W),
        pl.BlockSpec((None, SG_GROUPS, SG_CHUNK, SG_CHUNK), lambda s: (l, 0, 0, 0)),
        pl.BlockSpec((None, SG_CHUNK, SG_GROUPS), lambda s: (l, 0, 0)),
        pl.BlockSpec((None, 4, W, D), lambda s: (l, 0, 0, 0)),
        pl.BlockSpec((None, D, D), lambda s: (l, 0, 0)),
        _layer_row(l, D),
        pl.BlockSpec((None, D, ROUTE_W), lambda s: (l, 0, 0)),
        _layer_row(l, ROUTE_W),
        pl.BlockSpec((None, None, None, N_HEADS, HD, HD), lambda s: (_seq_idx(s), l, 0, 0, 0, 0)),
        pl.BlockSpec((None, None, None, 1, W), lambda s: (_seq_idx(s), l, 0, 0, 0)),
    ]
    return pl.pallas_call(
        _fwd_kernel,
        grid=(NT,),
        in_specs=in_specs,
        out_specs=[pl.BlockSpec((TL, D), lambda s: (s, 0)),
                   pl.BlockSpec((TL * ROW_TILES, 128), lambda s: (s, 0)),
                   pl.BlockSpec((TL, ROUTE_W), lambda s: (s, 0)),
                   pl.BlockSpec((1, ROUTE_W), lambda s: (0, 0)),
                   pl.BlockSpec((None, N_HEADS, HD, HD), lambda s: (s, 0, 0, 0)),
                   pl.BlockSpec((None, 1, W), lambda s: (s, 0, 0))],
        out_shape=[jax.ShapeDtypeStruct((T, D), F32), jax.ShapeDtypeStruct((T * ROW_TILES, 128), F32),
                   jax.ShapeDtypeStruct((T, ROUTE_W), F32), jax.ShapeDtypeStruct((1, ROUTE_W), F32),
                   jax.ShapeDtypeStruct((NT, N_HEADS, HD, HD), F32),
                   jax.ShapeDtypeStruct((NT, 1, W), F32)],
        scratch_shapes=[pltpu.VMEM((N_HEADS, HD, HD), F32), pltpu.VMEM((1, W), F32)],
        compiler_params=pltpu.CompilerParams(dimension_semantics=("arbitrary",),
                                             vmem_limit_bytes=VMEM_LIMIT),
        name="forward_pass",
    )(x, mod, *([z] * 21), o_b, h_b, G, lev, lb, hg_norm, conv_a, cw, cb, wax, bax, lam,
      ln_g, ln_b, sg_w, sg_bt, w_br, w_out, norm2, w_r, b_r, hg_init, lru_init)


def _token_rows(ref, row, n=1, tiles=ROW_TILES):
    return ref.at[pl.ds(row * tiles, n * tiles), :]


def _read_rows(ref, n, tiles=ROW_TILES, first=0):
    return jnp.concatenate([ref[pl.ds(first + s, n, stride=tiles), :] for s in range(ROW_TILES)], axis=1)


def _write_rows(ref, y, tiles=ROW_TILES, first=0):
    for s in range(ROW_TILES):
        ref[pl.ds(first + s, y.shape[0], stride=tiles), :] = y[:, s * 128:(s + 1) * 128]


def _padded(n):
    return ((n + TME - 1) // TME) * TME


def _segment_starts(cnt_ref, seg_ref):
    seg = 0
    for c in range(N_CLASSES):
        seg_ref[c] = seg
        seg = seg + _padded(cnt_ref[c])
    return seg


def _token_slot(meta_ref, seg_ref, tok):
    return seg_ref[meta_ref[2 * tok]] + meta_ref[2 * tok + 1]


def _wait_rows(hbm, sem, n, tiles=ROW_TILES):
    pltpu.make_async_copy(_token_rows(hbm, 0, n, tiles), _token_rows(hbm, 0, n, tiles), sem).wait()


def _dispatch_kernel(meta_ref, cnt_ref, h2t_ref, xs_hbm, zero_buf, seg_ref, sem):
    s = pl.program_id(0)
    total = _segment_starts(cnt_ref, seg_ref)

    def start(i, c):
        row = _token_slot(meta_ref, seg_ref, s * TD + i)
        pltpu.make_async_copy(_token_rows(h2t_ref, i), _token_rows(xs_hbm, row), sem).start()
        return c

    lax.fori_loop(0, TD, start, 0, unroll=8)
    _wait_rows(xs_hbm, sem, TD)

    @pl.when(s == T // TD - 1)
    def _():
        zero_buf[...] = jnp.zeros(zero_buf.shape, F32)

        def pad_copy(j):
            return pltpu.make_async_copy(_token_rows(zero_buf, 0), _token_rows(xs_hbm, j), sem)

        def pad_start(j, c):
            pad_copy(j).start()
            return c

        def pad_wait(j, c):
            pad_copy(j).wait()
            return c

        def tail_copy(t):
            return pltpu.make_async_copy(zero_buf, _token_rows(xs_hbm, t * TME, TME), sem)

        def tail_start(t, c):
            tail_copy(t).start()
            return c

        def tail_wait(t, c):
            tail_copy(t).wait()
            return c

        for c in range(N_CLASSES):
            n = cnt_ref[c]
            lax.fori_loop(seg_ref[c] + n, seg_ref[c] + _padded(n), pad_start, 0)
            lax.fori_loop(seg_ref[c] + n, seg_ref[c] + _padded(n), pad_wait, 0)
        lax.fori_loop(total // TME, N_ETILES, tail_start, 0)
        lax.fori_loop(total // TME, N_ETILES, tail_wait, 0)


def _dispatch(meta, cnt, h2t):
    return pl.pallas_call(
        _dispatch_kernel,
        grid_spec=pltpu.PrefetchScalarGridSpec(
            num_scalar_prefetch=2, grid=(T // TD,),
            in_specs=[pl.BlockSpec((TD * ROW_TILES, 128), lambda s, meta, cnt: (s, 0))],
            out_specs=pl.BlockSpec(memory_space=pl.ANY),
            scratch_shapes=[pltpu.VMEM((TME * ROW_TILES, 128), F32), pltpu.SMEM((N_CLASSES,), jnp.int32),
                            pltpu.SemaphoreType.DMA]),
        out_shape=jax.ShapeDtypeStruct((N_ROWS * ROW_TILES, 128), F32),
        compiler_params=pltpu.CompilerParams(dimension_semantics=("arbitrary",),
                                             vmem_limit_bytes=VMEM_LIMIT),
        name="dispatch",
    )(meta, cnt, h2t)


def _expert_kernel(lo_ref, hi_ref, tv_ref, xs_ref, wg0, wu0, wd0, wg1, wu1, wd1, ys_ref,
                   sg0, su0, sd0, sg1, su1, sd1):
    i = pl.program_id(0)
    prev = jnp.maximum(i - 1, 0)

    @pl.when(tv_ref[i] == 0)
    def _():
        ys_ref[...] = jnp.zeros(ys_ref.shape, F32)

    @pl.when(tv_ref[i] == 1)
    def _():
        sets = ((lo_ref, (wg0, wu0, wd0), (sg0, su0, sd0)), (hi_ref, (wg1, wu1, wd1), (sg1, su1, sd1)))
        for e_ref, w_refs, s_refs in sets:
            @pl.when(jnp.logical_or(i == 0, e_ref[i] != e_ref[prev]))
            def _():
                for w, sc in zip(w_refs, s_refs):
                    sc[...] = w[...].astype(BF16)

        x = _read_rows(xs_ref, TME).astype(BF16)
        for k, (_, _, (sg, su, sd)) in enumerate(sets):
            a = _dot(x, sg[...])
            u = _dot(x, su[...])
            y = _dot((a * jax.nn.sigmoid(a) * u).astype(BF16), sd[...])
            _write_rows(ys_ref, y, 2 * ROW_TILES, k * ROW_TILES)


def _grouped_experts(l, e_lo, e_hi, tv, xs, w_gate, w_up, w_down):
    def wspec(shape, which):
        return pl.BlockSpec((None, None) + shape, lambda i, lo, hi, tv: (l, (lo, hi)[which][i], 0, 0))

    wspecs = [wspec(shape, which) for which in range(2)
              for shape in ((D, D_EXPERT), (D, D_EXPERT), (D_EXPERT, D))]
    return pl.pallas_call(
        _expert_kernel,
        grid_spec=pltpu.PrefetchScalarGridSpec(
            num_scalar_prefetch=3, grid=(N_ETILES,),
            in_specs=[pl.BlockSpec((TME * ROW_TILES, 128), lambda i, lo, hi, tv: (i, 0))] + wspecs,
            out_specs=pl.BlockSpec((TME * 2 * ROW_TILES, 128), lambda i, lo, hi, tv: (i, 0)),
            scratch_shapes=[pltpu.VMEM(shape, BF16) for _ in range(2)
                            for shape in ((D, D_EXPERT), (D, D_EXPERT), (D_EXPERT, D))]),
        out_shape=jax.ShapeDtypeStruct((N_ROWS * 2 * ROW_TILES, 128), F32),
        compiler_params=pltpu.CompilerParams(dimension_semantics=("arbitrary",),
                                             vmem_limit_bytes=VMEM_LIMIT),
        name="grouped_experts",
    )(e_lo, e_hi, tv, xs, w_gate, w_up, w_down, w_gate, w_up, w_down)


def _combine_kernel(meta_ref, cnt_ref, x_ref, mod_ref, route_ref, ys_hbm, o_ref, buf, seg_ref, sem):
    s = pl.program_id(0)
    _segment_starts(cnt_ref, seg_ref)

    def start(i, c):
        row = _token_slot(meta_ref, seg_ref, s * TD + i)
        pltpu.make_async_copy(_token_rows(ys_hbm, row, 1, 2 * ROW_TILES), _token_rows(buf, i, 1, 2 * ROW_TILES),
                              sem).start()
        return c

    lax.fori_loop(0, TD, start, 0, unroll=8)
    _wait_rows(ys_hbm, sem, TD, 2 * ROW_TILES)
    route = route_ref[...]
    y = (route[:, 2:3] * _read_rows(buf, TD, 2 * ROW_TILES, 0)
         + route[:, 3:4] * _read_rows(buf, TD, 2 * ROW_TILES, ROW_TILES))
    o_ref[...] = x_ref[...] + mod_ref[5:6, :] * y


def _combine(l, meta, cnt, x, mod, route, ys):
    return pl.pallas_call(
        _combine_kernel,
        grid_spec=pltpu.PrefetchScalarGridSpec(
            num_scalar_prefetch=2, grid=(T // TD,),
            in_specs=[pl.BlockSpec((TD, D), lambda s, meta, cnt: (s, 0)),
                      pl.BlockSpec((None, None, 6, D), lambda s, meta, cnt: (l, _seq_idx(s * (TD // TL)), 0, 0)),
                      pl.BlockSpec((TD, ROUTE_W), lambda s, meta, cnt: (s, 0)),
                      pl.BlockSpec(memory_space=pl.ANY)],
            out_specs=pl.BlockSpec((TD, D), lambda s, meta, cnt: (s, 0)),
            scratch_shapes=[pltpu.VMEM((TD * 2 * ROW_TILES, 128), F32), pltpu.SMEM((N_CLASSES,), jnp.int32),
                            pltpu.SemaphoreType.DMA]),
        out_shape=jax.ShapeDtypeStruct((T, D), F32),
        compiler_params=pltpu.CompilerParams(dimension_semantics=("arbitrary",),
                                             vmem_limit_bytes=VMEM_LIMIT),
        name="combine",
    )(meta, cnt, x, mod, route, ys)


def _expert_pass(l, x, mod, h2t, route, cnt, w_gate, w_up, w_down):
    meta = route[:, 0:2].astype(jnp.int32).reshape(-1)
    cnt = cnt[0, :N_CLASSES].astype(jnp.int32)
    ends = jnp.cumsum((cnt + TME - 1) // TME)
    i = jnp.arange(N_ETILES, dtype=jnp.int32)
    tv = (i < ends[-1]).astype(jnp.int32)
    tc = jnp.sum(ends[None, :] <= jnp.minimum(i, ends[-1] - 1)[:, None], axis=1, dtype=jnp.int32)
    grp, pair = tc // PAIRS, tc % PAIRS
    e_lo = grp * EXP_PER_GROUP + jnp.asarray(PAIR_LO, jnp.int32)[pair]
    e_hi = grp * EXP_PER_GROUP + jnp.asarray(PAIR_HI, jnp.int32)[pair]
    xs = _dispatch(meta, cnt, h2t)
    ys = _grouped_experts(l, e_lo, e_hi, tv, xs, w_gate, w_up, w_down)
    return _combine(l, meta, cnt, x, mod, route, ys)


def _final_kernel(x_ref, g_ref, o_ref):
    x = x_ref[...]
    o_ref[...] = x * lax.rsqrt(jnp.mean(x * x, axis=-1, keepdims=True) + EPS) * g_ref[...]


def _final_norm(x, g, first_row, rows):
    tm = 1024
    return pl.pallas_call(
        _final_kernel,
        grid=(rows // tm,),
        in_specs=[pl.BlockSpec((tm, D), lambda i: (first_row // tm + i, 0)), pl.BlockSpec((1, D), lambda i: (0, 0))],
        out_specs=pl.BlockSpec((tm, D), lambda i: (i, 0)),
        out_shape=jax.ShapeDtypeStruct((rows, D), F32),
        compiler_params=pltpu.CompilerParams(dimension_semantics=("arbitrary",)),
        name="final_norm",
    )(x, g.reshape(1, D))


def _grid_pos_embed(rows, d):
    nf = d // 4
    freqs = jnp.exp(-math.log(10000.0) * jnp.arange(nf, dtype=F32) / nf)
    r = jnp.arange(rows, dtype=F32)[:, None] * freqs
    cl = jnp.arange(GRID_W, dtype=F32)[:, None] * freqs
    r_emb = jnp.concatenate([jnp.sin(r), jnp.cos(r)], axis=-1)
    c_emb = jnp.concatenate([jnp.sin(cl), jnp.cos(cl)], axis=-1)
    emb = jnp.concatenate([jnp.broadcast_to(r_emb[:, None], (rows, GRID_W, d // 2)),
                           jnp.broadcast_to(c_emb[None], (rows, GRID_W, d // 2))], axis=-1)
    return emb.reshape(rows * GRID_W, d)


def _block_diag(w):
    n, c = w.shape[-3], w.shape[-1]
    eye = jnp.eye(n, dtype=w.dtype)
    full = w[..., :, :, None, :] * eye[:, None, :, None]
    return full.reshape(*w.shape[:-3], n * c, n * c)


def kernel(x_prompt, x_sample, state_hgrn, state_lru, c, c_ctx, w_ada, b_ada, norm1, norm2, w_in, conv_a,
           hg_lb, hg_norm, lru_conv_w, lru_conv_b, lru_wa, lru_ba, lru_wx, lru_bx, lru_lam, sg_ln_g, sg_ln_b,
           sg_w, sg_b, w_br, w_out, w_rg, b_rg, w_re, b_re, w_gate, w_up, w_down, norm_f):
    assert x_prompt.shape == (N_CTX, L_CTX, D) and x_sample.shape == (N_LAT, L_LAT, D)

    xs = x_sample.astype(F32) + _grid_pos_embed(L_LAT // GRID_W, D)
    x = jnp.concatenate([x_prompt.astype(F32).reshape(-1, D), xs.reshape(-1, D)], axis=0)

    cond = jnp.concatenate([c_ctx.astype(F32)[None], c.astype(F32), jnp.zeros((8 - 1 - N_LAT, D), F32)], axis=0)
    mod = _modulation(cond, w_ada, b_ada).reshape(DEPTH, 8, 6, D)

    cs = jnp.cumsum(jax.nn.softmax(hg_lb.astype(F32), axis=1), axis=1)
    lb = (cs - cs[:, :1]).reshape(2, DEPTH, 1, W)
    wax = jnp.concatenate([_block_diag(lru_wa), _block_diag(lru_wx)], axis=-1).astype(BF16)
    bax = jnp.concatenate([lru_ba, lru_bx], axis=-1).reshape(DEPTH, 2, 1, 2 * W)
    lam = lru_lam.reshape(DEPTH, 2, 1, W)
    row = lambda a: a.reshape(DEPTH, 1, a.shape[-1])
    sg_bt = jnp.swapaxes(sg_b, 1, 2)
    rpad = ROUTE_W - N_GROUPS - N_EXPERTS
    w_r = jnp.concatenate([w_rg, w_re, jnp.zeros((DEPTH, D, rpad), F32)], axis=-1)
    b_r = jnp.concatenate([b_rg, b_re, jnp.zeros((DEPTH, rpad), F32)], axis=-1)
    w_br_b = w_br.astype(BF16)
    w_out_b = w_out.astype(BF16)
    sg_w_b = sg_w.astype(BF16)

    consts = [_scan_constants(rev) for rev in (False, True)]
    G_f, G_b = (jnp.asarray(g, BF16) for g, _ in consts)
    lev_f, lev_b = (jnp.asarray(v) for _, v in consts)

    hg_init = jnp.concatenate([jnp.zeros((1,) + state_hgrn.shape[1:], F32), state_hgrn.astype(F32)], axis=0)
    lru_init = jnp.concatenate([jnp.zeros((1,) + state_lru.shape[1:], F32), state_lru.astype(F32)], axis=0)
    lru_init = lru_init.reshape(1 + N_LAT, DEPTH, 2, 1, W)

    hg_states, lru_states = [], []
    for l in range(DEPTH):
        z = _in_projection(l, x, mod, row(norm1), w_in)
        o_b, h_b, hgs_b, lrus_b = _reverse_pass(l, z, G_b, lev_b, lb, lru_conv_w, row(lru_conv_b), wax, bax, lam,
                                                hg_init, lru_init)
        x, h2t, route, cnt, hgs_f, lrus_f = _forward_pass(
            l, x, mod, z, o_b, h_b, G_f, lev_f, lb, row(hg_norm), conv_a, lru_conv_w, row(lru_conv_b), wax, bax,
            lam, row(sg_ln_g), row(sg_ln_b), sg_w_b, sg_bt, w_br_b, w_out_b, row(norm2), w_r, row(b_r),
            hg_init, lru_init)
        x = _expert_pass(l, x, mod, h2t, route, cnt, w_gate, w_up, w_down)
        hg_states.append(jnp.stack([hgs_f[:CTX_TILES], hgs_b[:CTX_TILES]], axis=1))
        lru_states.append(jnp.stack([lrus_f[:CTX_TILES, 0], lrus_b[:CTX_TILES, 0]], axis=1))

    n_ctx = N_CTX * L_CTX
    y_prompt = _final_norm(x, norm_f, 0, n_ctx).reshape(N_CTX, L_CTX, D)
    y_sample = _final_norm(x, norm_f, n_ctx, T - n_ctx).reshape(N_LAT, L_LAT, D)
    return y_prompt, y_sample, jnp.stack(hg_states, axis=1), jnp.stack(lru_states, axis=1)
```

```python
import functools
import math

import numpy as np
import jax
import jax.numpy as jnp
from jax import lax
from jax.experimental import pallas as pl
from jax.experimental.pallas import tpu as pltpu

F32 = jnp.float32
BF16 = jnp.bfloat16

D = 1024
W = 512
DEPTH = 4
GRID_W = 64
N_HEADS = 4
HD = 128
LRU_BLOCKS = 8
LRU_BLOCK = W // LRU_BLOCKS
LRU_C = 8.0
SG_CHUNK = 128
SG_GROUPS = 4
N_GROUPS = 4
EXP_PER_GROUP = 4
N_EXPERTS = 16
D_EXPERT = 512
EPS = 1e-6
LOG2E = 1.4426950408889634
F32_TINY = 2.0 ** -126
D_IN = 12 * W + 4 * D

S_A_IN, S_A_B, S_A_C, S_B_Q, S_B_FF, S_B_FB, S_B_I, S_B_G, S_C_X, S_C_Y, S_D_U, S_D_V = range(12)
GATE_BLK = (12 * W) // D

N_CTX, L_CTX = 16, 256
N_LAT, L_LAT = 2, 2048
T = N_CTX * L_CTX + N_LAT * L_LAT
TL = 256
NT = T // TL
CTX_TILES = (N_CTX * L_CTX) // TL
LAT_TILES = L_LAT // TL
HC = 128
NCH = TL // HC
LOG_HC = 7
N_LEVELS = LOG_HC
N_MXU_LEVELS = 3
SUB = 8
HALO = 16
ROUTE_W = 128
EXP_LANE0 = N_GROUPS

VMEM_LIMIT = 56 * 1024 * 1024

TM_IN, TN_IN = 2048, 1024
TN_MOD = 1536
ROW_TILES = D // 128
TME = 256
TD = 1024
PAIRS = EXP_PER_GROUP * (EXP_PER_GROUP - 1) // 2
N_CLASSES = N_GROUPS * PAIRS
N_ETILES = T // TME + N_CLASSES
N_ROWS = N_ETILES * TME
PAIR_LO = [a for a in range(EXP_PER_GROUP) for b in range(a + 1, EXP_PER_GROUP)]
PAIR_HI = [b for a in range(EXP_PER_GROUP) for b in range(a + 1, EXP_PER_GROUP)]


def _scan_constants(reverse):
    i = np.arange(TL)[:, None]
    s = np.arange(TL)[None, :]
    mats = [(i // HC == s // HC) & (s <= i)]
    for B in (8, 4, 2):
        m = (i // B) * B + B // 2 - 1
        mats.append(np.where(i > m, (s > m) & (s <= i), (s > i) & (s <= m)))
    if reverse:
        mats = [g[::-1, ::-1] for g in mats]
    G = np.concatenate(mats, 0).astype(np.float32)
    r = np.arange(HC)[:, None]
    c = np.arange(HC)[None, :]
    lev = (LOG_HC - 1) - np.floor(np.log2(np.maximum(r ^ c, 1))).astype(np.int32)
    lev = np.where(r == c, N_LEVELS, lev)
    causal = (r < c) if reverse else (r > c)
    lev = np.where(causal | (r == c), lev, N_LEVELS + 1).astype(np.int32)
    return G, lev


def _split2(x):
    hi = x.astype(BF16)
    lo = (x - hi.astype(F32)).astype(BF16)
    return hi, lo


def _dot(a, b):
    return jnp.dot(a, b, preferred_element_type=F32)


def _dot_nt(a, b):
    return lax.dot_general(a, b, (((1,), (1,)), ((), ())), preferred_element_type=F32)


def _dot_split(a, b):
    ah, al = _split2(a)
    bh, bl = _split2(b)
    return _dot(ah, bh) + (_dot(ah, bl) + _dot(al, bh))


def _log_sigmoid(x):
    return jnp.minimum(x, 0.0) - jnp.log1p(jnp.exp(-jnp.abs(x)))


def _softplus(x):
    return jnp.maximum(x, 0.0) + jnp.log1p(jnp.exp(-jnp.abs(x)))


def _tile_flags(tt):
    is_lat = tt >= CTX_TILES
    pos = tt & (LAT_TILES - 1)
    first = jnp.logical_or(jnp.logical_not(is_lat), pos == 0)
    last = jnp.logical_or(jnp.logical_not(is_lat), pos == LAT_TILES - 1)
    return first, last


def _seq_idx(t):
    return jnp.where(t < CTX_TILES, 0, 1 + (t - CTX_TILES) // LAT_TILES)


def _shifted(x, prev, nxt, k):
    if k == 0:
        return x
    rows = lax.broadcasted_iota(jnp.int32, (SUB, x.shape[1]), 0)
    if k < 0:
        y = pltpu.roll(x, -k, 0)
        edge = y[:SUB]
        for r in range(-k):
            edge = jnp.where(rows == r, prev[HALO + k + r:HALO + k + r + 1, :], edge)
        return jnp.concatenate([edge, y[SUB:]], axis=0)
    y = pltpu.roll(x, TL - k, 0)
    edge = y[TL - SUB:]
    for r in range(k):
        edge = jnp.where(rows == SUB - k + r, nxt[r:r + 1, :], edge)
    return jnp.concatenate([y[:TL - SUB], edge], axis=0)


def _block_anchor(b, B, a):
    return jnp.concatenate([jnp.broadcast_to(b[s0 + a:s0 + a + 1, :], (B, b.shape[1]))
                            for s0 in range(0, TL, B)], axis=0)


def _hgrn_tile(q, zf, v, lb, G_ref, lev, s_ref, reverse):
    f = lb + (1.0 - lb) * jax.nn.sigmoid(zf)
    kk = 1.0 - f

    lf = jnp.log(jnp.maximum(f, F32_TINY)) * LOG2E
    hi, mid = _split2(lf)
    g = G_ref[...]
    sums = _dot(g, hi) + _dot(g, mid)
    b = sums[0:TL]
    e_lev = []
    row = lax.broadcasted_iota(jnp.int32, (TL, HD), 0)
    for l in range(N_LEVELS - N_MXU_LEVELS):
        B = HC >> l
        anchor = _block_anchor(b, B, B // 2 if reverse else B // 2 - 1)
        later = ((row & (B // 2)) == 0) if reverse else ((row & (B // 2)) != 0)
        sign = jnp.concatenate([jnp.where(later, 1.0, -1.0)] * N_HEADS, axis=1)
        e_lev.append(jnp.exp2((b - anchor) * sign).astype(BF16))
    for l in range(N_MXU_LEVELS):
        e_lev.append(jnp.exp2(sums[(1 + l) * TL:(2 + l) * TL]).astype(BF16))
    end = 0 if reverse else HC - 1
    e_b = jnp.exp2(b)
    e_e = jnp.exp2(_block_anchor(b, HC, end) - b)
    masks = [lev == l for l in range(N_LEVELS)]
    q_b = q.astype(BF16)
    k_b = kk.astype(BF16)

    outs = []
    for h in range(N_HEADS):
        sl = slice(h * HD, (h + 1) * HD)
        st = s_ref[h]
        o_h = [None] * NCH
        for c in (reversed(range(NCH)) if reverse else range(NCH)):
            rs = slice(c * HC, (c + 1) * HC)
            q_c, k_c, v_c = q[rs, sl], kk[rs, sl], v[rs, sl]
            d = jnp.zeros((HC, HC), F32)
            for l in range(N_LEVELS):
                e = e_lev[l][rs, sl]
                p = _dot_nt(q_b[rs, sl] * e, k_b[rs, sl] * e)
                d = jnp.where(masks[l], p, d)
            o = _dot(d.astype(BF16), v_c.astype(BF16))
            o = o + jnp.sum(q_c * k_c, axis=-1, keepdims=True) * v_c
            o = o + _dot_nt((q_c * e_b[rs, sl]).astype(BF16), st.astype(BF16))
            upd = _dot(v_c.T.astype(BF16), (k_c * e_e[rs, sl]).astype(BF16))
            st = e_b[c * HC + end:c * HC + end + 1, sl] * st + upd
            o_h[c] = o
        s_ref[h] = st
        outs.append(jnp.concatenate(o_h, axis=0))
    return jnp.concatenate(outs, axis=-1)


def _lin_scan(a, u, carry, reverse):
    sub = lax.broadcasted_iota(jnp.int32, a.shape, 0) & (SUB - 1)
    d = 1
    while d < SUB:
        if reverse:
            a_s, u_s, ok = pltpu.roll(a, TL - d, 0), pltpu.roll(u, TL - d, 0), sub < SUB - d
        else:
            a_s, u_s, ok = pltpu.roll(a, d, 0), pltpu.roll(u, d, 0), sub >= d
        u = jnp.where(ok, a * u_s + u, u)
        a = jnp.where(ok, a * a_s, a)
        d *= 2
    n = TL // SUB
    blocks = [None] * n
    for g in (reversed(range(n)) if reverse else range(n)):
        blk = u[g * SUB:(g + 1) * SUB] + a[g * SUB:(g + 1) * SUB] * carry
        carry = blk[0:1] if reverse else blk[SUB - 1:SUB]
        blocks[g] = blk
    return jnp.concatenate(blocks, axis=0), carry


def _lru_tile(zx, zx_p, zx_n, cw, cb, wax, bax, lam, hl_ref, reverse):
    xc = cb + sum(cw[k:k + 1, :] * _shifted(zx, zx_p, zx_n, k - 2) for k in range(4))
    rx = _dot(xc.astype(BF16), wax) + bax
    r = jax.nn.sigmoid(rx[:, :W])
    ig = jax.nn.sigmoid(rx[:, W:])
    log_a = -LRU_C * r * _softplus(-lam)
    a = jnp.exp(log_a)
    u = jnp.sqrt(-jnp.tanh(log_a) * (1.0 + a * a)) * (ig * xc)
    hseq, carry = _lin_scan(a, u, hl_ref[...], reverse)
    hl_ref[...] = carry
    return hseq


def _mod_kernel(c_ref, w_ref, b_ref, o_ref):
    c = c_ref[...]
    act = c * jax.nn.sigmoid(c)
    o_ref[...] = _dot_split(act, w_ref[...]) + b_ref[...]


def _modulation(cond, w_ada, b_ada):
    n = 6 * D // TN_MOD
    return pl.pallas_call(
        _mod_kernel,
        grid=(DEPTH, n),
        in_specs=[pl.BlockSpec((8, D), lambda l, j: (0, 0)),
                  pl.BlockSpec((None, D, TN_MOD), lambda l, j: (l, 0, j)),
                  pl.BlockSpec((None, 1, TN_MOD), lambda l, j: (l, 0, j))],
        out_specs=pl.BlockSpec((None, 8, TN_MOD), lambda l, j: (l, 0, j)),
        out_shape=jax.ShapeDtypeStruct((DEPTH, 8, 6 * D), F32),
        compiler_params=pltpu.CompilerParams(dimension_semantics=("arbitrary", "arbitrary"),
                                             vmem_limit_bytes=VMEM_LIMIT),
        name="modulation",
    )(cond, w_ada, b_ada.reshape(DEPTH, 1, 6 * D))


def _in_kernel(x_ref, mod_ref, g_ref, w_ref, z_ref, h_scr):
    @pl.when(pl.program_id(1) == 0)
    def _():
        x = x_ref[...]
        h = x * lax.rsqrt(jnp.mean(x * x, axis=-1, keepdims=True) + EPS) * g_ref[...]
        h = h * (1.0 + mod_ref[1:2, :]) + mod_ref[0:1, :]
        h_scr[...] = h.astype(BF16)

    z_ref[...] = _dot(h_scr[...], w_ref[...].astype(BF16)).astype(BF16)


def _in_projection(l, x, mod, norm1, w_in):
    tiles_per_seq = L_LAT // TM_IN
    ctx = (N_CTX * L_CTX) // TM_IN

    def seq(i):
        return jnp.where(i < ctx, 0, 1 + (i - ctx) // tiles_per_seq)

    return pl.pallas_call(
        _in_kernel,
        grid=(T // TM_IN, D_IN // TN_IN),
        in_specs=[pl.BlockSpec((TM_IN, D), lambda i, j: (i, 0)),
                  pl.BlockSpec((None, None, 6, D), lambda i, j: (l, seq(i), 0, 0)),
                  pl.BlockSpec((None, 1, D), lambda i, j: (l, 0, 0)),
                  pl.BlockSpec((None, D, TN_IN), lambda i, j: (l, 0, j))],
        out_specs=pl.BlockSpec((TM_IN, TN_IN), lambda i, j: (i, j)),
        out_shape=jax.ShapeDtypeStruct((T, D_IN), BF16),
        scratch_shapes=[pltpu.VMEM((TM_IN, D), BF16)],
        compiler_params=pltpu.CompilerParams(dimension_semantics=("arbitrary", "arbitrary"),
                                             vmem_limit_bytes=VMEM_LIMIT),
        name="in_projection",
    )(x, mod, norm1, w_in)


def _bwd_kernel(zq, zf, zi, zx, zx_p, zx_n, G_ref, lev_ref, lb_ref, cw_ref, cb_ref, wax_ref, bax_ref,
                lam_ref, hg0_ref, lru0_ref,
                o_ref, h_ref, hgs_ref, lrus_ref, s_scr, hl_scr):
    tt = NT - 1 - pl.program_id(0)
    first, last = _tile_flags(tt)

    @pl.when(last)
    def _():
        for h in range(N_HEADS):
            s_scr[h] = hg0_ref[h].T
        hl_scr[...] = lru0_ref[...]

    has_prev = jnp.where(first, 0.0, 1.0)
    has_next = jnp.where(last, 0.0, 1.0)
    zqv = zq[...].astype(F32)
    q = zqv * jax.nn.sigmoid(zqv)
    o_ref[...] = _hgrn_tile(q, zf[...].astype(F32), zi[...].astype(F32), lb_ref[...], G_ref, lev_ref[...],
                            s_scr, True)
    h_ref[...] = _lru_tile(zx[...].astype(F32), zx_p[...].astype(F32) * has_prev,
                           zx_n[...].astype(F32) * has_next, cw_ref[...], cb_ref[...], wax_ref[...],
                           bax_ref[...], lam_ref[...], hl_scr, True)
    for h in range(N_HEADS):
        hgs_ref[h] = s_scr[h].T
    lrus_ref[...] = hl_scr[...]


def _zslab(k, tile):
    return pl.BlockSpec((TL, W), lambda s: (tile(s), k))


def _zprev(k, tile):
    return pl.BlockSpec((HALO, W), lambda s: (jnp.maximum(tile(s) * (TL // HALO) - 1, 0), k))


def _znext(k, tile):
    return pl.BlockSpec((HALO, W), lambda s: (jnp.minimum((tile(s) + 1) * (TL // HALO), T // HALO - 1), k))


def _const(shape):
    nd = len(shape)
    return pl.BlockSpec(shape, lambda s: (0,) * nd)


def _layer_row(l, n):
    return pl.BlockSpec((None, 1, n), lambda s: (l, 0, 0))


def _reverse_pass(l, z, G, lev, lb, cw, cb, wax, bax, lam, hg_init, lru_init):
    tile = lambda s: NT - 1 - s
    state_specs = [
        pl.BlockSpec((None, None, None, N_HEADS, HD, HD), lambda s: (_seq_idx(tile(s)), l, 1, 0, 0, 0)),
        pl.BlockSpec((None, None, None, 1, W), lambda s: (_seq_idx(tile(s)), l, 1, 0, 0)),
    ]
    return pl.pallas_call(
        _bwd_kernel,
        grid=(NT,),
        in_specs=[_zslab(S_B_Q, tile), _zslab(S_B_FB, tile), _zslab(S_B_I, tile), _zslab(S_C_X, tile),
                  _zprev(S_C_X, tile), _znext(S_C_X, tile),
                  _const(G.shape), _const(lev.shape),
                  pl.BlockSpec((None, None, 1, W), lambda s: (1, l, 0, 0)),
                  pl.BlockSpec((None, 4, W), lambda s: (l, 0, 0)),
                  _layer_row(l, W),
                  pl.BlockSpec((None, None, W, 2 * W), lambda s: (l, 1, 0, 0)),
                  pl.BlockSpec((None, None, 1, 2 * W), lambda s: (l, 1, 0, 0)),
                  pl.BlockSpec((None, None, 1, W), lambda s: (l, 1, 0, 0)),
                  ] + state_specs,
        out_specs=[pl.BlockSpec((TL, W), lambda s: (tile(s), 0)),
                   pl.BlockSpec((TL, W), lambda s: (tile(s), 0)),
                   pl.BlockSpec((None, N_HEADS, HD, HD), lambda s: (tile(s), 0, 0, 0)),
                   pl.BlockSpec((None, 1, W), lambda s: (tile(s), 0, 0))],
        out_shape=[jax.ShapeDtypeStruct((T, W), F32), jax.ShapeDtypeStruct((T, W), F32),
                   jax.ShapeDtypeStruct((NT, N_HEADS, HD, HD), F32),
                   jax.ShapeDtypeStruct((NT, 1, W), F32)],
        scratch_shapes=[pltpu.VMEM((N_HEADS, HD, HD), F32), pltpu.VMEM((1, W), F32)],
        compiler_params=pltpu.CompilerParams(dimension_semantics=("arbitrary",),
                                             vmem_limit_bytes=VMEM_LIMIT),
        name="reverse_pass",
    )(z, z, z, z, z, z, G, lev, lb, cw, cb, wax, bax, lam, hg_init, lru_init)


def _route(h2, wr, br, cnt_ref):
    logits = _dot_split(h2, wr) + br
    lane = lax.broadcasted_iota(jnp.int32, logits.shape, 1)
    lane_f = lane.astype(F32)
    neg = jnp.float32(-jnp.inf)
    is_g = lane < N_GROUPS
    gl = jnp.where(is_g, logits, neg)
    gmax = jnp.max(gl, axis=-1, keepdims=True)
    g_sel = jnp.min(jnp.where(gl == gmax, lane_f, float(ROUTE_W)), axis=-1, keepdims=True)
    g_p = 1.0 / jnp.sum(jnp.where(is_g, jnp.exp(gl - gmax), 0.0), axis=-1, keepdims=True)
    e_grp = ((lane - EXP_LANE0) >> 2).astype(F32)
    in_grp = (lane >= EXP_LANE0) & (lane < EXP_LANE0 + N_EXPERTS) & (e_grp == g_sel)
    el = jnp.where(in_grp, logits, neg)
    m1 = jnp.max(el, axis=-1, keepdims=True)
    i1 = jnp.min(jnp.where(el == m1, lane_f, float(ROUTE_W)), axis=-1, keepdims=True)
    el2 = jnp.where(lane_f == i1, neg, el)
    m2 = jnp.max(el2, axis=-1, keepdims=True)
    i2 = jnp.min(jnp.where(el2 == m2, lane_f, float(ROUTE_W)), axis=-1, keepdims=True)
    t = jnp.exp(m2 - m1)
    w1 = g_p / (1.0 + t)
    w2 = w1 * t

    first_is_lo = i1 < i2
    lo = jnp.minimum(i1, i2) - EXP_LANE0 - EXP_PER_GROUP * g_sel
    hi = jnp.maximum(i1, i2) - EXP_LANE0 - EXP_PER_GROUP * g_sel
    cls = g_sel * PAIRS + lo * (2 * EXP_PER_GROUP - 1 - lo) * 0.5 + (hi - lo - 1.0)
    oh = (lane_f == cls).astype(F32)
    r = lax.broadcasted_iota(jnp.int32, (TL, TL), 0)
    cc = lax.broadcasted_iota(jnp.int32, (TL, TL), 1)
    before = _dot(jnp.where(r > cc, 1.0, 0.0).astype(BF16), oh.astype(BF16)) + cnt_ref[...]
    rank = jnp.sum(oh * before, axis=-1, keepdims=True)
    cnt_ref[...] += jnp.sum(oh, axis=0, keepdims=True)
    cols = (cls, rank, jnp.where(first_is_lo, w1, w2), jnp.where(first_is_lo, w2, w1))
    out = jnp.zeros(logits.shape, F32)
    for k, col in enumerate(cols):
        out = jnp.where(lane == k, col, out)
    return out


def _fwd_kernel(x_ref, mod_ref,
                za_in, za_b, za_c, za_in_p, za_in_n, za_c_p, za_c_n,
                zq, zf, zi, zg, zx, zx_p, zx_n, zy, zu, zv, gt0, gt1, gt2, gt3,
                ob_ref, hb_ref, G_ref, lev_ref,
                lb_ref, hgn_ref, ca_ref, cw_ref, cb_ref, wax_ref, bax_ref, lam_ref,
                lng_ref, lnb_ref, sgw_ref, sgb_ref, wbr_ref, wout_ref, n2_ref, wr_ref, br_ref,
                hg0_ref, lru0_ref,
                xo_ref, h2t_ref, route_ref, cnt_ref, hgs_ref, lrus_ref,
                s_scr, hl_scr):
    tt = pl.program_id(0)
    first, last = _tile_flags(tt)

    @pl.when(tt == 0)
    def _():
        cnt_ref[...] = jnp.zeros(cnt_ref.shape, F32)

    @pl.when(first)
    def _():
        for h in range(N_HEADS):
            s_scr[h] = hg0_ref[h].T
        hl_scr[...] = lru0_ref[...]

    has_prev = jnp.where(first, 0.0, 1.0)
    has_next = jnp.where(last, 0.0, 1.0)
    f32 = lambda r: r[...].astype(F32)

    p = f32(za_c) * f32(za_in)
    p_p = f32(za_c_p) * f32(za_in_p) * has_prev
    p_n = f32(za_c_n) * f32(za_in_n) * has_next
    ca = ca_ref[...]
    ya = f32(za_b) * sum(ca[k:k + 1, :] * _shifted(p, p_p, p_n, k - 1) for k in range(3))

    zqv = f32(zq)
    q = zqv * jax.nn.sigmoid(zqv)
    o = _hgrn_tile(q, f32(zf), f32(zi), lb_ref[...], G_ref, lev_ref[...], s_scr, False) + ob_ref[...]
    zgv = f32(zg)
    gate_b = hgn_ref[...] * (zgv * jax.nn.sigmoid(zgv))
    yb = jnp.concatenate(
        [o[:, h * HD:(h + 1) * HD]
         * lax.rsqrt(jnp.mean(jnp.square(o[:, h * HD:(h + 1) * HD]), axis=-1, keepdims=True) + EPS)
         for h in range(N_HEADS)], axis=-1) * gate_b

    hf = _lru_tile(f32(zx), f32(zx_p) * has_prev, f32(zx_n) * has_next, cw_ref[...], cb_ref[...],
                   wax_ref[...], bax_ref[...], lam_ref[...], hl_scr, False)
    yc = (hf + hb_ref[...]) * jax.nn.gelu(f32(zy))

    u = jax.nn.gelu(f32(zu))
    v = jax.nn.gelu(f32(zv))
    mu = jnp.mean(v, axis=-1, keepdims=True)
    vc = v - mu
    var = jnp.mean(vc * vc, axis=-1, keepdims=True)
    vn = (vc * lax.rsqrt(var + EPS) * lng_ref[...] + lnb_ref[...]).astype(BF16)
    sgb = sgb_ref[...]
    gw = W // SG_GROUPS
    chunks = []
    for n in range(TL // SG_CHUNK):
        cols = []
        for g in range(SG_GROUPS):
            blk = vn[n * SG_CHUNK:(n + 1) * SG_CHUNK, g * gw:(g + 1) * gw]
            cols.append(_dot(sgw_ref[g], blk) + sgb[:, g:g + 1])
        chunks.append(jnp.concatenate(cols, axis=-1))
    yd = u * jnp.concatenate(chunks, axis=0)

    mix = jnp.zeros((TL, D), F32)
    for k, (y, gt) in enumerate(((ya, gt0), (yb, gt1), (yc, gt2), (yd, gt3))):
        mix = mix + jax.nn.sigmoid(f32(gt)) * _dot(y.astype(BF16), wbr_ref[k])
    x = x_ref[...] + mod_ref[2:3, :] * _dot(mix.astype(BF16), wout_ref[...])
    xo_ref[...] = x

    h2 = x * lax.rsqrt(jnp.mean(x * x, axis=-1, keepdims=True) + EPS) * n2_ref[...]
    h2 = h2 * (1.0 + mod_ref[4:5, :]) + mod_ref[3:4, :]
    for s in range(ROW_TILES):
        h2t_ref[pl.ds(s, TL, stride=ROW_TILES), :] = h2[:, s * 128:(s + 1) * 128]
    route_ref[...] = _route(h2, wr_ref[...], br_ref[...], cnt_ref)

    for h in range(N_HEADS):
        hgs_ref[h] = s_scr[h].T
    lrus_ref[...] = hl_scr[...]


def _forward_pass(l, x, mod, z, o_b, h_b, G, lev, lb, hg_norm, conv_a, cw, cb, wax, bax, lam,
                  ln_g, ln_b, sg_w, sg_bt, w_br, w_out, norm2, w_r, b_r, hg_init, lru_init):
    tile = lambda s: s
    gate = lambda k: pl.BlockSpec((TL, D), lambda s: (s, GATE_BLK + k))
    in_specs = [
        pl.BlockSpec((TL, D), lambda s: (s, 0)),
        pl.BlockSpec((None, None, 6, D), lambda s: (l, _seq_idx(s), 0, 0)),
        _zslab(S_A_IN, tile), _zslab(S_A_B, tile), _zslab(S_A_C, tile),
        _zprev(S_A_IN, tile), _znext(S_A_IN, tile), _zprev(S_A_C, tile), _znext(S_A_C, tile),
        _zslab(S_B_Q, tile), _zslab(S_B_FF, tile), _zslab(S_B_I, tile), _zslab(S_B_G, tile),
        _zslab(S_C_X, tile), _zprev(S_C_X, tile), _znext(S_C_X, tile), _zslab(S_C_Y, tile),
        _zslab(S_D_U, tile), _zslab(S_D_V, tile), gate(0), gate(1), gate(2), gate(3),
        pl.BlockSpec((TL, W), lambda s: (s, 0)), pl.BlockSpec((TL, W), lambda s: (s, 0)),
        _const(G.shape), _const(lev.shape),
        pl.BlockSpec((None, None, 1, W), lambda s: (0, l, 0, 0)),
        _layer_row(l, W),
        pl.BlockSpec((None, 3, W), lambda s: (l, 0, 0)),
        pl.BlockSpec((None, 4, W), lambda s: (l, 0, 0)),
        _layer_row(l, W),
        pl.BlockSpec((None, None, W, 2 * W), lambda s: (l, 0, 0, 0)),
        pl.BlockSpec((None, None, 1, 2 * W), lambda s: (l, 0, 0, 0)),
        pl.BlockSpec((None, None, 1, W), lambda s: (l, 0, 0, 0)),
        _layer_row(l, W), _layer_row(l, W),
        pl.BlockSpec((None, SG_GROUPS, SG_CHUNK, SG_CHUNK), lambda s: (l, 0, 0, 0)),
        pl.BlockSpec((None, SG_CHUNK, SG_GROUPS), lambda s: (l, 0, 0)),
        pl.BlockSpec((None, 4, W, D), lambda s: (l, 0, 0, 0)),
        pl.BlockSpec((None, D, D), lambda s: (l, 0, 0)),
        _layer_row(l, D),
        pl.BlockSpec((None, D, ROUTE_W), lambda s: (l, 0, 0)),
        _layer_row(l, ROUTE_W),
        pl.BlockSpec((None, None, None, N_HEADS, HD, HD), lambda s: (_seq_idx(s), l, 0, 0, 0, 0)),
        pl.BlockSpec((None, None, None, 1, W), lambda s: (_seq_idx(s), l, 0, 0, 0)),
    ]
    return pl.pallas_call(
        _fwd_kernel,
        grid=(NT,),
        in_specs=in_specs,
        out_specs=[pl.BlockSpec((TL, D), lambda s: (s, 0)),
                   pl.BlockSpec((TL * ROW_TILES, 128), lambda s: (s, 0)),
                   pl.BlockSpec((TL, ROUTE_W), lambda s: (s, 0)),
                   pl.BlockSpec((1, ROUTE_W), lambda s: (0, 0)),
                   pl.BlockSpec((None, N_HEADS, HD, HD), lambda s: (s, 0, 0, 0)),
                   pl.BlockSpec((None, 1, W), lambda s: (s, 0, 0))],
        out_shape=[jax.ShapeDtypeStruct((T, D), F32), jax.ShapeDtypeStruct((T * ROW_TILES, 128), F32),
                   jax.ShapeDtypeStruct((T, ROUTE_W), F32), jax.ShapeDtypeStruct((1, ROUTE_W), F32),
                   jax.ShapeDtypeStruct((NT, N_HEADS, HD, HD), F32),
                   jax.ShapeDtypeStruct((NT, 1, W), F32)],
        scratch_shapes=[pltpu.VMEM((N_HEADS, HD, HD), F32), pltpu.VMEM((1, W), F32)],
        compiler_params=pltpu.CompilerParams(dimension_semantics=("arbitrary",),
                                             vmem_limit_bytes=VMEM_LIMIT),
        name="forward_pass",
    )(x, mod, *([z] * 21), o_b, h_b, G, lev, lb, hg_norm, conv_a, cw, cb, wax, bax, lam,
      ln_g, ln_b, sg_w, sg_bt, w_br, w_out, norm2, w_r, b_r, hg_init, lru_init)


def _token_rows(ref, row, n=1, tiles=ROW_TILES):
    return ref.at[pl.ds(row * tiles, n * tiles), :]


def _read_rows(ref, n, tiles=ROW_TILES, first=0):
    return jnp.concatenate([ref[pl.ds(first + s, n, stride=tiles), :] for s in range(ROW_TILES)], axis=1)


def _write_rows(ref, y, tiles=ROW_TILES, first=0):
    for s in range(ROW_TILES):
        ref[pl.ds(first + s, y.shape[0], stride=tiles), :] = y[:, s * 128:(s + 1) * 128]


def _padded(n):
    return ((n + TME - 1) // TME) * TME


def _segment_starts(cnt_ref, seg_ref):
    seg = 0
    for c in range(N_CLASSES):
        seg_ref[c] = seg
        seg = seg + _padded(cnt_ref[c])
    return seg


def _token_slot(meta_ref, seg_ref, tok):
    return seg_ref[meta_ref[2 * tok]] + meta_ref[2 * tok + 1]


def _wait_rows(hbm, sem, n, tiles=ROW_TILES):
    pltpu.make_async_copy(_token_rows(hbm, 0, n, tiles), _token_rows(hbm, 0, n, tiles), sem).wait()


def _dispatch_kernel(meta_ref, cnt_ref, h2t_ref, xs_hbm, zero_buf, seg_ref, sem):
    s = pl.program_id(0)
    total = _segment_starts(cnt_ref, seg_ref)

    def start(i, c):
        row = _token_slot(meta_ref, seg_ref, s * TD + i)
        pltpu.make_async_copy(_token_rows(h2t_ref, i), _token_rows(xs_hbm, row), sem).start()
        return c

    lax.fori_loop(0, TD, start, 0, unroll=8)
    _wait_rows(xs_hbm, sem, TD)

    @pl.when(s == T // TD - 1)
    def _():
        zero_buf[...] = jnp.zeros(zero_buf.shape, F32)

        def pad_copy(j):
            return pltpu.make_async_copy(_token_rows(zero_buf, 0), _token_rows(xs_hbm, j), sem)

        def pad_start(j, c):
            pad_copy(j).start()
            return c

        def pad_wait(j, c):
            pad_copy(j).wait()
            return c

        def tail_copy(t):
            return pltpu.make_async_copy(zero_buf, _token_rows(xs_hbm, t * TME, TME), sem)

        def tail_start(t, c):
            tail_copy(t).start()
            return c

        def tail_wait(t, c):
            tail_copy(t).wait()
            return c

        for c in range(N_CLASSES):
            n = cnt_ref[c]
            lax.fori_loop(seg_ref[c] + n, seg_ref[c] + _padded(n), pad_start, 0)
            lax.fori_loop(seg_ref[c] + n, seg_ref[c] + _padded(n), pad_wait, 0)
        lax.fori_loop(total // TME, N_ETILES, tail_start, 0)
        lax.fori_loop(total // TME, N_ETILES, tail_wait, 0)


def _dispatch(meta, cnt, h2t):
    return pl.pallas_call(
        _dispatch_kernel,
        grid_spec=pltpu.PrefetchScalarGridSpec(
            num_scalar_prefetch=2, grid=(T // TD,),
            in_specs=[pl.BlockSpec((TD * ROW_TILES, 128), lambda s, meta, cnt: (s, 0))],
            out_specs=pl.BlockSpec(memory_space=pl.ANY),
            scratch_shapes=[pltpu.VMEM((TME * ROW_TILES, 128), F32), pltpu.SMEM((N_CLASSES,), jnp.int32),
                            pltpu.SemaphoreType.DMA]),
        out_shape=jax.ShapeDtypeStruct((N_ROWS * ROW_TILES, 128), F32),
        compiler_params=pltpu.CompilerParams(dimension_semantics=("arbitrary",),
                                             vmem_limit_bytes=VMEM_LIMIT),
        name="dispatch",
    )(meta, cnt, h2t)


def _expert_kernel(lo_ref, hi_ref, tv_ref, xs_ref, wg0, wu0, wd0, wg1, wu1, wd1, ys_ref,
                   sg0, su0, sd0, sg1, su1, sd1):
    i = pl.program_id(0)
    prev = jnp.maximum(i - 1, 0)

    @pl.when(tv_ref[i] == 0)
    def _():
        ys_ref[...] = jnp.zeros(ys_ref.shape, F32)

    @pl.when(tv_ref[i] == 1)
    def _():
        sets = ((lo_ref, (wg0, wu0, wd0), (sg0, su0, sd0)), (hi_ref, (wg1, wu1, wd1), (sg1, su1, sd1)))
        for e_ref, w_refs, s_refs in sets:
            @pl.when(jnp.logical_or(i == 0, e_ref[i] != e_ref[prev]))
            def _():
                for w, sc in zip(w_refs, s_refs):
                    sc[...] = w[...].astype(BF16)

        x = _read_rows(xs_ref, TME).astype(BF16)
        for k, (_, _, (sg, su, sd)) in enumerate(sets):
            a = _dot(x, sg[...])
            u = _dot(x, su[...])
            y = _dot((a * jax.nn.sigmoid(a) * u).astype(BF16), sd[...])
            _write_rows(ys_ref, y, 2 * ROW_TILES, k * ROW_TILES)


def _grouped_experts(l, e_lo, e_hi, tv, xs, w_gate, w_up, w_down):
    def wspec(shape, which):
        return pl.BlockSpec((None, None) + shape, lambda i, lo, hi, tv: (l, (lo, hi)[which][i], 0, 0))

    wspecs = [wspec(shape, which) for which in range(2)
              for shape in ((D, D_EXPERT), (D, D_EXPERT), (D_EXPERT, D))]
    return pl.pallas_call(
        _expert_kernel,
        grid_spec=pltpu.PrefetchScalarGridSpec(
            num_scalar_prefetch=3, grid=(N_ETILES,),
            in_specs=[pl.BlockSpec((TME * ROW_TILES, 128), lambda i, lo, hi, tv: (i, 0))] + wspecs,
            out_specs=pl.BlockSpec((TME * 2 * ROW_TILES, 128), lambda i, lo, hi, tv: (i, 0)),
            scratch_shapes=[pltpu.VMEM(shape, BF16) for _ in range(2)
                            for shape in ((D, D_EXPERT), (D, D_EXPERT), (D_EXPERT, D))]),
        out_shape=jax.ShapeDtypeStruct((N_ROWS * 2 * ROW_TILES, 128), F32),
        compiler_params=pltpu.CompilerParams(dimension_semantics=("arbitrary",),
                                             vmem_limit_bytes=VMEM_LIMIT),
        name="grouped_experts",
    )(e_lo, e_hi, tv, xs, w_gate, w_up, w_down, w_gate, w_up, w_down)


def _combine_kernel(meta_ref, cnt_ref, x_ref, mod_ref, route_ref, ys_hbm, o_ref, buf, seg_ref, sem):
    s = pl.program_id(0)
    _segment_starts(cnt_ref, seg_ref)

    def start(i, c):
        row = _token_slot(meta_ref, seg_ref, s * TD + i)
        pltpu.make_async_copy(_token_rows(ys_hbm, row, 1, 2 * ROW_TILES), _token_rows(buf, i, 1, 2 * ROW_TILES),
                              sem).start()
        return c

    lax.fori_loop(0, TD, start, 0, unroll=8)
    _wait_rows(ys_hbm, sem, TD, 2 * ROW_TILES)
    route = route_ref[...]
    y = (route[:, 2:3] * _read_rows(buf, TD, 2 * ROW_TILES, 0)
         + route[:, 3:4] * _read_rows(buf, TD, 2 * ROW_TILES, ROW_TILES))
    o_ref[...] = x_ref[...] + mod_ref[5:6, :] * y


def _combine(l, meta, cnt, x, mod, route, ys):
    return pl.pallas_call(
        _combine_kernel,
        grid_spec=pltpu.PrefetchScalarGridSpec(
            num_scalar_prefetch=2, grid=(T // TD,),
            in_specs=[pl.BlockSpec((TD, D), lambda s, meta, cnt: (s, 0)),
                      pl.BlockSpec((None, None, 6, D), lambda s, meta, cnt: (l, _seq_idx(s * (TD // TL)), 0, 0)),
                      pl.BlockSpec((TD, ROUTE_W), lambda s, meta, cnt: (s, 0)),
                      pl.BlockSpec(memory_space=pl.ANY)],
            out_specs=pl.BlockSpec((TD, D), lambda s, meta, cnt: (s, 0)),
            scratch_shapes=[pltpu.VMEM((TD * 2 * ROW_TILES, 128), F32), pltpu.SMEM((N_CLASSES,), jnp.int32),
                            pltpu.SemaphoreType.DMA]),
        out_shape=jax.ShapeDtypeStruct((T, D), F32),
        compiler_params=pltpu.CompilerParams(dimension_semantics=("arbitrary",),
                                             vmem_limit_bytes=VMEM_LIMIT),
        name="combine",
    )(meta, cnt, x, mod, route, ys)


def _expert_pass(l, x, mod, h2t, route, cnt, w_gate, w_up, w_down):
    meta = route[:, 0:2].astype(jnp.int32).reshape(-1)
    cnt = cnt[0, :N_CLASSES].astype(jnp.int32)
    ends = jnp.cumsum((cnt + TME - 1) // TME)
    i = jnp.arange(N_ETILES, dtype=jnp.int32)
    tv = (i < ends[-1]).astype(jnp.int32)
    tc = jnp.sum(ends[None, :] <= jnp.minimum(i, ends[-1] - 1)[:, None], axis=1, dtype=jnp.int32)
    grp, pair = tc // PAIRS, tc % PAIRS
    e_lo = grp * EXP_PER_GROUP + jnp.asarray(PAIR_LO, jnp.int32)[pair]
    e_hi = grp * EXP_PER_GROUP + jnp.asarray(PAIR_HI, jnp.int32)[pair]
    xs = _dispatch(meta, cnt, h2t)
    ys = _grouped_experts(l, e_lo, e_hi, tv, xs, w_gate, w_up, w_down)
    return _combine(l, meta, cnt, x, mod, route, ys)


def _final_kernel(x_ref, g_ref, o_ref):
    x = x_ref[...]
    o_ref[...] = x * lax.rsqrt(jnp.mean(x * x, axis=-1, keepdims=True) + EPS) * g_ref[...]


def _final_norm(x, g, first_row, rows):
    tm = 1024
    return pl.pallas_call(
        _final_kernel,
        grid=(rows // tm,),
        in_specs=[pl.BlockSpec((tm, D), lambda i: (first_row // tm + i, 0)), pl.BlockSpec((1, D), lambda i: (0, 0))],
        out_specs=pl.BlockSpec((tm, D), lambda i: (i, 0)),
        out_shape=jax.ShapeDtypeStruct((rows, D), F32),
        compiler_params=pltpu.CompilerParams(dimension_semantics=("arbitrary",)),
        name="final_norm",
    )(x, g.reshape(1, D))


def _grid_pos_embed(rows, d):
    nf = d // 4
    freqs = jnp.exp(-math.log(10000.0) * jnp.arange(nf, dtype=F32) / nf)
    r = jnp.arange(rows, dtype=F32)[:, None] * freqs
    cl = jnp.arange(GRID_W, dtype=F32)[:, None] * freqs
    r_emb = jnp.concatenate([jnp.sin(r), jnp.cos(r)], axis=-1)
    c_emb = jnp.concatenate([jnp.sin(cl), jnp.cos(cl)], axis=-1)
    emb = jnp.concatenate([jnp.broadcast_to(r_emb[:, None], (rows, GRID_W, d // 2)),
                           jnp.broadcast_to(c_emb[None], (rows, GRID_W, d // 2))], axis=-1)
    return emb.reshape(rows * GRID_W, d)


def _block_diag(w):
    n, c = w.shape[-3], w.shape[-1]
    eye = jnp.eye(n, dtype=w.dtype)
    full = w[..., :, :, None, :] * eye[:, None, :, None]
    return full.reshape(*w.shape[:-3], n * c, n * c)


def kernel(x_prompt, x_sample, state_hgrn, state_lru, c, c_ctx, w_ada, b_ada, norm1, norm2, w_in, conv_a,
           hg_lb, hg_norm, lru_conv_w, lru_conv_b, lru_wa, lru_ba, lru_wx, lru_bx, lru_lam, sg_ln_g, sg_ln_b,
           sg_w, sg_b, w_br, w_out, w_rg, b_rg, w_re, b_re, w_gate, w_up, w_down, norm_f):
    assert x_prompt.shape == (N_CTX, L_CTX, D) and x_sample.shape == (N_LAT, L_LAT, D)

    xs = x_sample.astype(F32) + _grid_pos_embed(L_LAT // GRID_W, D)
    x = jnp.concatenate([x_prompt.astype(F32).reshape(-1, D), xs.reshape(-1, D)], axis=0)

    cond = jnp.concatenate([c_ctx.astype(F32)[None], c.astype(F32), jnp.zeros((8 - 1 - N_LAT, D), F32)], axis=0)
    mod = _modulation(cond, w_ada, b_ada).reshape(DEPTH, 8, 6, D)

    cs = jnp.cumsum(jax.nn.softmax(hg_lb.astype(F32), axis=1), axis=1)
    lb = (cs - cs[:, :1]).reshape(2, DEPTH, 1, W)
    wax = jnp.concatenate([_block_diag(lru_wa.astype(BF16)), _block_diag(lru_wx.astype(BF16))], axis=-1)
    bax = jnp.concatenate([lru_ba, lru_bx], axis=-1).reshape(DEPTH, 2, 1, 2 * W)
    lam = lru_lam.reshape(DEPTH, 2, 1, W)
    row = lambda a: a.reshape(DEPTH, 1, a.shape[-1])
    sg_bt = jnp.swapaxes(sg_b, 1, 2)
    rpad = ROUTE_W - N_GROUPS - N_EXPERTS
    w_r = jnp.concatenate([w_rg, w_re, jnp.zeros((DEPTH, D, rpad), F32)], axis=-1)
    b_r = jnp.concatenate([b_rg, b_re, jnp.zeros((DEPTH, rpad), F32)], axis=-1)
    w_br_b = w_br.astype(BF16)
    w_out_b = w_out.astype(BF16)
    sg_w_b = sg_w.astype(BF16)

    consts = [_scan_constants(rev) for rev in (False, True)]
    G_f, G_b = (jnp.asarray(g, BF16) for g, _ in consts)
    lev_f, lev_b = (jnp.asarray(v) for _, v in consts)

    hg_init = jnp.concatenate([jnp.zeros((1,) + state_hgrn.shape[1:], F32), state_hgrn.astype(F32)], axis=0)
    lru_init = jnp.concatenate([jnp.zeros((1,) + state_lru.shape[1:], F32), state_lru.astype(F32)], axis=0)
    lru_init = lru_init.reshape(1 + N_LAT, DEPTH, 2, 1, W)

    hg_states, lru_states = [], []
    for l in range(DEPTH):
        z = _in_projection(l, x, mod, row(norm1), w_in)
        o_b, h_b, hgs_b, lrus_b = _reverse_pass(l, z, G_b, lev_b, lb, lru_conv_w, row(lru_conv_b), wax, bax, lam,
                                                hg_init, lru_init)
        x, h2t, route, cnt, hgs_f, lrus_f = _forward_pass(
            l, x, mod, z, o_b, h_b, G_f, lev_f, lb, row(hg_norm), conv_a, lru_conv_w, row(lru_conv_b), wax, bax,
            lam, row(sg_ln_g), row(sg_ln_b), sg_w_b, sg_bt, w_br_b, w_out_b, row(norm2), w_r, row(b_r),
            hg_init, lru_init)
        x = _expert_pass(l, x, mod, h2t, route, cnt, w_gate, w_up, w_down)
        hg_states.append(jnp.stack([hgs_f[:CTX_TILES], hgs_b[:CTX_TILES]], axis=1))
        lru_states.append(jnp.stack([lrus_f[:CTX_TILES, 0], lrus_b[:CTX_TILES, 0]], axis=1))

    n_ctx = N_CTX * L_CTX
    y_prompt = _final_norm(x, norm_f, 0, n_ctx).reshape(N_CTX, L_CTX, D)
    y_sample = _final_norm(x, norm_f, n_ctx, T - n_ctx).reshape(N_LAT, L_LAT, D)
    return y_prompt, y_sample, jnp.stack(hg_states, axis=1), jnp.stack(lru_states, axis=1)
```

```python
import functools
import math

import numpy as np
import jax
import jax.numpy as jnp
from jax import lax
from jax.experimental import pallas as pl
from jax.experimental.pallas import tpu as pltpu

F32 = jnp.float32
BF16 = jnp.bfloat16

D = 1024
W = 512
DEPTH = 4
GRID_W = 64
N_HEADS = 4
HD = 128
LRU_BLOCKS = 8
LRU_BLOCK = W // LRU_BLOCKS
LRU_C = 8.0
SG_CHUNK = 128
SG_GROUPS = 4
N_GROUPS = 4
EXP_PER_GROUP = 4
N_EXPERTS = 16
D_EXPERT = 512
EPS = 1e-6
LOG2E = 1.4426950408889634
F32_TINY = 2.0 ** -126
D_IN = 12 * W + 4 * D

S_A_IN, S_A_B, S_A_C, S_B_Q, S_B_FF, S_B_FB, S_B_I, S_B_G, S_C_X, S_C_Y, S_D_U, S_D_V = range(12)
GATE_BLK = (12 * W) // D

N_CTX, L_CTX = 16, 256
N_LAT, L_LAT = 2, 2048
T = N_CTX * L_CTX + N_LAT * L_LAT
TL = 256
NT = T // TL
CTX_TILES = (N_CTX * L_CTX) // TL
LAT_TILES = L_LAT // TL
HC = 128
NCH = TL // HC
LOG_HC = 7
N_LEVELS = LOG_HC
N_MXU_LEVELS = 3
SUB = 8
HALO = 16
ROUTE_W = 128
EXP_LANE0 = N_GROUPS

VMEM_LIMIT = 56 * 1024 * 1024

TM_IN, TN_IN = 2048, 1024
TN_MOD = 1536
ROW_TILES = D // 128
TME = 256
TD = 1024
PAIRS = EXP_PER_GROUP * (EXP_PER_GROUP - 1) // 2
N_CLASSES = N_GROUPS * PAIRS
N_ETILES = T // TME + N_CLASSES
N_ROWS = N_ETILES * TME
PAIR_LO = [a for a in range(EXP_PER_GROUP) for b in range(a + 1, EXP_PER_GROUP)]
PAIR_HI = [b for a in range(EXP_PER_GROUP) for b in range(a + 1, EXP_PER_GROUP)]


def _scan_constants(reverse):
    i = np.arange(TL)[:, None]
    s = np.arange(TL)[None, :]
    mats = [(i // HC == s // HC) & (s <= i)]
    for B in (8, 4, 2):
        m = (i // B) * B + B // 2 - 1
        mats.append(np.where(i > m, (s > m) & (s <= i), (s > i) & (s <= m)))
    if reverse:
        mats = [g[::-1, ::-1] for g in mats]
    G = np.concatenate(mats, 0).astype(np.float32)
    r = np.arange(HC)[:, None]
    c = np.arange(HC)[None, :]
    lev = (LOG_HC - 1) - np.floor(np.log2(np.maximum(r ^ c, 1))).astype(np.int32)
    lev = np.where(r == c, N_LEVELS, lev)
    causal = (r < c) if reverse else (r > c)
    lev = np.where(causal | (r == c), lev, N_LEVELS + 1).astype(np.int32)
    return G, lev


def _split2(x):
    hi = x.astype(BF16)
    lo = (x - hi.astype(F32)).astype(BF16)
    return hi, lo


def _dot(a, b):
    return jnp.dot(a, b, preferred_element_type=F32)


def _dot_nt(a, b):
    return lax.dot_general(a, b, (((1,), (1,)), ((), ())), preferred_element_type=F32)


def _dot_split(a, b):
    ah, al = _split2(a)
    bh, bl = _split2(b)
    return _dot(ah, bh) + (_dot(ah, bl) + _dot(al, bh))


def _log_sigmoid(x):
    return jnp.minimum(x, 0.0) - jnp.log1p(jnp.exp(-jnp.abs(x)))


def _softplus(x):
    return jnp.maximum(x, 0.0) + jnp.log1p(jnp.exp(-jnp.abs(x)))


def _tile_flags(tt):
    is_lat = tt >= CTX_TILES
    pos = tt & (LAT_TILES - 1)
    first = jnp.logical_or(jnp.logical_not(is_lat), pos == 0)
    last = jnp.logical_or(jnp.logical_not(is_lat), pos == LAT_TILES - 1)
    return first, last


def _seq_idx(t):
    return jnp.where(t < CTX_TILES, 0, 1 + (t - CTX_TILES) // LAT_TILES)


def _shifted(x, prev, nxt, k):
    if k == 0:
        return x
    rows = lax.broadcasted_iota(jnp.int32, (SUB, x.shape[1]), 0)
    if k < 0:
        y = pltpu.roll(x, -k, 0)
        edge = y[:SUB]
        for r in range(-k):
            edge = jnp.where(rows == r, prev[HALO + k + r:HALO + k + r + 1, :], edge)
        return jnp.concatenate([edge, y[SUB:]], axis=0)
    y = pltpu.roll(x, TL - k, 0)
    edge = y[TL - SUB:]
    for r in range(k):
        edge = jnp.where(rows == SUB - k + r, nxt[r:r + 1, :], edge)
    return jnp.concatenate([y[:TL - SUB], edge], axis=0)


def _block_anchor(b, B, a):
    return jnp.concatenate([jnp.broadcast_to(b[s0 + a:s0 + a + 1, :], (B, b.shape[1]))
                            for s0 in range(0, TL, B)], axis=0)


def _hgrn_tile(q, zf, v, lb, G_ref, lev, s_ref, reverse):
    f = lb + (1.0 - lb) * jax.nn.sigmoid(zf)
    kk = 1.0 - f

    lf = jnp.log(jnp.maximum(f, F32_TINY)) * LOG2E
    hi, mid = _split2(lf)
    g = G_ref[...]
    sums = _dot(g, hi) + _dot(g, mid)
    b = sums[0:TL]
    e_lev = []
    row = lax.broadcasted_iota(jnp.int32, (TL, HD), 0)
    for l in range(N_LEVELS - N_MXU_LEVELS):
        B = HC >> l
        anchor = _block_anchor(b, B, B // 2 if reverse else B // 2 - 1)
        later = ((row & (B // 2)) == 0) if reverse else ((row & (B // 2)) != 0)
        sign = jnp.concatenate([jnp.where(later, 1.0, -1.0)] * N_HEADS, axis=1)
        e_lev.append(jnp.exp2((b - anchor) * sign).astype(BF16))
    for l in range(N_MXU_LEVELS):
        e_lev.append(jnp.exp2(sums[(1 + l) * TL:(2 + l) * TL]).astype(BF16))
    end = 0 if reverse else HC - 1
    e_b = jnp.exp2(b)
    e_e = jnp.exp2(_block_anchor(b, HC, end) - b)
    masks = [lev == l for l in range(N_LEVELS)]
    q_b = q.astype(BF16)
    k_b = kk.astype(BF16)

    outs = []
    for h in range(N_HEADS):
        sl = slice(h * HD, (h + 1) * HD)
        st = s_ref[h]
        o_h = [None] * NCH
        for c in (reversed(range(NCH)) if reverse else range(NCH)):
            rs = slice(c * HC, (c + 1) * HC)
            q_c, k_c, v_c = q[rs, sl], kk[rs, sl], v[rs, sl]
            d = jnp.zeros((HC, HC), F32)
            for l in range(N_LEVELS):
                e = e_lev[l][rs, sl]
                p = _dot_nt(q_b[rs, sl] * e, k_b[rs, sl] * e)
                d = jnp.where(masks[l], p, d)
            o = _dot(d.astype(BF16), v_c.astype(BF16))
            o = o + jnp.sum(q_c * k_c, axis=-1, keepdims=True) * v_c
            o = o + _dot_nt((q_c * e_b[rs, sl]).astype(BF16), st.astype(BF16))
            upd = _dot(v_c.T.astype(BF16), (k_c * e_e[rs, sl]).astype(BF16))
            st = e_b[c * HC + end:c * HC + end + 1, sl] * st + upd
            o_h[c] = o
        s_ref[h] = st
        outs.append(jnp.concatenate(o_h, axis=0))
    return jnp.concatenate(outs, axis=-1)


def _lin_scan(a, u, carry, reverse):
    n = TL // SUB
    width = a.shape[1]
    a = a.reshape(n, SUB, width)
    u = u.reshape(n, SUB, width)
    sub = lax.broadcasted_iota(jnp.int32, a.shape, 1)
    d = 1
    while d < SUB:
        if reverse:
            a_s, u_s, ok = pltpu.roll(a, SUB - d, 1), pltpu.roll(u, SUB - d, 1), sub < SUB - d
        else:
            a_s, u_s, ok = pltpu.roll(a, d, 1), pltpu.roll(u, d, 1), sub >= d
        u = jnp.where(ok, a * u_s + u, u)
        a = jnp.where(ok, a * a_s, a)
        d *= 2
    a = a.reshape(TL, width)
    u = u.reshape(TL, width)
    blocks = [None] * n
    for g in (reversed(range(n)) if reverse else range(n)):
        blk = u[g * SUB:(g + 1) * SUB] + a[g * SUB:(g + 1) * SUB] * carry
        carry = blk[0:1] if reverse else blk[SUB - 1:SUB]
        blocks[g] = blk
    return jnp.concatenate(blocks, axis=0), carry


def _lru_tile(zx, zx_p, zx_n, cw, cb, wax, bax, lam, hl_ref, reverse):
    xc = cb + sum(cw[k:k + 1, :] * _shifted(zx, zx_p, zx_n, k - 2) for k in range(4))
    rx = _dot(xc.astype(BF16), wax) + bax
    r = jax.nn.sigmoid(rx[:, :W])
    ig = jax.nn.sigmoid(rx[:, W:])
    log_a = -LRU_C * r * _softplus(-lam)
    a = jnp.exp(log_a)
    u = jnp.sqrt(-jnp.tanh(log_a) * (1.0 + a * a)) * (ig * xc)
    hseq, carry = _lin_scan(a, u, hl_ref[...], reverse)
    hl_ref[...] = carry
    return hseq


def _mod_kernel(c_ref, w_ref, b_ref, o_ref):
    c = c_ref[...]
    act = c * jax.nn.sigmoid(c)
    o_ref[...] = _dot_split(act, w_ref[...]) + b_ref[...]


def _modulation(cond, w_ada, b_ada):
    n = 6 * D // TN_MOD
    return pl.pallas_call(
        _mod_kernel,
        grid=(DEPTH, n),
        in_specs=[pl.BlockSpec((8, D), lambda l, j: (0, 0)),
                  pl.BlockSpec((None, D, TN_MOD), lambda l, j: (l, 0, j)),
                  pl.BlockSpec((None, 1, TN_MOD), lambda l, j: (l, 0, j))],
        out_specs=pl.BlockSpec((None, 8, TN_MOD), lambda l, j: (l, 0, j)),
        out_shape=jax.ShapeDtypeStruct((DEPTH, 8, 6 * D), F32),
        compiler_params=pltpu.CompilerParams(dimension_semantics=("arbitrary", "arbitrary"),
                                             vmem_limit_bytes=VMEM_LIMIT),
        name="modulation",
    )(cond, w_ada, b_ada.reshape(DEPTH, 1, 6 * D))


def _in_kernel(x_ref, mod_ref, g_ref, w_ref, z_ref, h_scr):
    @pl.when(pl.program_id(1) == 0)
    def _():
        x = x_ref[...]
        h = x * lax.rsqrt(jnp.mean(x * x, axis=-1, keepdims=True) + EPS) * g_ref[...]
        h = h * (1.0 + mod_ref[1:2, :]) + mod_ref[0:1, :]
        h_scr[...] = h.astype(BF16)

    z_ref[...] = _dot(h_scr[...], w_ref[...].astype(BF16)).astype(BF16)


def _in_projection(l, x, mod, norm1, w_in):
    tiles_per_seq = L_LAT // TM_IN
    ctx = (N_CTX * L_CTX) // TM_IN

    def seq(i):
        return jnp.where(i < ctx, 0, 1 + (i - ctx) // tiles_per_seq)

    return pl.pallas_call(
        _in_kernel,
        grid=(T // TM_IN, D_IN // TN_IN),
        in_specs=[pl.BlockSpec((TM_IN, D), lambda i, j: (i, 0)),
                  pl.BlockSpec((None, None, 6, D), lambda i, j: (l, seq(i), 0, 0)),
                  pl.BlockSpec((None, 1, D), lambda i, j: (l, 0, 0)),
                  pl.BlockSpec((None, D, TN_IN), lambda i, j: (l, 0, j))],
        out_specs=pl.BlockSpec((TM_IN, TN_IN), lambda i, j: (i, j)),
        out_shape=jax.ShapeDtypeStruct((T, D_IN), BF16),
        scratch_shapes=[pltpu.VMEM((TM_IN, D), BF16)],
        compiler_params=pltpu.CompilerParams(dimension_semantics=("arbitrary", "arbitrary"),
                                             vmem_limit_bytes=VMEM_LIMIT),
        name="in_projection",
    )(x, mod, norm1, w_in)


def _bwd_kernel(zq, zf, zi, zx, zx_p, zx_n, G_ref, lev_ref, lb_ref, cw_ref, cb_ref, wax_ref, bax_ref,
                lam_ref, hg0_ref, lru0_ref,
                o_ref, h_ref, hgs_ref, lrus_ref, s_scr, hl_scr):
    tt = NT - 1 - pl.program_id(0)
    first, last = _tile_flags(tt)

    @pl.when(last)
    def _():
        for h in range(N_HEADS):
            s_scr[h] = hg0_ref[h].T
        hl_scr[...] = lru0_ref[...]

    has_prev = jnp.where(first, 0.0, 1.0)
    has_next = jnp.where(last, 0.0, 1.0)
    zqv = zq[...].astype(F32)
    q = zqv * jax.nn.sigmoid(zqv)
    o_ref[...] = _hgrn_tile(q, zf[...].astype(F32), zi[...].astype(F32), lb_ref[...], G_ref, lev_ref[...],
                            s_scr, True)
    h_ref[...] = _lru_tile(zx[...].astype(F32), zx_p[...].astype(F32) * has_prev,
                           zx_n[...].astype(F32) * has_next, cw_ref[...], cb_ref[...], wax_ref[...],
                           bax_ref[...], lam_ref[...], hl_scr, True)
    for h in range(N_HEADS):
        hgs_ref[h] = s_scr[h].T
    lrus_ref[...] = hl_scr[...]


def _zslab(k, tile):
    return pl.BlockSpec((TL, W), lambda s: (tile(s), k))


def _zprev(k, tile):
    return pl.BlockSpec((HALO, W), lambda s: (jnp.maximum(tile(s) * (TL // HALO) - 1, 0), k))


def _znext(k, tile):
    return pl.BlockSpec((HALO, W), lambda s: (jnp.minimum((tile(s) + 1) * (TL // HALO), T // HALO - 1), k))


def _const(shape):
    nd = len(shape)
    return pl.BlockSpec(shape, lambda s: (0,) * nd)


def _layer_row(l, n):
    return pl.BlockSpec((None, 1, n), lambda s: (l, 0, 0))


def _reverse_pass(l, z, G, lev, lb, cw, cb, wax, bax, lam, hg_init, lru_init):
    tile = lambda s: NT - 1 - s
    state_specs = [
        pl.BlockSpec((None, None, None, N_HEADS, HD, HD), lambda s: (_seq_idx(tile(s)), l, 1, 0, 0, 0)),
        pl.BlockSpec((None, None, None, 1, W), lambda s: (_seq_idx(tile(s)), l, 1, 0, 0)),
    ]
    return pl.pallas_call(
        _bwd_kernel,
        grid=(NT,),
        in_specs=[_zslab(S_B_Q, tile), _zslab(S_B_FB, tile), _zslab(S_B_I, tile), _zslab(S_C_X, tile),
                  _zprev(S_C_X, tile), _znext(S_C_X, tile),
                  _const(G.shape), _const(lev.shape),
                  pl.BlockSpec((None, None, 1, W), lambda s: (1, l, 0, 0)),
                  pl.BlockSpec((None, 4, W), lambda s: (l, 0, 0)),
                  _layer_row(l, W),
                  pl.BlockSpec((None, None, W, 2 * W), lambda s: (l, 1, 0, 0)),
                  pl.BlockSpec((None, None, 1, 2 * W), lambda s: (l, 1, 0, 0)),
                  pl.BlockSpec((None, None, 1, W), lambda s: (l, 1, 0, 0)),
                  ] + state_specs,
        out_specs=[pl.BlockSpec((TL, W), lambda s: (tile(s), 0)),
                   pl.BlockSpec((TL, W), lambda s: (tile(s), 0)),
                   pl.BlockSpec((None, N_HEADS, HD, HD), lambda s: (tile(s), 0, 0, 0)),
                   pl.BlockSpec((None, 1, W), lambda s: (tile(s), 0, 0))],
        out_shape=[jax.ShapeDtypeStruct((T, W), F32), jax.ShapeDtypeStruct((T, W), F32),
                   jax.ShapeDtypeStruct((NT, N_HEADS, HD, HD), F32),
                   jax.ShapeDtypeStruct((NT, 1, W), F32)],
        scratch_shapes=[pltpu.VMEM((N_HEADS, HD, HD), F32), pltpu.VMEM((1, W), F32)],
        compiler_params=pltpu.CompilerParams(dimension_semantics=("arbitrary",),
                                             vmem_limit_bytes=VMEM_LIMIT),
        name="reverse_pass",
    )(z, z, z, z, z, z, G, lev, lb, cw, cb, wax, bax, lam, hg_init, lru_init)


def _route(h2, wr, br, cnt_ref):
    logits = _dot_split(h2, wr) + br
    lane = lax.broadcasted_iota(jnp.int32, logits.shape, 1)
    lane_f = lane.astype(F32)
    neg = jnp.float32(-jnp.inf)
    is_g = lane < N_GROUPS
    gl = jnp.where(is_g, logits, neg)
    gmax = jnp.max(gl, axis=-1, keepdims=True)
    g_sel = jnp.min(jnp.where(gl == gmax, lane_f, float(ROUTE_W)), axis=-1, keepdims=True)
    g_p = 1.0 / jnp.sum(jnp.where(is_g, jnp.exp(gl - gmax), 0.0), axis=-1, keepdims=True)
    e_grp = ((lane - EXP_LANE0) >> 2).astype(F32)
    in_grp = (lane >= EXP_LANE0) & (lane < EXP_LANE0 + N_EXPERTS) & (e_grp == g_sel)
    el = jnp.where(in_grp, logits, neg)
    m1 = jnp.max(el, axis=-1, keepdims=True)
    i1 = jnp.min(jnp.where(el == m1, lane_f, float(ROUTE_W)), axis=-1, keepdims=True)
    el2 = jnp.where(lane_f == i1, neg, el)
    m2 = jnp.max(el2, axis=-1, keepdims=True)
    i2 = jnp.min(jnp.where(el2 == m2, lane_f, float(ROUTE_W)), axis=-1, keepdims=True)
    t = jnp.exp(m2 - m1)
    w1 = g_p / (1.0 + t)
    w2 = w1 * t

    first_is_lo = i1 < i2
    lo = jnp.minimum(i1, i2) - EXP_LANE0 - EXP_PER_GROUP * g_sel
    hi = jnp.maximum(i1, i2) - EXP_LANE0 - EXP_PER_GROUP * g_sel
    cls = g_sel * PAIRS + lo * (2 * EXP_PER_GROUP - 1 - lo) * 0.5 + (hi - lo - 1.0)
    oh = (lane_f == cls).astype(F32)
    r = lax.broadcasted_iota(jnp.int32, (TL, TL), 0)
    cc = lax.broadcasted_iota(jnp.int32, (TL, TL), 1)
    before = _dot(jnp.where(r > cc, 1.0, 0.0).astype(BF16), oh.astype(BF16)) + cnt_ref[...]
    rank = jnp.sum(oh * before, axis=-1, keepdims=True)
    cnt_ref[...] += jnp.sum(oh, axis=0, keepdims=True)
    cols = (cls, rank, jnp.where(first_is_lo, w1, w2), jnp.where(first_is_lo, w2, w1))
    out = jnp.zeros(logits.shape, F32)
    for k, col in enumerate(cols):
        out = jnp.where(lane == k, col, out)
    return out


def _fwd_kernel(x_ref, mod_ref,
                za_in, za_b, za_c, za_in_p, za_in_n, za_c_p, za_c_n,
                zq, zf, zi, zg, zx, zx_p, zx_n, zy, zu, zv, gt0, gt1, gt2, gt3,
                ob_ref, hb_ref, G_ref, lev_ref,
                lb_ref, hgn_ref, ca_ref, cw_ref, cb_ref, wax_ref, bax_ref, lam_ref,
                lng_ref, lnb_ref, sgw_ref, sgb_ref, wbr_ref, wout_ref, n2_ref, wr_ref, br_ref,
                hg0_ref, lru0_ref,
                xo_ref, h2t_ref, route_ref, cnt_ref, hgs_ref, lrus_ref,
                s_scr, hl_scr):
    tt = pl.program_id(0)
    first, last = _tile_flags(tt)

    @pl.when(tt == 0)
    def _():
        cnt_ref[...] = jnp.zeros(cnt_ref.shape, F32)

    @pl.when(first)
    def _():
        for h in range(N_HEADS):
            s_scr[h] = hg0_ref[h].T
        hl_scr[...] = lru0_ref[...]

    has_prev = jnp.where(first, 0.0, 1.0)
    has_next = jnp.where(last, 0.0, 1.0)
    f32 = lambda r: r[...].astype(F32)

    p = f32(za_c) * f32(za_in)
    p_p = f32(za_c_p) * f32(za_in_p) * has_prev
    p_n = f32(za_c_n) * f32(za_in_n) * has_next
    ca = ca_ref[...]
    ya = f32(za_b) * sum(ca[k:k + 1, :] * _shifted(p, p_p, p_n, k - 1) for k in range(3))

    zqv = f32(zq)
    q = zqv * jax.nn.sigmoid(zqv)
    o = _hgrn_tile(q, f32(zf), f32(zi), lb_ref[...], G_ref, lev_ref[...], s_scr, False) + ob_ref[...]
    zgv = f32(zg)
    gate_b = hgn_ref[...] * (zgv * jax.nn.sigmoid(zgv))
    yb = jnp.concatenate(
        [o[:, h * HD:(h + 1) * HD]
         * lax.rsqrt(jnp.mean(jnp.square(o[:, h * HD:(h + 1) * HD]), axis=-1, keepdims=True) + EPS)
         for h in range(N_HEADS)], axis=-1) * gate_b

    hf = _lru_tile(f32(zx), f32(zx_p) * has_prev, f32(zx_n) * has_next, cw_ref[...], cb_ref[...],
                   wax_ref[...], bax_ref[...], lam_ref[...], hl_scr, False)
    yc = (hf + hb_ref[...]) * jax.nn.gelu(f32(zy))

    u = jax.nn.gelu(f32(zu))
    v = jax.nn.gelu(f32(zv))
    mu = jnp.mean(v, axis=-1, keepdims=True)
    vc = v - mu
    var = jnp.mean(vc * vc, axis=-1, keepdims=True)
    vn = (vc * lax.rsqrt(var + EPS) * lng_ref[...] + lnb_ref[...]).astype(BF16)
    sgb = sgb_ref[...]
    gw = W // SG_GROUPS
    chunks = []
    for n in range(TL // SG_CHUNK):
        cols = []
        for g in range(SG_GROUPS):
            blk = vn[n * SG_CHUNK:(n + 1) * SG_CHUNK, g * gw:(g + 1) * gw]
            cols.append(_dot(sgw_ref[g], blk) + sgb[:, g:g + 1])
        chunks.append(jnp.concatenate(cols, axis=-1))
    yd = u * jnp.concatenate(chunks, axis=0)

    mix = jnp.zeros((TL, D), F32)
    for k, (y, gt) in enumerate(((ya, gt0), (yb, gt1), (yc, gt2), (yd, gt3))):
        mix = mix + jax.nn.sigmoid(f32(gt)) * _dot(y.astype(BF16), wbr_ref[k])
    x = x_ref[...] + mod_ref[2:3, :] * _dot(mix.astype(BF16), wout_ref[...])
    xo_ref[...] = x

    h2 = x * lax.rsqrt(jnp.mean(x * x, axis=-1, keepdims=True) + EPS) * n2_ref[...]
    h2 = h2 * (1.0 + mod_ref[4:5, :]) + mod_ref[3:4, :]
    for s in range(ROW_TILES):
        h2t_ref[pl.ds(s, TL, stride=ROW_TILES), :] = h2[:, s * 128:(s + 1) * 128]
    route_ref[...] = _route(h2, wr_ref[...], br_ref[...], cnt_ref)

    for h in range(N_HEADS):
        hgs_ref[h] = s_scr[h].T
    lrus_ref[...] = hl_scr[...]


def _forward_pass(l, x, mod, z, o_b, h_b, G, lev, lb, hg_norm, conv_a, cw, cb, wax, bax, lam,
                  ln_g, ln_b, sg_w, sg_bt, w_br, w_out, norm2, w_r, b_r, hg_init, lru_init):
    tile = lambda s: s
    gate = lambda k: pl.BlockSpec((TL, D), lambda s: (s, GATE_BLK + k))
    in_specs = [
        pl.BlockSpec((TL, D), lambda s: (s, 0)),
        pl.BlockSpec((None, None, 6, D), lambda s: (l, _seq_idx(s), 0, 0)),
        _zslab(S_A_IN, tile), _zslab(S_A_B, tile), _zslab(S_A_C, tile),
        _zprev(S_A_IN, tile), _znext(S_A_IN, tile), _zprev(S_A_C, tile), _znext(S_A_C, tile),
        _zslab(S_B_Q, tile), _zslab(S_B_FF, tile), _zslab(S_B_I, tile), _zslab(S_B_G, tile),
        _zslab(S_C_X, tile), _zprev(S_C_X, tile), _znext(S_C_X, tile), _zslab(S_C_Y, tile),
        _zslab(S_D_U, tile), _zslab(S_D_V, tile), gate(0), gate(1), gate(2), gate(3),
        pl.BlockSpec((TL, W), lambda s: (s, 0)), pl.BlockSpec((TL, W), lambda s: (s, 0)),
        _const(G.shape), _const(lev.shape),
        pl.BlockSpec((None, None, 1, W), lambda s: (0, l, 0, 0)),
        _layer_row(l, W),
        pl.BlockSpec((None, 3, W), lambda s: (l, 0, 0)),
        pl.BlockSpec((None, 4, W), lambda s: (l, 0, 0)),
        _layer_row(l, W),
        pl.BlockSpec((None, None, W, 2 * W), lambda s: (l, 0, 0, 0)),
        pl.BlockSpec((None, None, 1, 2 * W), lambda s: (l, 0, 0, 0)),
        pl.BlockSpec((None, None, 1, W), lambda s: (l, 0, 0, 0)),
        _layer_row(l, W), _layer_row(l, W),
        pl.BlockSpec((None, SG_GROUPS, SG_CHUNK, SG_CHUNK), lambda s: (l, 0, 0, 0)),
        pl.BlockSpec((None, SG_CHUNK, SG_GROUPS), lambda s: (l, 0, 0)),
        pl.BlockSpec((None, 4, W, D), lambda s: (l, 0, 0, 0)),
        pl.BlockSpec((None, D, D), lambda s: (l, 0, 0)),
        _layer_row(l, D),
        pl.BlockSpec((None, D, ROUTE_W), lambda s: (l, 0, 0)),
        _layer_row(l, ROUTE_W),
        pl.BlockSpec((None, None, None, N_HEADS, HD, HD), lambda s: (_seq_idx(s), l, 0, 0, 0, 0)),
        pl.BlockSpec((None, None, None, 1, W), lambda s: (_seq_idx(s), l, 0, 0, 0)),
    ]
    return pl.pallas_call(
        _fwd_kernel,
        grid=(NT,),
        in_specs=in_specs,
        out_specs=[pl.BlockSpec((TL, D), lambda s: (s, 0)),
                   pl.BlockSpec((TL * ROW_TILES, 128), lambda s: (s, 0)),
                   pl.BlockSpec((TL, ROUTE_W), lambda s: (s, 0)),
                   pl.BlockSpec((1, ROUTE_W), lambda s: (0, 0)),
                   pl.BlockSpec((None, N_HEADS, HD, HD), lambda s: (s, 0, 0, 0)),
                   pl.BlockSpec((None, 1, W), lambda s: (s, 0, 0))],
        out_shape=[jax.ShapeDtypeStruct((T, D), F32), jax.ShapeDtypeStruct((T * ROW_TILES, 128), F32),
                   jax.ShapeDtypeStruct((T, ROUTE_W), F32), jax.ShapeDtypeStruct((1, ROUTE_W), F32),
                   jax.ShapeDtypeStruct((NT, N_HEADS, HD, HD), F32),
                   jax.ShapeDtypeStruct((NT, 1, W), F32)],
        scratch_shapes=[pltpu.VMEM((N_HEADS, HD, HD), F32), pltpu.VMEM((1, W), F32)],
        compiler_params=pltpu.CompilerParams(dimension_semantics=("arbitrary",),
                                             vmem_limit_bytes=VMEM_LIMIT),
        name="forward_pass",
    )(x, mod, *([z] * 21), o_b, h_b, G, lev, lb, hg_norm, conv_a, cw, cb, wax, bax, lam,
      ln_g, ln_b, sg_w, sg_bt, w_br, w_out, norm2, w_r, b_r, hg_init, lru_init)


def _token_rows(ref, row, n=1, tiles=ROW_TILES):
    return ref.at[pl.ds(row * tiles, n * tiles), :]


def _read_rows(ref, n, tiles=ROW_TILES, first=0):
    return jnp.concatenate([ref[pl.ds(first + s, n, stride=tiles), :] for s in range(ROW_TILES)], axis=1)


def _write_rows(ref, y, tiles=ROW_TILES, first=0):
    for s in range(ROW_TILES):
        ref[pl.ds(first + s, y.shape[0], stride=tiles), :] = y[:, s * 128:(s + 1) * 128]


def _padded(n):
    return ((n + TME - 1) // TME) * TME


def _segment_starts(cnt_ref, seg_ref):
    seg = 0
    for c in range(N_CLASSES):
        seg_ref[c] = seg
        seg = seg + _padded(cnt_ref[c])
    return seg


def _token_slot(meta_ref, seg_ref, tok):
    return seg_ref[meta_ref[2 * tok]] + meta_ref[2 * tok + 1]


def _wait_rows(hbm, sem, n, tiles=ROW_TILES):
    pltpu.make_async_copy(_token_rows(hbm, 0, n, tiles), _token_rows(hbm, 0, n, tiles), sem).wait()


def _dispatch_kernel(meta_ref, cnt_ref, h2t_ref, xs_hbm, zero_buf, seg_ref, sem):
    s = pl.program_id(0)
    total = _segment_starts(cnt_ref, seg_ref)

    def start(i, c):
        row = _token_slot(meta_ref, seg_ref, s * TD + i)
        pltpu.make_async_copy(_token_rows(h2t_ref, i), _token_rows(xs_hbm, row), sem).start()
        return c

    lax.fori_loop(0, TD, start, 0, unroll=8)
    _wait_rows(xs_hbm, sem, TD)

    @pl.when(s == T // TD - 1)
    def _():
        zero_buf[...] = jnp.zeros(zero_buf.shape, F32)

        def pad_copy(j):
            return pltpu.make_async_copy(_token_rows(zero_buf, 0), _token_rows(xs_hbm, j), sem)

        def pad_start(j, c):
            pad_copy(j).start()
            return c

        def pad_wait(j, c):
            pad_copy(j).wait()
            return c

        def tail_copy(t):
            return pltpu.make_async_copy(zero_buf, _token_rows(xs_hbm, t * TME, TME), sem)

        def tail_start(t, c):
            tail_copy(t).start()
            return c

        def tail_wait(t, c):
            tail_copy(t).wait()
            return c

        for c in range(N_CLASSES):
            n = cnt_ref[c]
            lax.fori_loop(seg_ref[c] + n, seg_ref[c] + _padded(n), pad_start, 0)
            lax.fori_loop(seg_ref[c] + n, seg_ref[c] + _padded(n), pad_wait, 0)
        lax.fori_loop(total // TME, N_ETILES, tail_start, 0)
        lax.fori_loop(total // TME, N_ETILES, tail_wait, 0)


def _dispatch(meta, cnt, h2t):
    return pl.pallas_call(
        _dispatch_kernel,
        grid_spec=pltpu.PrefetchScalarGridSpec(
            num_scalar_prefetch=2, grid=(T // TD,),
            in_specs=[pl.BlockSpec((TD * ROW_TILES, 128), lambda s, meta, cnt: (s, 0))],
            out_specs=pl.BlockSpec(memory_space=pl.ANY),
            scratch_shapes=[pltpu.VMEM((TME * ROW_TILES, 128), F32), pltpu.SMEM((N_CLASSES,), jnp.int32),
                            pltpu.SemaphoreType.DMA]),
        out_shape=jax.ShapeDtypeStruct((N_ROWS * ROW_TILES, 128), F32),
        compiler_params=pltpu.CompilerParams(dimension_semantics=("arbitrary",),
                                             vmem_limit_bytes=VMEM_LIMIT),
        name="dispatch",
    )(meta, cnt, h2t)


def _expert_kernel(lo_ref, hi_ref, tv_ref, xs_ref, wg0, wu0, wd0, wg1, wu1, wd1, ys_ref,
                   sg0, su0, sd0, sg1, su1, sd1):
    i = pl.program_id(0)
    prev = jnp.maximum(i - 1, 0)

    @pl.when(tv_ref[i] == 0)
    def _():
        ys_ref[...] = jnp.zeros(ys_ref.shape, F32)

    @pl.when(tv_ref[i] == 1)
    def _():
        sets = ((lo_ref, (wg0, wu0, wd0), (sg0, su0, sd0)), (hi_ref, (wg1, wu1, wd1), (sg1, su1, sd1)))
        for e_ref, w_refs, s_refs in sets:
            @pl.when(jnp.logical_or(i == 0, e_ref[i] != e_ref[prev]))
            def _():
                for w, sc in zip(w_refs, s_refs):
                    sc[...] = w[...].astype(BF16)

        x = _read_rows(xs_ref, TME).astype(BF16)
        for k, (_, _, (sg, su, sd)) in enumerate(sets):
            a = _dot(x, sg[...])
            u = _dot(x, su[...])
            y = _dot((a * jax.nn.sigmoid(a) * u).astype(BF16), sd[...])
            _write_rows(ys_ref, y, 2 * ROW_TILES, k * ROW_TILES)


def _grouped_experts(l, e_lo, e_hi, tv, xs, w_gate, w_up, w_down):
    def wspec(shape, which):
        return pl.BlockSpec((None, None) + shape, lambda i, lo, hi, tv: (l, (lo, hi)[which][i], 0, 0))

    wspecs = [wspec(shape, which) for which in range(2)
              for shape in ((D, D_EXPERT), (D, D_EXPERT), (D_EXPERT, D))]
    return pl.pallas_call(
        _expert_kernel,
        grid_spec=pltpu.PrefetchScalarGridSpec(
            num_scalar_prefetch=3, grid=(N_ETILES,),
            in_specs=[pl.BlockSpec((TME * ROW_TILES, 128), lambda i, lo, hi, tv: (i, 0))] + wspecs,
            out_specs=pl.BlockSpec((TME * 2 * ROW_TILES, 128), lambda i, lo, hi, tv: (i, 0)),
            scratch_shapes=[pltpu.VMEM(shape, BF16) for _ in range(2)
                            for shape in ((D, D_EXPERT), (D, D_EXPERT), (D_EXPERT, D))]),
        out_shape=jax.ShapeDtypeStruct((N_ROWS * 2 * ROW_TILES, 128), F32),
        compiler_params=pltpu.CompilerParams(dimension_semantics=("arbitrary",),
                                             vmem_limit_bytes=VMEM_LIMIT),
        name="grouped_experts",
    )(e_lo, e_hi, tv, xs, w_gate, w_up, w_down, w_gate, w_up, w_down)


def _combine_kernel(meta_ref, cnt_ref, x_ref, mod_ref, route_ref, ys_hbm, o_ref, buf, seg_ref, sem):
    s = pl.program_id(0)
    _segment_starts(cnt_ref, seg_ref)

    def start(i, c):
        row = _token_slot(meta_ref, seg_ref, s * TD + i)
        pltpu.make_async_copy(_token_rows(ys_hbm, row, 1, 2 * ROW_TILES), _token_rows(buf, i, 1, 2 * ROW_TILES),
                              sem).start()
        return c

    lax.fori_loop(0, TD, start, 0, unroll=8)
    _wait_rows(ys_hbm, sem, TD, 2 * ROW_TILES)
    route = route_ref[...]
    y = (route[:, 2:3] * _read_rows(buf, TD, 2 * ROW_TILES, 0)
         + route[:, 3:4] * _read_rows(buf, TD, 2 * ROW_TILES, ROW_TILES))
    o_ref[...] = x_ref[...] + mod_ref[5:6, :] * y


def _combine(l, meta, cnt, x, mod, route, ys):
    return pl.pallas_call(
        _combine_kernel,
        grid_spec=pltpu.PrefetchScalarGridSpec(
            num_scalar_prefetch=2, grid=(T // TD,),
            in_specs=[pl.BlockSpec((TD, D), lambda s, meta, cnt: (s, 0)),
                      pl.BlockSpec((None, None, 6, D), lambda s, meta, cnt: (l, _seq_idx(s * (TD // TL)), 0, 0)),
                      pl.BlockSpec((TD, ROUTE_W), lambda s, meta, cnt: (s, 0)),
                      pl.BlockSpec(memory_space=pl.ANY)],
            out_specs=pl.BlockSpec((TD, D), lambda s, meta, cnt: (s, 0)),
            scratch_shapes=[pltpu.VMEM((TD * 2 * ROW_TILES, 128), F32), pltpu.SMEM((N_CLASSES,), jnp.int32),
                            pltpu.SemaphoreType.DMA]),
        out_shape=jax.ShapeDtypeStruct((T, D), F32),
        compiler_params=pltpu.CompilerParams(dimension_semantics=("arbitrary",),
                                             vmem_limit_bytes=VMEM_LIMIT),
        name="combine",
    )(meta, cnt, x, mod, route, ys)


def _expert_pass(l, x, mod, h2t, route, cnt, w_gate, w_up, w_down):
    meta = route[:, 0:2].astype(jnp.int32).reshape(-1)
    cnt = cnt[0, :N_CLASSES].astype(jnp.int32)
    ends = jnp.cumsum((cnt + TME - 1) // TME)
    i = jnp.arange(N_ETILES, dtype=jnp.int32)
    tv = (i < ends[-1]).astype(jnp.int32)
    tc = jnp.sum(ends[None, :] <= jnp.minimum(i, ends[-1] - 1)[:, None], axis=1, dtype=jnp.int32)
    grp, pair = tc // PAIRS, tc % PAIRS
    e_lo = grp * EXP_PER_GROUP + jnp.asarray(PAIR_LO, jnp.int32)[pair]
    e_hi = grp * EXP_PER_GROUP + jnp.asarray(PAIR_HI, jnp.int32)[pair]
    xs = _dispatch(meta, cnt, h2t)
    ys = _grouped_experts(l, e_lo, e_hi, tv, xs, w_gate, w_up, w_down)
    return _combine(l, meta, cnt, x, mod, route, ys)


def _final_kernel(x_ref, g_ref, o_ref):
    x = x_ref[...]
    o_ref[...] = x * lax.rsqrt(jnp.mean(x * x, axis=-1, keepdims=True) + EPS) * g_ref[...]


def _final_norm(x, g, first_row, rows):
    tm = 1024
    return pl.pallas_call(
        _final_kernel,
        grid=(rows // tm,),
        in_specs=[pl.BlockSpec((tm, D), lambda i: (first_row // tm + i, 0)), pl.BlockSpec((1, D), lambda i: (0, 0))],
        out_specs=pl.BlockSpec((tm, D), lambda i: (i, 0)),
        out_shape=jax.ShapeDtypeStruct((rows, D), F32),
        compiler_params=pltpu.CompilerParams(dimension_semantics=("arbitrary",)),
        name="final_norm",
    )(x, g.reshape(1, D))


def _grid_pos_embed(rows, d):
    nf = d // 4
    freqs = jnp.exp(-math.log(10000.0) * jnp.arange(nf, dtype=F32) / nf)
    r = jnp.arange(rows, dtype=F32)[:, None] * freqs
    cl = jnp.arange(GRID_W, dtype=F32)[:, None] * freqs
    r_emb = jnp.concatenate([jnp.sin(r), jnp.cos(r)], axis=-1)
    c_emb = jnp.concatenate([jnp.sin(cl), jnp.cos(cl)], axis=-1)
    emb = jnp.concatenate([jnp.broadcast_to(r_emb[:, None], (rows, GRID_W, d // 2)),
                           jnp.broadcast_to(c_emb[None], (rows, GRID_W, d // 2))], axis=-1)
    return emb.reshape(rows * GRID_W, d)


def _block_diag(w):
    n, c = w.shape[-3], w.shape[-1]
    eye = jnp.eye(n, dtype=w.dtype)
    full = w[..., :, :, None, :] * eye[:, None, :, None]
    return full.reshape(*w.shape[:-3], n * c, n * c)


def kernel(x_prompt, x_sample, state_hgrn, state_lru, c, c_ctx, w_ada, b_ada, norm1, norm2, w_in, conv_a,
           hg_lb, hg_norm, lru_conv_w, lru_conv_b, lru_wa, lru_ba, lru_wx, lru_bx, lru_lam, sg_ln_g, sg_ln_b,
           sg_w, sg_b, w_br, w_out, w_rg, b_rg, w_re, b_re, w_gate, w_up, w_down, norm_f):
    assert x_prompt.shape == (N_CTX, L_CTX, D) and x_sample.shape == (N_LAT, L_LAT, D)

    xs = x_sample.astype(F32) + _grid_pos_embed(L_LAT // GRID_W, D)
    x = jnp.concatenate([x_prompt.astype(F32).reshape(-1, D), xs.reshape(-1, D)], axis=0)

    cond = jnp.concatenate([c_ctx.astype(F32)[None], c.astype(F32), jnp.zeros((8 - 1 - N_LAT, D), F32)], axis=0)
    mod = _modulation(cond, w_ada, b_ada).reshape(DEPTH, 8, 6, D)

    cs = jnp.cumsum(jax.nn.softmax(hg_lb.astype(F32), axis=1), axis=1)
    lb = (cs - cs[:, :1]).reshape(2, DEPTH, 1, W)
    wax = jnp.concatenate([_block_diag(lru_wa.astype(BF16)), _block_diag(lru_wx.astype(BF16))], axis=-1)
    bax = jnp.concatenate([lru_ba, lru_bx], axis=-1).reshape(DEPTH, 2, 1, 2 * W)
    lam = lru_lam.reshape(DEPTH, 2, 1, W)
    row = lambda a: a.reshape(DEPTH, 1, a.shape[-1])
    sg_bt = jnp.swapaxes(sg_b, 1, 2)
    rpad = ROUTE_W - N_GROUPS - N_EXPERTS
    w_r = jnp.concatenate([w_rg, w_re, jnp.zeros((DEPTH, D, rpad), F32)], axis=-1)
    b_r = jnp.concatenate([b_rg, b_re, jnp.zeros((DEPTH, rpad), F32)], axis=-1)
    w_br_b = w_br.astype(BF16)
    w_out_b = w_out.astype(BF16)
    sg_w_b = sg_w.astype(BF16)

    consts = [_scan_constants(rev) for rev in (False, True)]
    G_f, G_b = (jnp.asarray(g, BF16) for g, _ in consts)
    lev_f, lev_b = (jnp.asarray(v) for _, v in consts)

    hg_init = jnp.concatenate([jnp.zeros((1,) + state_hgrn.shape[1:], F32), state_hgrn.astype(F32)], axis=0)
    lru_init = jnp.concatenate([jnp.zeros((1,) + state_lru.shape[1:], F32), state_lru.astype(F32)], axis=0)
    lru_init = lru_init.reshape(1 + N_LAT, DEPTH, 2, 1, W)

    hg_states, lru_states = [], []
    for l in range(DEPTH):
        z = _in_projection(l, x, mod, row(norm1), w_in)
        o_b, h_b, hgs_b, lrus_b = _reverse_pass(l, z, G_b, lev_b, lb, lru_conv_w, row(lru_conv_b), wax, bax, lam,
                                                hg_init, lru_init)
        x, h2t, route, cnt, hgs_f, lrus_f = _forward_pass(
            l, x, mod, z, o_b, h_b, G_f, lev_f, lb, row(hg_norm), conv_a, lru_conv_w, row(lru_conv_b), wax, bax,
            lam, row(sg_ln_g), row(sg_ln_b), sg_w_b, sg_bt, w_br_b, w_out_b, row(norm2), w_r, row(b_r),
            hg_init, lru_init)
        x = _expert_pass(l, x, mod, h2t, route, cnt, w_gate, w_up, w_down)
        hg_states.append(jnp.stack([hgs_f[:CTX_TILES], hgs_b[:CTX_TILES]], axis=1))
        lru_states.append(jnp.stack([lrus_f[:CTX_TILES, 0], lrus_b[:CTX_TILES, 0]], axis=1))

    n_ctx = N_CTX * L_CTX
    y_prompt = _final_norm(x, norm_f, 0, n_ctx).reshape(N_CTX, L_CTX, D)
    y_sample = _final_norm(x, norm_f, n_ctx, T - n_ctx).reshape(N_LAT, L_LAT, D)
    return y_prompt, y_sample, jnp.stack(hg_states, axis=1), jnp.stack(lru_states, axis=1)
```
